```python
import math
import jax, jax.numpy as jnp
from jax import lax
import numpy as np

D_MODEL = 2048
BATCH = 8
SEQ = 2048
DEPTH = 1
DEC_BATCH = 32
DEC_SEQ = 1
PAST_LEN = 8192
PAGE_SIZE = 128

ATT_DH = 64
ATT_HEADS = D_MODEL // (4 * ATT_DH)
ATT_WIDTH = ATT_HEADS * 2 * ATT_DH
ATT_SCALE = ATT_DH ** -0.5
Q_BLOCK = 128
SSD_HEADDIM = 64
SSD_HEADS = D_MODEL // (2 * SSD_HEADDIM)
SSD_WIDTH = SSD_HEADS * SSD_HEADDIM
SSD_GROUPS = 2
D_STATE = 128
CONV_W = 4
CONV_DIM = SSD_WIDTH + 2 * SSD_GROUPS * D_STATE
SSD_CHUNK = 128
MIX_WIDTH = ATT_WIDTH + SSD_WIDTH
IN_SPLITS = [ATT_WIDTH, 2 * ATT_WIDTH, 3 * ATT_WIDTH, 3 * ATT_WIDTH + SSD_WIDTH,
             3 * ATT_WIDTH + SSD_WIDTH + CONV_DIM]
IN_COLS = 3 * ATT_WIDTH + SSD_WIDTH + CONV_DIM + SSD_HEADS
N_EXPERTS = 32
TOP_K = 4
D_FF = D_MODEL
SWIGLU_ALPHA = 1.702
SWIGLU_LIMIT = 7.0
MOE_BLOCK = 128
DEEPNORM_ALPHA = (2.0 * DEPTH) ** 0.25
DEEPNORM_BETA = (8.0 * DEPTH) ** -0.25
LN_EPS = 1e-5
RMS_EPS = 1e-5

kernel_name = 'hybrid_diffattn_ssd_moe_step'


def layer_norm(x, g, b):
    xf = x.astype(jnp.float32)
    mu = jnp.mean(xf, axis=-1, keepdims=True)
    var = jnp.mean(jnp.square(xf - mu), axis=-1, keepdims=True)
    return ((xf - mu) * lax.rsqrt(var + LN_EPS) * g.astype(jnp.float32) + b.astype(jnp.float32)).astype(x.dtype)


def rms_norm(x, w):
    xf = x.astype(jnp.float32)
    return (xf * lax.rsqrt(jnp.mean(jnp.square(xf), axis=-1, keepdims=True) + RMS_EPS) * w.astype(jnp.float32)).astype(x.dtype)


def alibi_slopes():
    return 2.0 ** (-8.0 * jnp.arange(1, ATT_HEADS + 1, dtype=jnp.float32) / ATT_HEADS)


def diff_attention_block(q, q_pos, k_past, v_past, kpos_past, k_new, v_new, kpos_new, lam, slopes):
    def scores(k, kpos):
        s = jnp.einsum('bqhmd,bkhmd->bhmqk', q, k, preferred_element_type=jnp.float32) * ATT_SCALE
        dist = (q_pos[:, None] - kpos[None, :]).astype(jnp.float32)
        s = s - slopes[None, :, None, None, None] * dist
        return jnp.where(dist >= 0, s, -jnp.inf)
    n_past = k_past.shape[1]
    p = jax.nn.softmax(jnp.concatenate([scores(k_past, kpos_past), scores(k_new, kpos_new)], axis=-1), axis=-1)
    a = (p[:, :, 0] - lam * p[:, :, 1]).astype(v_new.dtype)
    return (jnp.einsum('bhqk,bkhe->bqhe', a[..., :n_past], v_past)
            + jnp.einsum('bhqk,bkhe->bqhe', a[..., n_past:], v_new))


def attend(q, q_pos, k_past, v_past, kpos_past, k_new, v_new, kpos_new, lam, slopes):
    b, tq = q.shape[:2]
    qb = Q_BLOCK if tq % Q_BLOCK == 0 else tq
    nb = tq // qb
    qs = jnp.moveaxis(q.reshape(b, nb, qb, ATT_HEADS, 2, ATT_DH), 1, 0)
    ps = q_pos.reshape(nb, qb)
    out = lax.map(lambda args: diff_attention_block(args[0], args[1], k_past, v_past, kpos_past,
                                                    k_new, v_new, kpos_new, lam, slopes), (qs, ps))
    return jnp.moveaxis(out, 0, 1).reshape(b, tq, ATT_HEADS, 2 * ATT_DH)


def segsum(a):
    cs = jnp.cumsum(a, axis=-1)
    l = a.shape[-1]
    mask = jnp.tril(jnp.ones((l, l), dtype=bool))
    return jnp.where(mask, cs[..., :, None] - cs[..., None, :], -jnp.inf)


def ssd_scan(xs, dt, a, bm, cm, h0):
    f32 = jnp.float32
    b, t, h, p = xs.shape
    cl = min(SSD_CHUNK, t)
    nc = -(-t // cl)
    pad = nc * cl - t
    rep = SSD_HEADS // SSD_GROUPS
    xd = xs.astype(f32) * dt[..., None]
    da = dt * a
    bh = jnp.repeat(bm.astype(f32), rep, axis=2)
    ch = jnp.repeat(cm.astype(f32), rep, axis=2)
    if pad:
        xd = jnp.pad(xd, ((0, 0), (0, pad), (0, 0), (0, 0)))
        bh = jnp.pad(bh, ((0, 0), (0, pad), (0, 0), (0, 0)))
        ch = jnp.pad(ch, ((0, 0), (0, pad), (0, 0), (0, 0)))
        da = jnp.pad(da, ((0, 0), (0, pad), (0, 0)))
    xd = xd.reshape(b, nc, cl, h, p)
    bh = bh.reshape(b, nc, cl, h, D_STATE)
    ch = ch.reshape(b, nc, cl, h, D_STATE)
    da = jnp.transpose(da.reshape(b, nc, cl, h), (0, 3, 1, 2))
    cs = jnp.cumsum(da, axis=-1)
    y_diag = jnp.einsum('bclhn,bcshn,bhcls,bcshp->bclhp', ch, bh, jnp.exp(segsum(da)), xd)
    decay_states = jnp.exp(cs[..., -1:] - cs)
    states = jnp.einsum('bclhn,bhcl,bclhp->bchpn', bh, decay_states, xd)
    states = jnp.concatenate([h0.astype(f32)[:, None], states], axis=1)
    chunk_decay = jnp.exp(segsum(jnp.pad(cs[..., -1], ((0, 0), (0, 0), (1, 0)))))
    states = jnp.einsum('bhzc,bchpn->bzhpn', chunk_decay, states)
    y_off = jnp.einsum('bclhn,bchpn,bhcl->bclhp', ch, states[:, :-1], jnp.exp(cs))
    y = (y_diag + y_off).reshape(b, nc * cl, h, p)[:, :t]
    return y.astype(xs.dtype), states[:, -1].astype(h0.dtype)


def moe(x, w_router, b_router, w_mlp1, b_mlp1, w_mlp2, b_mlp2):
    b, t, d = x.shape
    n_tok = b * t
    xt = x.reshape(n_tok, d)
    logits = jnp.einsum('td,de->te', xt, w_router).astype(jnp.float32) + b_router.astype(jnp.float32)
    top_v, top_i = lax.top_k(logits, TOP_K)
    gates = jax.nn.softmax(top_v, axis=-1)
    m = n_tok * TOP_K
    flat_e = top_i.reshape(-1).astype(jnp.int32)
    flat_tok = jnp.repeat(jnp.arange(n_tok, dtype=jnp.int32), TOP_K)
    flat_g = gates.reshape(-1)
    order = jnp.argsort(flat_e)
    e_sorted = flat_e[order]
    counts = jnp.zeros((N_EXPERTS,), jnp.int32).at[flat_e].add(1)
    padded = (counts + MOE_BLOCK - 1) // MOE_BLOCK * MOE_BLOCK
    start = jnp.cumsum(counts) - counts
    pend = jnp.cumsum(padded)
    pstart = pend - padded
    dest = pstart[e_sorted] + jnp.arange(m, dtype=jnp.int32) - start[e_sorted]
    n_rows = -(-m // MOE_BLOCK) * MOE_BLOCK + N_EXPERTS * MOE_BLOCK
    n_blk = n_rows // MOE_BLOCK
    row_tok = jnp.full((n_rows,), n_tok, jnp.int32).at[dest].set(flat_tok[order])
    row_gate = jnp.zeros((n_rows,), jnp.float32).at[dest].set(flat_g[order])
    blk_expert = jnp.minimum(jnp.searchsorted(pend, jnp.arange(n_blk, dtype=jnp.int32) * MOE_BLOCK, side='right'),
                             N_EXPERTS - 1).astype(jnp.int32)
    x_rows = jnp.concatenate([xt, jnp.zeros((1, d), xt.dtype)], axis=0)[row_tok].reshape(n_blk, MOE_BLOCK, d)

    def expert_block(args):
        xb, e = args
        hdn = xb @ w_mlp1[e] + b_mlp1[e]
        glu, lin = jnp.split(hdn, 2, axis=-1)
        glu = jnp.minimum(glu, SWIGLU_LIMIT)
        lin = jnp.clip(lin, -SWIGLU_LIMIT, SWIGLU_LIMIT)
        act = glu * jax.nn.sigmoid(SWIGLU_ALPHA * glu) * (lin + 1.0)
        return act @ w_mlp2[e] + b_mlp2[e]

    y_rows = lax.map(expert_block, (x_rows, blk_expert)).reshape(n_rows, d)
    y = jax.ops.segment_sum(y_rows.astype(jnp.float32) * row_gate[:, None], row_tok, num_segments=n_tok + 1)[:n_tok]
    return y.astype(x.dtype).reshape(b, t, d)


def hybrid_layer(x, q_pos, k_past, v_past, kpos_past, ssm_init, conv_init, lam_init,
                 w_in, w_out, lambda_qk, attn_subln_w, conv_w, conv_b, dt_bias, a_log, d_skip, ssd_norm_w,
                 ln1_g, ln1_b, w_router, b_router, w_mlp1, b_mlp1, w_mlp2, b_mlp2, ln2_g, ln2_b):
    b, t, _ = x.shape
    proj = jnp.einsum('btd,dc->btc', x, w_in)
    q, k, v, z, xbc, dt = jnp.split(proj, IN_SPLITS, axis=-1)
    q = q.reshape(b, t, ATT_HEADS, 2, ATT_DH)
    k = k.reshape(b, t, ATT_HEADS, 2, ATT_DH)
    v = v.reshape(b, t, ATT_HEADS, 2 * ATT_DH)
    lq = lambda_qk.astype(jnp.float32)
    lam = jnp.exp(jnp.sum(lq[0] * lq[1])) - jnp.exp(jnp.sum(lq[2] * lq[3])) + lam_init
    att = attend(q, q_pos, k_past, v_past, kpos_past, k, v, q_pos, lam, alibi_slopes())
    att = (rms_norm(att, attn_subln_w) * (1.0 - lam_init)).reshape(b, t, ATT_WIDTH)
    xbc_full = jnp.concatenate([conv_init.astype(xbc.dtype), xbc], axis=1)
    conv = conv_b + xbc_full[:, 0:t] * conv_w[0]
    for i in range(1, CONV_W):
        conv = conv + xbc_full[:, i:i + t] * conv_w[i]
    xbc_act = jax.nn.silu(conv)
    xs, bm, cm = jnp.split(xbc_act, [SSD_WIDTH, SSD_WIDTH + SSD_GROUPS * D_STATE], axis=-1)
    xs = xs.reshape(b, t, SSD_HEADS, SSD_HEADDIM)
    bm = bm.reshape(b, t, SSD_GROUPS, D_STATE)
    cm = cm.reshape(b, t, SSD_GROUPS, D_STATE)
    dtp = jax.nn.softplus(dt.astype(jnp.float32) + dt_bias.astype(jnp.float32))
    a = -jnp.exp(a_log.astype(jnp.float32))
    y, ssm_new = ssd_scan(xs, dtp, a, bm, cm, ssm_init)
    y = y + d_skip[:, None].astype(y.dtype) * xs
    yg = (y.reshape(b, t, SSD_WIDTH) * jax.nn.silu(z)).reshape(b, t, SSD_GROUPS, SSD_WIDTH // SSD_GROUPS)
    ssd_out = rms_norm(yg, ssd_norm_w.reshape(SSD_GROUPS, SSD_WIDTH // SSD_GROUPS)).reshape(b, t, SSD_WIDTH)
    mix = jnp.einsum('btc,cd->btd', jnp.concatenate([att, ssd_out], axis=-1), w_out)
    h = layer_norm(DEEPNORM_ALPHA * x + mix, ln1_g, ln1_b)
    out = layer_norm(DEEPNORM_ALPHA * h + moe(h, w_router, b_router, w_mlp1, b_mlp1, w_mlp2, b_mlp2), ln2_g, ln2_b)
    return out, k.reshape(b, t, ATT_HEADS, 2 * ATT_DH), v, ssm_new, xbc_full[:, t:]


def setup_inputs(seed: int = 0) -> dict:
    key = jax.random.key(seed)
    ks = jax.random.split(key, 32)
    f32 = jnp.float32
    n_pages = PAST_LEN // PAGE_SIZE
    n_used = DEC_BATCH * n_pages
    n_pool = n_used + n_used // 4
    nrm = lambda k, shape, s: jax.random.normal(k, shape, f32) * s
    page_table = jax.random.permutation(ks[3], n_pool)[:n_used].reshape(DEC_BATCH, n_pages).astype(jnp.int32)
    dt0 = jnp.exp(jax.random.uniform(ks[12], (DEPTH, SSD_HEADS), f32) * (math.log(0.1) - math.log(0.001)) + math.log(0.001))
    return {
        'x_prompt': nrm(ks[0], (BATCH, SEQ, D_MODEL), 1.0),
        'x_sample': nrm(ks[1], (DEC_BATCH, DEC_SEQ, D_MODEL), 1.0),
        'cache_k': nrm(ks[2], (DEPTH, n_pool, PAGE_SIZE, ATT_HEADS, 2 * ATT_DH), 1.0),
        'cache_v': nrm(ks[4], (DEPTH, n_pool, PAGE_SIZE, ATT_HEADS, 2 * ATT_DH), 1.0),
        'page_table': page_table,
        'state_ssm': nrm(ks[5], (DEPTH, DEC_BATCH, SSD_HEADS, SSD_HEADDIM, D_STATE), 0.5),
        'state_conv': nrm(ks[6], (DEPTH, DEC_BATCH, CONV_W - 1, CONV_DIM), 1.0),
        'w_in': nrm(ks[7], (DEPTH, D_MODEL, IN_COLS), D_MODEL ** -0.5),
        'w_out': nrm(ks[8], (DEPTH, MIX_WIDTH, D_MODEL), MIX_WIDTH ** -0.5 * DEEPNORM_BETA),
        'lambda_qk': nrm(ks[9], (DEPTH, 4, ATT_DH), 0.1),
        'attn_subln_w': 1.0 + nrm(ks[10], (DEPTH, 2 * ATT_DH), 0.02),
        'conv_w': nrm(ks[11], (DEPTH, CONV_W, CONV_DIM), CONV_W ** -0.5),
        'conv_b': nrm(ks[13], (DEPTH, CONV_DIM), 0.01),
        'dt_bias': dt0 + jnp.log(-jnp.expm1(-dt0)),
        'a_log': jnp.log(jax.random.uniform(ks[14], (DEPTH, SSD_HEADS), f32, 1.0, 16.0)),
        'd_skip': 1.0 + nrm(ks[15], (DEPTH, SSD_HEADS), 0.1),
        'ssd_norm_w': 1.0 + nrm(ks[16], (DEPTH, SSD_WIDTH), 0.02),
        'ln1_g': 1.0 + nrm(ks[17], (DEPTH, D_MODEL), 0.02),
        'ln1_b': nrm(ks[18], (DEPTH, D_MODEL), 0.01),
        'w_router': nrm(ks[19], (DEPTH, D_MODEL, N_EXPERTS), D_MODEL ** -0.5),
        'b_router': nrm(ks[20], (DEPTH, N_EXPERTS), 0.01),
        'w_mlp1': nrm(ks[21], (DEPTH, N_EXPERTS, D_MODEL, 2 * D_FF), D_MODEL ** -0.5),
        'b_mlp1': nrm(ks[22], (DEPTH, N_EXPERTS, 2 * D_FF), 0.01),
        'w_mlp2': nrm(ks[23], (DEPTH, N_EXPERTS, D_FF, D_MODEL), D_FF ** -0.5 * DEEPNORM_BETA),
        'b_mlp2': nrm(ks[24], (DEPTH, N_EXPERTS, D_MODEL), 0.01),
        'ln2_g': 1.0 + nrm(ks[25], (DEPTH, D_MODEL), 0.02),
        'ln2_b': nrm(ks[26], (DEPTH, D_MODEL), 0.01),
    }


def reference(x_prompt, x_sample, cache_k, cache_v, page_table, state_ssm, state_conv,
              w_in, w_out, lambda_qk, attn_subln_w, conv_w, conv_b, dt_bias, a_log, d_skip, ssd_norm_w,
              ln1_g, ln1_b, w_router, b_router, w_mlp1, b_mlp1, w_mlp2, b_mlp2, ln2_g, ln2_b):
    bp, sp = x_prompt.shape[:2]
    bd, sd = x_sample.shape[:2]
    n_pages = page_table.shape[1]
    past_len = n_pages * PAGE_SIZE
    pos_prompt = jnp.arange(sp, dtype=jnp.int32)
    pos_sample = past_len + jnp.arange(sd, dtype=jnp.int32)
    pos_past = jnp.arange(past_len, dtype=jnp.int32)
    k_empty = jnp.zeros((bp, 0, ATT_HEADS, 2, ATT_DH), x_prompt.dtype)
    v_empty = jnp.zeros((bp, 0, ATT_HEADS, 2 * ATT_DH), x_prompt.dtype)
    pos_empty = jnp.zeros((0,), jnp.int32)
    ssm_zero = jnp.zeros((bp, SSD_HEADS, SSD_HEADDIM, D_STATE), state_ssm.dtype)
    conv_zero = jnp.zeros((bp, CONV_W - 1, CONV_DIM), state_conv.dtype)
    yp, ys = x_prompt, x_sample
    kp_l, vp_l, sp_l, cp_l, ks_l, vs_l, ss_l, cs_l = [], [], [], [], [], [], [], []
    for l in range(DEPTH):
        lam_init = 0.8 - 0.6 * math.exp(-0.3 * l)
        params = (w_in[l], w_out[l], lambda_qk[l], attn_subln_w[l], conv_w[l], conv_b[l], dt_bias[l], a_log[l],
                  d_skip[l], ssd_norm_w[l], ln1_g[l], ln1_b[l], w_router[l], b_router[l], w_mlp1[l], b_mlp1[l],
                  w_mlp2[l], b_mlp2[l], ln2_g[l], ln2_b[l])
        yp, kp, vp, ssp, cvp = hybrid_layer(yp, pos_prompt, k_empty, v_empty, pos_empty, ssm_zero, conv_zero,
                                            lam_init, *params)
        k_past = cache_k[l][page_table].reshape(bd, past_len, ATT_HEADS, 2, ATT_DH)
        v_past = cache_v[l][page_table].reshape(bd, past_len, ATT_HEADS, 2 * ATT_DH)
        ys, kss, vss, sss, cvs = hybrid_layer(ys, pos_sample, k_past, v_past, pos_past, state_ssm[l], state_conv[l],
                                              lam_init, *params)
        kp_l.append(kp); vp_l.append(vp); sp_l.append(ssp); cp_l.append(cvp)
        ks_l.append(kss); vs_l.append(vss); ss_l.append(sss); cs_l.append(cvs)
    return (yp, ys, jnp.stack(kp_l), jnp.stack(vp_l), jnp.stack(sp_l), jnp.stack(cp_l),
            jnp.stack(ks_l), jnp.stack(vs_l), jnp.stack(ss_l), jnp.stack(cs_l))
```

```python
import functools
import math

import jax
import jax.numpy as jnp
from jax import lax
from jax.experimental import pallas as pl
from jax.experimental.pallas import tpu as pltpu

F32 = jnp.float32
BF16 = jnp.bfloat16

ATT_DH = 64
ATT_HEADS = 8
ATT_WIDTH = ATT_HEADS * 2 * ATT_DH
ATT_SCALE = ATT_DH ** -0.5
SSD_HEADDIM = 64
SSD_HEADS = 16
SSD_WIDTH = SSD_HEADS * SSD_HEADDIM
SSD_GROUPS = 2
D_STATE = 128
CONV_W = 4
CONV_DIM = SSD_WIDTH + 2 * SSD_GROUPS * D_STATE
SSD_CHUNK = 128
N_EXPERTS = 32
TOP_K = 4
SWIGLU_ALPHA = 1.702
SWIGLU_LIMIT = 7.0
LN_EPS = 1e-5
RMS_EPS = 1e-5
PAGE_SIZE = 128

V7X_VMEM_LIMIT_BYTES = 56 * 1024 * 1024
LANES = 128

MOE_ROW_TILE = 1024
MOE_SUB = 256
MOE_F_TILE = 256


def _params(sem, vmem=V7X_VMEM_LIMIT_BYTES):
    return pltpu.CompilerParams(dimension_semantics=sem, vmem_limit_bytes=vmem)


def _sigmoid(x):
    return 1.0 / (1.0 + jnp.exp(-x))


def _silu(x):
    return x * _sigmoid(x)


def _softplus(x):
    return jnp.maximum(x, 0.0) + jnp.log1p(jnp.exp(-jnp.abs(x)))


def _layer_norm(x, g, b):
    mu = jnp.mean(x, axis=-1, keepdims=True)
    xc = x - mu
    var = jnp.mean(xc * xc, axis=-1, keepdims=True)
    return xc * lax.rsqrt(var + LN_EPS) * g + b


def _split3(x):
    a = x.astype(BF16)
    r = x - a.astype(F32)
    b = r.astype(BF16)
    c = (r - b.astype(F32)).astype(BF16)
    return a, b, c


def _lam(lq):
    s01 = jnp.sum(lq[0:1, :] * lq[1:2, :], axis=-1, keepdims=True)
    s23 = jnp.sum(lq[2:3, :] * lq[3:4, :], axis=-1, keepdims=True)
    return jnp.exp(s01) - jnp.exp(s23)


def _cast_kernel(w_ref, o_ref):
    o_ref[...] = w_ref[...].astype(BF16)


def _cast_bf16(w, row_tile=256):
    r, c = w.shape
    rt = min(row_tile, r)
    return pl.pallas_call(
        _cast_kernel,
        grid=(r // rt,),
        in_specs=[pl.BlockSpec((rt, c), lambda i: (i, 0))],
        out_specs=pl.BlockSpec((rt, c), lambda i: (i, 0)),
        out_shape=jax.ShapeDtypeStruct((r, c), BF16),
        compiler_params=_params(("arbitrary",)),
    )(w)


def _inproj_kernel(x_ref, w_ref, q_ref, k_ref, v_ref, z_ref, xbc_ref, dt_ref):
    xb = x_ref[...].astype(BF16)

    def mm(c0, c1):
        return jnp.dot(xb, w_ref[:, c0:c1], preferred_element_type=F32)

    a = ATT_WIDTH
    q_ref[...] = (mm(0, a) * ATT_SCALE).astype(q_ref.dtype)
    k_ref[...] = mm(a, 2 * a)
    v_ref[...] = mm(2 * a, 3 * a)
    z_ref[...] = mm(3 * a, 3 * a + SSD_WIDTH)
    c0 = 3 * a + SSD_WIDTH
    xbc_ref[...] = mm(c0, c0 + CONV_DIM)
    dt_ref[...] = mm(c0 + CONV_DIM, c0 + CONV_DIM + SSD_HEADS)


def _inproj(x2d, w_in_bf16, q_dtype):
    m, d = x2d.shape
    ncol = w_in_bf16.shape[1]
    tm = min(256, m)
    row = lambda i: (i, 0)
    widths = (ATT_WIDTH, ATT_WIDTH, ATT_WIDTH, SSD_WIDTH, CONV_DIM, SSD_HEADS)
    dtypes = (q_dtype, F32, F32, F32, F32, F32)
    return pl.pallas_call(
        _inproj_kernel,
        grid=(m // tm,),
        in_specs=[pl.BlockSpec((tm, d), row),
                  pl.BlockSpec((d, ncol), lambda i: (0, 0), pipeline_mode=pl.Buffered(1))],
        out_specs=[pl.BlockSpec((tm, w), row) for w in widths],
        out_shape=[jax.ShapeDtypeStruct((m, w), dt) for w, dt in zip(widths, dtypes)],
        compiler_params=_params(("arbitrary",)),
    )(x2d, w_in_bf16)


def _attn_prompt_kernel(slopes_ref, q_ref, k_ref, v_ref, lq_ref, w_ref, o_ref, kb_ref, vb_ref, *, tq, lam_init):
    h = pl.program_id(1)
    qi = pl.program_id(2)

    @pl.when(qi == 0)
    def _():
        kb_ref[...] = k_ref[0].astype(BF16)
        vb_ref[...] = v_ref[0].astype(BF16)

    slope = slopes_ref[h]
    q = q_ref[0]
    lane = lax.broadcasted_iota(jnp.int32, (tq, 2 * ATT_DH), 1)
    zero = jnp.zeros_like(q)
    qs = (jnp.where(lane < ATT_DH, q, zero), jnp.where(lane >= ATT_DH, q, zero))
    r = lax.broadcasted_iota(jnp.int32, (tq, tq), 0)
    c = lax.broadcasted_iota(jnp.int32, (tq, tq), 1)
    base = (r - c).astype(F32) * slope
    future = c > r

    def block(j, carry, diag):
        kb = kb_ref[pl.ds(pl.multiple_of(j * tq, tq), tq), :]
        vb = vb_ref[pl.ds(pl.multiple_of(j * tq, tq), tq), :]
        off = jnp.full((1, 1), (qi - j) * tq, jnp.int32).astype(F32) * slope
        bias = base + off
        out = []
        for mi in range(2):
            m, l, a = carry[mi]
            s = lax.dot_general(qs[mi], kb, (((1,), (1,)), ((), ())), preferred_element_type=F32) - bias
            if diag:
                s = jnp.where(future, -jnp.inf, s)
            mn = jnp.maximum(m, jnp.max(s, axis=-1, keepdims=True))
            p = jnp.exp(s - mn)
            al = jnp.exp(m - mn)
            l = al * l + jnp.sum(p, axis=-1, keepdims=True)
            a = al * a + jnp.dot(p.astype(BF16), vb, preferred_element_type=F32)
            out.append((mn, l, a))
        return tuple(out)

    init1 = (jnp.full((tq, 1), -1e30, F32), jnp.zeros((tq, 1), F32), jnp.zeros((tq, 2 * ATT_DH), F32))
    carry = lax.fori_loop(0, qi, lambda j, cr: block(j, cr, False), (init1, init1))
    (_, l0, a0), (_, l1, a1) = block(qi, carry, True)
    lam = _lam(lq_ref[...]) + lam_init
    o = a0 / l0 - lam * (a1 / l1)
    o = o * lax.rsqrt(jnp.mean(o * o, axis=-1, keepdims=True) + RMS_EPS) * w_ref[...] * (1.0 - lam_init)
    o_ref[0] = o.astype(BF16)


def _attn_prompt(q, k, v, lambda_qk, subln_w, slopes, lam_init):
    b, t, _ = k.shape
    tq = min(256, t)
    hd = 2 * ATT_DH
    kern = functools.partial(_attn_prompt_kernel, tq=tq, lam_init=lam_init)
    return pl.pallas_call(
        kern,
        grid=(b, ATT_HEADS, t // tq),
        in_specs=[pl.BlockSpec(memory_space=pltpu.SMEM),
                  pl.BlockSpec((1, tq, hd), lambda bi, h, qi: (bi, qi, h)),
                  pl.BlockSpec((1, t, hd), lambda bi, h, qi: (bi, 0, h)),
                  pl.BlockSpec((1, t, hd), lambda bi, h, qi: (bi, 0, h)),
                  pl.BlockSpec((4, ATT_DH), lambda bi, h, qi: (0, 0)),
                  pl.BlockSpec((1, hd), lambda bi, h, qi: (0, 0))],
        out_specs=pl.BlockSpec((1, tq, hd), lambda bi, h, qi: (bi, qi, h)),
        out_shape=jax.ShapeDtypeStruct((b, t, ATT_WIDTH), BF16),
        scratch_shapes=[pltpu.VMEM((t, hd), BF16), pltpu.VMEM((t, hd), BF16)],
        compiler_params=_params(("arbitrary", "arbitrary", "arbitrary")),
    )(slopes, q, k, v, lambda_qk, subln_w.reshape(1, hd))


def _attn_sample_kernel(pt_ref, q_ref, kn_ref, vn_ref, slope_ref, lq_ref, w_ref, kp_ref, vp_ref, o_ref,
                        qbd_ref, s_ref, a_ref, m_ref, snew_ref, anew_ref, acc_ref, *, past_len, lam_init):
    ph = pl.program_id(1)
    p = pl.program_id(2)
    n_pages = pl.num_programs(2)
    nh = ATT_HEADS
    nrow = 2 * nh
    col = pl.ds(pl.multiple_of(p * PAGE_SIZE, PAGE_SIZE), PAGE_SIZE)

    @pl.when((ph == 0) & (p == 0))
    def _():
        rowi = lax.broadcasted_iota(jnp.int32, (nrow, ATT_WIDTH), 0)
        lanei = lax.broadcasted_iota(jnp.int32, (nrow, ATT_WIDTH), 1)
        qf = jnp.broadcast_to(q_ref[0].astype(F32), (nrow, ATT_WIDTH))
        qbd = jnp.where((lanei // ATT_DH) == 2 * (rowi % nh) + rowi // nh, qf, 0.0)
        qbd_ref[...] = qbd.astype(BF16)
        s_new = jnp.sum(qbd * kn_ref[0].astype(BF16).astype(F32), axis=-1, keepdims=True)
        snew_ref[...] = s_new
        m_ref[...] = s_new

    @pl.when(ph == 0)
    def _():
        kb = kp_ref[0].astype(BF16)
        s = lax.dot_general(qbd_ref[...], kb, (((1,), (1,)), ((), ())), preferred_element_type=F32)
        pos = p * PAGE_SIZE + lax.broadcasted_iota(jnp.int32, (nrow, PAGE_SIZE), 1)
        s = s - slope_ref[...] * (past_len - pos).astype(F32)
        s_ref[:, col] = s
        m_ref[...] = jnp.maximum(m_ref[...], jnp.max(s, axis=-1, keepdims=True))

    @pl.when((ph == 1) & (p == 0))
    def _():
        m = m_ref[...]
        e = jnp.exp(s_ref[...] - m)
        e_new = jnp.exp(snew_ref[...] - m)
        den = jnp.sum(e, axis=-1, keepdims=True) + e_new
        lam = _lam(lq_ref[...]) + lam_init
        pn = e / den
        a_ref[0:nh, :] = pn[0:nh] - lam * pn[nh:nrow]
        a_ref[nh:nrow, :] = jnp.zeros((nh, a_ref.shape[1]), F32)
        pn_new = e_new / den
        anew_ref[...] = pn_new[0:nh] - lam * pn_new[nh:nrow]
        acc_ref[...] = jnp.zeros_like(acc_ref)

    @pl.when(ph == 1)
    def _():
        vb = vp_ref[0].astype(BF16)
        acc_ref[...] += jnp.dot(a_ref[:, col].astype(BF16), vb, preferred_element_type=F32)

    @pl.when((ph == 1) & (p == n_pages - 1))
    def _():
        hd = 2 * ATT_DH
        for h in range(nh):
            o = acc_ref[h:h + 1, h * hd:(h + 1) * hd] + anew_ref[h:h + 1, :] * vn_ref[0, :, h * hd:(h + 1) * hd]
            o = o * lax.rsqrt(jnp.mean(o * o, axis=-1, keepdims=True) + RMS_EPS) * w_ref[...] * (1.0 - lam_init)
            o_ref[0, :, h * hd:(h + 1) * hd] = o.astype(BF16)


def _attn_sample(q, k_new, v_new, cache_k, cache_v, page_table, lambda_qk, subln_w, slopes, lam_init):
    bd = q.shape[0]
    n_pages = page_table.shape[1]
    n_pool = cache_k.shape[0]
    nrow = 2 * ATT_HEADS
    hd = 2 * ATT_DH
    kp = cache_k.reshape(n_pool, PAGE_SIZE, ATT_WIDTH)
    vp = cache_v.reshape(n_pool, PAGE_SIZE, ATT_WIDTH)
    slope_rows = jnp.tile(slopes, 2).reshape(nrow, 1)
    past_len = n_pages * PAGE_SIZE
    kern = functools.partial(_attn_sample_kernel, past_len=past_len, lam_init=lam_init)
    row3 = lambda b, ph, p, pt: (b, 0, 0)
    const2 = lambda b, ph, p, pt: (0, 0)
    kpage = lambda b, ph, p, pt: (pt[b * n_pages + jnp.where(ph == 0, p, n_pages - 1)], 0, 0)
    vpage = lambda b, ph, p, pt: (pt[b * n_pages + jnp.where(ph == 0, 0, p)], 0, 0)
    grid_spec = pltpu.PrefetchScalarGridSpec(
        num_scalar_prefetch=1,
        grid=(bd, 2, n_pages),
        in_specs=[pl.BlockSpec((1, 1, ATT_WIDTH), row3),
                  pl.BlockSpec((1, 1, ATT_WIDTH), row3),
                  pl.BlockSpec((1, 1, ATT_WIDTH), row3),
                  pl.BlockSpec((nrow, 1), const2),
                  pl.BlockSpec((4, ATT_DH), const2),
                  pl.BlockSpec((1, hd), const2),
                  pl.BlockSpec((1, PAGE_SIZE, ATT_WIDTH), kpage),
                  pl.BlockSpec((1, PAGE_SIZE, ATT_WIDTH), vpage)],
        out_specs=pl.BlockSpec((1, 1, ATT_WIDTH), row3),
        scratch_shapes=[pltpu.VMEM((nrow, ATT_WIDTH), BF16), pltpu.VMEM((nrow, past_len), F32),
                        pltpu.VMEM((nrow, past_len), F32), pltpu.VMEM((nrow, 1), F32),
                        pltpu.VMEM((nrow, 1), F32), pltpu.VMEM((ATT_HEADS, 1), F32),
                        pltpu.VMEM((nrow, ATT_WIDTH), F32)],
    )
    out = pl.pallas_call(
        kern,
        grid_spec=grid_spec,
        out_shape=jax.ShapeDtypeStruct((bd, 1, ATT_WIDTH), BF16),
        compiler_params=_params(("arbitrary", "arbitrary", "arbitrary")),
    )(page_table.reshape(-1), q.reshape(bd, 1, ATT_WIDTH), k_new.reshape(bd, 1, ATT_WIDTH),
      v_new.reshape(bd, 1, ATT_WIDTH), slope_rows, lambda_qk, subln_w.reshape(1, hd), kp, vp)
    return out.reshape(bd, ATT_WIDTH)


def _attn_decode_kernel(pt_ref, q_ref, kn_ref, vn_ref, slope_ref, lq_ref, w_ref, *rest, pps, past_len, lam_init):
    kp_refs, vp_refs = rest[:pps], rest[pps:2 * pps]
    o_ref, qbd_ref, s_ref, a_ref, m_ref, snew_ref, anew_ref, acc_ref = rest[2 * pps:]
    ph = pl.program_id(1)
    p = pl.program_id(2)
    n_steps = pl.num_programs(2)
    nh = ATT_HEADS
    nrow = 2 * nh
    hd = 2 * ATT_DH

    @pl.when((ph == 0) & (p == 0))
    def _():
        rowi = lax.broadcasted_iota(jnp.int32, (nrow, ATT_WIDTH), 0)
        lanei = lax.broadcasted_iota(jnp.int32, (nrow, ATT_WIDTH), 1)
        qf = jnp.broadcast_to(q_ref[0], (nrow, ATT_WIDTH))
        qbd = jnp.where((lanei // ATT_DH) == 2 * (rowi % nh) + rowi // nh, qf, 0.0)
        qbd_ref[...] = qbd.astype(BF16)
        prod = qbd * kn_ref[0]
        ones = jnp.ones((8, ATT_WIDTH), BF16)
        s_new = sum(lax.dot_general(ones, part, (((1,), (1,)), ((), ())), preferred_element_type=F32)
                    for part in _split3(prod))[0:1, :]
        snew_ref[...] = s_new
        m_ref[...] = s_new

    @pl.when(ph == 0)
    def _():
        for i in range(pps):
            kb = kp_refs[i][0].astype(BF16)
            s = lax.dot_general(kb, qbd_ref[...], (((1,), (1,)), ((), ())), preferred_element_type=F32)
            page = p * pps + i
            pos = page * PAGE_SIZE + lax.broadcasted_iota(jnp.int32, (PAGE_SIZE, 1), 0)
            s = s - (past_len - pos).astype(F32) * slope_ref[...]
            s_ref[pl.ds(pl.multiple_of(page * PAGE_SIZE, PAGE_SIZE), PAGE_SIZE), :] = s
            m_ref[...] = jnp.maximum(m_ref[...], jnp.max(s, axis=0, keepdims=True))

    @pl.when((ph == 1) & (p == 0))
    def _():
        m = m_ref[...]
        e = jnp.exp(s_ref[...] - m)
        e_new = jnp.exp(snew_ref[...] - m)
        den = jnp.sum(e, axis=0, keepdims=True) + e_new
        lam = _lam(lq_ref[...]) + lam_init
        pn = e / den
        a_ref[...] = pn[:, 0:nh] - lam * pn[:, nh:nrow]
        pn_new = e_new / den
        anew_ref[...] = pn_new[:, 0:nh] - lam * pn_new[:, nh:nrow]
        acc_ref[...] = jnp.zeros_like(acc_ref)

    @pl.when(ph == 1)
    def _():
        for i in range(pps):
            page = p * pps + i
            rows = pl.ds(pl.multiple_of(page * PAGE_SIZE, PAGE_SIZE), PAGE_SIZE)
            for h in range(nh):
                acc_ref[:, h * hd:(h + 1) * hd] += a_ref[rows, h:h + 1] * vp_refs[i][0, :, h * hd:(h + 1) * hd]

    @pl.when((ph == 1) & (p == n_steps - 1))
    def _():
        for h in range(nh):
            o = (jnp.sum(acc_ref[:, h * hd:(h + 1) * hd], axis=0, keepdims=True)
                 + anew_ref[:, h:h + 1] * vn_ref[0, :, h * hd:(h + 1) * hd])
            o = o * lax.rsqrt(jnp.mean(o * o, axis=-1, keepdims=True) + RMS_EPS) * w_ref[...] * (1.0 - lam_init)
            o_ref[0, :, h * hd:(h + 1) * hd] = o.astype(BF16)


def _attn_decode(q, k_new, v_new, cache_k, cache_v, page_table, lambda_qk, subln_w, slopes, lam_init):
    bd = q.shape[0]
    n_pages = page_table.shape[1]
    n_pool = cache_k.shape[0]
    nrow = 2 * ATT_HEADS
    hd = 2 * ATT_DH
    pps = max(c for c in (1, 2, 4) if n_pages % c == 0)
    n_steps = n_pages // pps
    kp = cache_k.reshape(n_pool, PAGE_SIZE, ATT_WIDTH)
    vp = cache_v.reshape(n_pool, PAGE_SIZE, ATT_WIDTH)
    slope_cols = jnp.tile(slopes, 2).reshape(1, nrow)
    past_len = n_pages * PAGE_SIZE
    kern = functools.partial(_attn_decode_kernel, pps=pps, past_len=past_len, lam_init=lam_init)
    row3 = lambda b, ph, p, pt: (b, 0, 0)
    const2 = lambda b, ph, p, pt: (0, 0)

    def kpage(i):
        return lambda b, ph, p, pt: (pt[b * n_pages + jnp.where(ph == 0, p, n_steps - 1) * pps + i], 0, 0)

    def vpage(i):
        return lambda b, ph, p, pt: (pt[b * n_pages + jnp.where(ph == 0, 0, p) * pps + i], 0, 0)

    page_block = (1, PAGE_SIZE, ATT_WIDTH)
    grid_spec = pltpu.PrefetchScalarGridSpec(
        num_scalar_prefetch=1,
        grid=(bd, 2, n_steps),
        in_specs=[pl.BlockSpec((1, 1, ATT_WIDTH), row3),
                  pl.BlockSpec((1, 1, ATT_WIDTH), row3),
                  pl.BlockSpec((1, 1, ATT_WIDTH), row3),
                  pl.BlockSpec((1, nrow), const2),
                  pl.BlockSpec((4, ATT_DH), const2),
                  pl.BlockSpec((1, hd), const2)]
                 + [pl.BlockSpec(page_block, kpage(i)) for i in range(pps)]
                 + [pl.BlockSpec(page_block, vpage(i)) for i in range(pps)],
        out_specs=pl.BlockSpec((1, 1, ATT_WIDTH), row3),
        scratch_shapes=[pltpu.VMEM((nrow, ATT_WIDTH), BF16), pltpu.VMEM((past_len, nrow), F32),
                        pltpu.VMEM((past_len, ATT_HEADS), F32), pltpu.VMEM((1, nrow), F32),
                        pltpu.VMEM((1, nrow), F32), pltpu.VMEM((1, ATT_HEADS), F32),
                        pltpu.VMEM((PAGE_SIZE, ATT_WIDTH), F32)],
    )
    out = pl.pallas_call(
        kern,
        grid_spec=grid_spec,
        out_shape=jax.ShapeDtypeStruct((bd, 1, ATT_WIDTH), BF16),
        compiler_params=_params(("arbitrary", "arbitrary", "arbitrary")),
    )(page_table.reshape(-1), q.reshape(bd, 1, ATT_WIDTH), k_new.reshape(bd, 1, ATT_WIDTH),
      v_new.reshape(bd, 1, ATT_WIDTH), slope_cols, lambda_qk, subln_w.reshape(1, hd), *([kp] * pps), *([vp] * pps))
    return out.reshape(bd, ATT_WIDTH)


def _gated_group_norm(y, z, w):
    yg = y * _silu(z)
    gw = SSD_WIDTH // SSD_GROUPS
    parts = []
    for g in range(SSD_GROUPS):
        v = yg[:, g * gw:(g + 1) * gw]
        parts.append(v * lax.rsqrt(jnp.mean(v * v, axis=-1, keepdims=True) + RMS_EPS) * w[:, g * gw:(g + 1) * gw])
    return parts


def _ssd_prompt_kernel(xbc_ref, dt_ref, dtt_ref, z_ref, cw_ref, cb_ref, dtb_ref, dtbt_ref, al_ref, alt_ref,
                       dsk_ref, nw_ref, h0_ref, c0_ref, y_ref, st_ref, cv_ref, xpad_ref, ysc_ref, xdd_ref):
    c = pl.program_id(1)
    nc = pl.num_programs(1)
    L = SSD_CHUNK
    P = SSD_HEADDIM
    hpg = SSD_HEADS // SSD_GROUPS
    halo = 8

    @pl.when(c == 0)
    def _():
        st_ref[...] = h0_ref[...]
        xpad_ref[0:halo, :] = jnp.zeros((halo, CONV_DIM), F32)
        xpad_ref[halo - (CONV_W - 1):halo, :] = c0_ref[0]

    xc = xbc_ref[0]
    xpad_ref[halo:halo + L, :] = xc
    conv = cb_ref[...]
    for i in range(CONV_W - 1):
        sh = CONV_W - 1 - i
        conv = conv + xpad_ref[halo - sh:halo - sh + L, :] * cw_ref[i:i + 1, :]
    conv = conv + xc * cw_ref[CONV_W - 1:CONV_W, :]
    tail = xc[L - (CONV_W - 1):L, :]
    xpad_ref[halo - (CONV_W - 1):halo, :] = tail

    @pl.when(c == nc - 1)
    def _():
        cv_ref[0] = tail

    act = _silu(conv)
    xs = act[:, :SSD_WIDTH]
    bmat = [act[:, SSD_WIDTH + g * D_STATE:SSD_WIDTH + (g + 1) * D_STATE].astype(BF16) for g in range(SSD_GROUPS)]
    c_off = SSD_WIDTH + SSD_GROUPS * D_STATE
    cmat = [act[:, c_off + g * D_STATE:c_off + (g + 1) * D_STATE].astype(BF16) for g in range(SSD_GROUPS)]

    dtp = _softplus(dt_ref[0] + dtb_ref[...])
    dtpt = _softplus(dtt_ref[0] + dtbt_ref[...])
    da = dtp * (-jnp.exp(al_ref[...]))
    dat = dtpt * (-jnp.exp(alt_ref[...]))
    ri = lax.broadcasted_iota(jnp.int32, (L, L), 0)
    ci = lax.broadcasted_iota(jnp.int32, (L, L), 1)
    causal = ri >= ci
    tri = jnp.where(causal, 1.0, 0.0).astype(BF16)
    trit = jnp.where(ci >= ri, 1.0, 0.0).astype(BF16)
    cs = sum(jnp.dot(tri, part, preferred_element_type=F32) for part in _split3(da))
    cst = sum(jnp.dot(part, trit, preferred_element_type=F32) for part in _split3(dat))

    cb = [lax.dot_general(cmat[g], bmat[g], (((1,), (1,)), ((), ())), preferred_element_type=F32)
          for g in range(SSD_GROUPS)]
    dsk = dsk_ref[...]

    for h in range(SSD_HEADS):
        g = h // hpg
        cs_col = cs[:, h:h + 1]
        diff = cs_col - cst[h:h + 1, :]
        lmat = jnp.exp(jnp.where(causal, diff, -jnp.inf))
        mmat = (cb[g] * lmat).astype(BF16)
        xs_h = xs[:, h * P:(h + 1) * P]
        xd_h = xs_h * dtp[:, h:h + 1]
        y = jnp.dot(mmat, xd_h.astype(BF16), preferred_element_type=F32)
        st = st_ref[0, h]
        yoff = lax.dot_general(cmat[g], st.astype(BF16), (((1,), (1,)), ((), ())), preferred_element_type=F32)
        y = y + jnp.exp(cs_col) * yoff + dsk[:, h:h + 1] * xs_h
        ysc_ref[:, h * P:(h + 1) * P] = y
        cs_last = cs[L - 1:L, h:h + 1]
        xdd_ref[:, h * P:(h + 1) * P] = xd_h * jnp.exp(cs_last - cs_col)

    xddt = xdd_ref[...].T
    for h in range(SSD_HEADS):
        g = h // hpg
        new = jnp.dot(xddt[h * P:(h + 1) * P, :].astype(BF16), bmat[g], preferred_element_type=F32)
        cs_last = cs[L - 1:L, h:h + 1]
        st_ref[0, h] = jnp.exp(cs_last) * st_ref[0, h] + new

    parts = _gated_group_norm(ysc_ref[...], z_ref[0], nw_ref[...])
    gw = SSD_WIDTH // SSD_GROUPS
    for g in range(SSD_GROUPS):
        y_ref[0, :, g * gw:(g + 1) * gw] = parts[g].astype(BF16)


def _ssd_prompt(xbc, dt, z, h0, conv0, conv_w, conv_b, dt_bias, a_log, d_skip, norm_w):
    b, t, _ = xbc.shape
    L = SSD_CHUNK
    nc = t // L
    dtt = jnp.swapaxes(dt, 1, 2)
    seq = lambda bi, ci: (bi, ci, 0)
    const2 = lambda bi, ci: (0, 0)
    full2 = lambda shp: pl.BlockSpec(shp, const2)
    return pl.pallas_call(
        _ssd_prompt_kernel,
        grid=(b, nc),
        in_specs=[pl.BlockSpec((1, L, CONV_DIM), seq),
                  pl.BlockSpec((1, L, SSD_HEADS), seq),
                  pl.BlockSpec((1, SSD_HEADS, L), lambda bi, ci: (bi, 0, ci)),
                  pl.BlockSpec((1, L, SSD_WIDTH), seq),
                  full2((CONV_W, CONV_DIM)), full2((1, CONV_DIM)),
                  full2((1, SSD_HEADS)), full2((SSD_HEADS, 1)),
                  full2((1, SSD_HEADS)), full2((SSD_HEADS, 1)),
                  full2((1, SSD_HEADS)), full2((1, SSD_WIDTH)),
                  pl.BlockSpec((1, SSD_HEADS, SSD_HEADDIM, D_STATE), lambda bi, ci: (bi, 0, 0, 0)),
                  pl.BlockSpec((1, CONV_W - 1, CONV_DIM), lambda bi, ci: (bi, 0, 0))],
        out_specs=[pl.BlockSpec((1, L, SSD_WIDTH), seq),
                   pl.BlockSpec((1, SSD_HEADS, SSD_HEADDIM, D_STATE), lambda bi, ci: (bi, 0, 0, 0)),
                   pl.BlockSpec((1, CONV_W - 1, CONV_DIM), lambda bi, ci: (bi, 0, 0))],
        out_shape=[jax.ShapeDtypeStruct((b, t, SSD_WIDTH), BF16),
                   jax.ShapeDtypeStruct((b, SSD_HEADS, SSD_HEADDIM, D_STATE), F32),
                   jax.ShapeDtypeStruct((b, CONV_W - 1, CONV_DIM), F32)],
        scratch_shapes=[pltpu.VMEM((8 + L, CONV_DIM), F32), pltpu.VMEM((L, SSD_WIDTH), F32),
                        pltpu.VMEM((L, SSD_WIDTH), F32)],
        compiler_params=_params(("arbitrary", "arbitrary")),
    )(xbc, dt, dtt, z, conv_w, conv_b.reshape(1, -1), dt_bias.reshape(1, -1), dt_bias.reshape(-1, 1),
      a_log.reshape(1, -1), a_log.reshape(-1, 1), d_skip.reshape(1, -1), norm_w.reshape(1, -1), h0, conv0)


def _bf16_round(v):
    return v.astype(BF16).astype(F32)


def _ssd_sample_kernel(xbc_ref, ci_ref, dt_ref, z_ref, cw_ref, cb_ref, dtb_ref, al_ref, dsk_ref, nw_ref, h0_ref,
                       xbc8_ref, ci8_ref, dt8_ref, cw8_ref, cb8_ref, dtb8_ref, y_ref, st_ref, cv_ref, ysc_ref):
    P = SSD_HEADDIM
    hpg = SSD_HEADS // SSD_GROUPS
    xrow = xbc_ref[0]
    hist = ci_ref[0]
    conv = cb_ref[...]
    for i in range(CONV_W - 1):
        conv = conv + hist[i:i + 1, :] * cw_ref[i:i + 1, :]
    conv = conv + xrow * cw_ref[CONV_W - 1:CONV_W, :]
    cv_ref[0, 0:CONV_W - 2, :] = hist[1:CONV_W - 1, :]
    cv_ref[0, CONV_W - 2:CONV_W - 1, :] = xrow
    act = _silu(conv)
    xs = act[:, :SSD_WIDTH]
    c_off = SSD_WIDTH + SSD_GROUPS * D_STATE
    dtp = _softplus(dt_ref[0] + dtb_ref[...])
    decay = jnp.exp(dtp * (-jnp.exp(al_ref[...])))
    dsk = dsk_ref[...]

    nx = SSD_WIDTH // LANES
    conv8 = cb8_ref[0:nx, :]
    for i in range(CONV_W - 1):
        conv8 = conv8 + ci8_ref[0, i, 0:nx, :] * cw8_ref[i, 0:nx, :]
    conv8 = conv8 + xbc8_ref[0, 0:nx, :] * cw8_ref[CONV_W - 1, 0:nx, :]
    xd8 = _silu(conv8) * _softplus(dt8_ref[0] + dtb8_ref[...])
    xdt = xd8.T

    brows = [act[:, SSD_WIDTH + g * D_STATE:SSD_WIDTH + (g + 1) * D_STATE] for g in range(SSD_GROUPS)]
    crows = [act[:, c_off + g * D_STATE:c_off + (g + 1) * D_STATE] for g in range(SSD_GROUPS)]
    cbs = [jnp.sum(_bf16_round(brows[g]) * _bf16_round(crows[g]), axis=-1, keepdims=True) for g in range(SSD_GROUPS)]
    c8s = [jnp.broadcast_to(crows[g], (8, D_STATE)).astype(BF16) for g in range(SSD_GROUPS)]
    hpr = LANES // P
    for h in range(SSD_HEADS):
        g = h // hpg
        xs_h = xs[:, h * P:(h + 1) * P]
        xd_h = xs_h * dtp[:, h:h + 1]
        xcol = xdt[(h % hpr) * P:(h % hpr + 1) * P, h // hpr:h // hpr + 1]
        dec = decay[:, h:h + 1]
        h0q = h0_ref[0, h].astype(BF16)
        st_ref[0, h] = _bf16_round(dec) * h0q.astype(F32) + _bf16_round(xcol * brows[g])
        yoff = lax.dot_general(c8s[g], h0q, (((1,), (1,)), ((), ())), preferred_element_type=F32)[0:1, :]
        ysc_ref[:, h * P:(h + 1) * P] = (cbs[g] * xd_h + dec * yoff) + dsk[:, h:h + 1] * xs_h
    parts = _gated_group_norm(ysc_ref[...], z_ref[0], nw_ref[...])
    gw = SSD_WIDTH // SSD_GROUPS
    for g in range(SSD_GROUPS):
        y_ref[0, :, g * gw:(g + 1) * gw] = parts[g].astype(BF16)


def _ssd_sample(xbc, dt, z, h0, conv0, conv_w, conv_b, dt_bias, a_log, d_skip, norm_w):
    bd = xbc.shape[0]
    nr = CONV_DIM // LANES
    nx = SSD_WIDTH // LANES
    rep = SSD_HEADDIM
    row3 = lambda b: (b, 0, 0)
    const2 = lambda b: (0, 0)
    full2 = lambda shp: pl.BlockSpec(shp, const2)
    state = pl.BlockSpec((1, SSD_HEADS, SSD_HEADDIM, D_STATE), lambda b: (b, 0, 0, 0))
    y, st, cv = pl.pallas_call(
        _ssd_sample_kernel,
        grid=(bd,),
        in_specs=[pl.BlockSpec((1, 1, CONV_DIM), row3),
                  pl.BlockSpec((1, CONV_W - 1, CONV_DIM), row3),
                  pl.BlockSpec((1, 1, SSD_HEADS), row3),
                  pl.BlockSpec((1, 1, SSD_WIDTH), row3),
                  full2((CONV_W, CONV_DIM)), full2((1, CONV_DIM)), full2((1, SSD_HEADS)), full2((1, SSD_HEADS)),
                  full2((1, SSD_HEADS)), full2((1, SSD_WIDTH)), state,
                  pl.BlockSpec((1, nr, LANES), row3),
                  pl.BlockSpec((1, CONV_W - 1, nr, LANES), lambda b: (b, 0, 0, 0)),
                  pl.BlockSpec((1, nx, LANES), row3),
                  pl.BlockSpec((CONV_W, nr, LANES), lambda b: (0, 0, 0)),
                  full2((nr, LANES)), full2((nx, LANES))],
        out_specs=[pl.BlockSpec((1, 1, SSD_WIDTH), row3), state,
                   pl.BlockSpec((1, CONV_W - 1, CONV_DIM), row3)],
        out_shape=[jax.ShapeDtypeStruct((bd, 1, SSD_WIDTH), BF16),
                   jax.ShapeDtypeStruct((bd, SSD_HEADS, SSD_HEADDIM, D_STATE), F32),
                   jax.ShapeDtypeStruct((bd, CONV_W - 1, CONV_DIM), F32)],
        scratch_shapes=[pltpu.VMEM((1, SSD_WIDTH), F32)],
        compiler_params=_params(("arbitrary",)),
    )(xbc.reshape(bd, 1, CONV_DIM), conv0, dt.reshape(bd, 1, SSD_HEADS), z.reshape(bd, 1, SSD_WIDTH),
      conv_w, conv_b.reshape(1, -1), dt_bias.reshape(1, -1), a_log.reshape(1, -1), d_skip.reshape(1, -1),
      norm_w.reshape(1, -1), h0,
      xbc.reshape(bd, nr, LANES), conv0.reshape(bd, CONV_W - 1, nr, LANES),
      jnp.repeat(dt, rep, axis=-1).reshape(bd, nx, LANES), conv_w.reshape(CONV_W, nr, LANES),
      conv_b.reshape(nr, LANES), jnp.repeat(dt_bias, rep).reshape(nx, LANES))
    return y.reshape(bd, SSD_WIDTH), st, cv


def _outproj_kernel(att_ref, ssd_ref, x_ref, w_ref, g_ref, b_ref, wr_ref, br_ref, h_ref, route_ref, cnt_ref,
                    carry_ref, *, alpha):
    i = pl.program_id(0)
    tm = x_ref.shape[0]

    @pl.when(i == 0)
    def _():
        carry_ref[...] = jnp.zeros_like(carry_ref)

    mix = jnp.dot(att_ref[...], w_ref[0:ATT_WIDTH, :], preferred_element_type=F32)
    mix = mix + jnp.dot(ssd_ref[...], w_ref[ATT_WIDTH:ATT_WIDTH + SSD_WIDTH, :], preferred_element_type=F32)
    hval = _layer_norm(alpha * x_ref[...] + mix, g_ref[...], b_ref[...])
    h_ref[...] = hval

    logits = jnp.dot(hval.astype(BF16), wr_ref[...].astype(BF16), preferred_element_type=F32) + br_ref[...]

    lane = lax.broadcasted_iota(jnp.int32, (tm, N_EXPERTS), 1).astype(F32)
    work = logits
    chosen = jnp.zeros((tm, N_EXPERTS), F32)
    vals, idxs = [], []
    for _ in range(TOP_K):
        mk = jnp.max(work, axis=-1, keepdims=True)
        ik = jnp.min(jnp.where(work == mk, lane, float(N_EXPERTS)), axis=-1, keepdims=True)
        sel = lane == ik
        work = jnp.where(sel, -jnp.inf, work)
        chosen = jnp.where(sel, 1.0, chosen)
        vals.append(mk)
        idxs.append(ik)
    es = [jnp.exp(v - vals[0]) for v in vals]
    den = es[0] + es[1] + es[2] + es[3]

    ri = lax.broadcasted_iota(jnp.int32, (tm, tm), 0)
    ci = lax.broadcasted_iota(jnp.int32, (tm, tm), 1)
    before = jnp.where(ci < ri, 1.0, 0.0).astype(BF16)
    prefix = jnp.dot(before, chosen.astype(BF16), preferred_element_type=F32) + carry_ref[...]
    carry_ref[...] = carry_ref[...] + jnp.sum(chosen, axis=0, keepdims=True)
    cnt_ref[...] = carry_ref[...]

    olane = lax.broadcasted_iota(jnp.int32, (tm, LANES), 1)
    route = jnp.zeros((tm, LANES), F32)
    for k in range(TOP_K):
        rank_k = jnp.sum(jnp.where(lane == idxs[k], prefix, 0.0), axis=-1, keepdims=True)
        route = jnp.where(olane == k, idxs[k], route)
        route = jnp.where(olane == TOP_K + k, es[k] / den, route)
        route = jnp.where(olane == 2 * TOP_K + k, rank_k, route)
    route_ref[...] = route


def _outproj_router(att, ssd, x2d, w_out_bf16, ln_g, ln_b, w_router, b_router, alpha):
    m, d = x2d.shape
    tm = min(256, m)
    row = lambda i: (i, 0)
    const = lambda i: (0, 0)
    kern = functools.partial(_outproj_kernel, alpha=alpha)
    return pl.pallas_call(
        kern,
        grid=(m // tm,),
        in_specs=[pl.BlockSpec((tm, ATT_WIDTH), row), pl.BlockSpec((tm, SSD_WIDTH), row), pl.BlockSpec((tm, d), row),
                  pl.BlockSpec((ATT_WIDTH + SSD_WIDTH, d), const, pipeline_mode=pl.Buffered(1)),
                  pl.BlockSpec((1, d), const), pl.BlockSpec((1, d), const),
                  pl.BlockSpec((d, N_EXPERTS), const), pl.BlockSpec((1, N_EXPERTS), const)],
        out_specs=[pl.BlockSpec((tm, d), row), pl.BlockSpec((tm, LANES), row), pl.BlockSpec((1, N_EXPERTS), const)],
        out_shape=[jax.ShapeDtypeStruct((m, d), F32), jax.ShapeDtypeStruct((m, LANES), F32),
                   jax.ShapeDtypeStruct((1, N_EXPERTS), F32)],
        scratch_shapes=[pltpu.VMEM((1, N_EXPERTS), F32)],
        compiler_params=_params(("arbitrary",)),
    )(att, ssd, x2d, w_out_bf16, ln_g.reshape(1, d), ln_b.reshape(1, d), w_router, b_router.reshape(1, -1))


def _row_copy(src, dst, sem):
    return pltpu.make_async_copy(src, dst, sem)


def _scatter_kernel(cnt_ref, pst_ref, dest_ref, h_ref, xrows_ref, zero_ref, sem, zsem):
    i = pl.program_id(0)
    tm = h_ref.shape[0]

    @pl.when(i == 0)
    def _():
        zero_ref[...] = jnp.zeros_like(zero_ref)

        def per_expert(e, _):
            n = cnt_ref[e]
            base = pst_ref[e]
            end = (n + MOE_SUB - 1) // MOE_SUB * MOE_SUB

            def start(r, _):
                _row_copy(zero_ref.at[pl.ds(0, 1)], xrows_ref.at[pl.ds(base + r, 1)], zsem).start()
                return 0

            def wait(r, _):
                _row_copy(zero_ref.at[pl.ds(0, 1)], xrows_ref.at[pl.ds(base + r, 1)], zsem).wait()
                return 0

            lax.fori_loop(n, end, start, 0)
            lax.fori_loop(n, end, wait, 0)
            return 0

        lax.fori_loop(0, N_EXPERTS, per_expert, 0)

    def start(t, _):
        for k in range(TOP_K):
            d = dest_ref[0, 0, t * TOP_K + k]
            _row_copy(h_ref.at[pl.ds(t, 1)], xrows_ref.at[pl.ds(d, 1)], sem).start()
        return 0

    def wait(t, _):
        for k in range(TOP_K):
            d = dest_ref[0, 0, t * TOP_K + k]
            _row_copy(h_ref.at[pl.ds(t, 1)], xrows_ref.at[pl.ds(d, 1)], sem).wait()
        return 0

    lax.fori_loop(0, tm, start, 0)
    lax.fori_loop(0, tm, wait, 0)


def _moe_scatter(h2d, dest, counts, pstart, n_rows):
    m, d = h2d.shape
    tm = min(256, m)
    nt = m // tm
    dest3 = dest.reshape(nt, 1, tm * TOP_K)
    grid_spec = pltpu.PrefetchScalarGridSpec(
        num_scalar_prefetch=2,
        grid=(nt,),
        in_specs=[pl.BlockSpec((1, 1, tm * TOP_K), lambda i, c, p: (i, 0, 0), memory_space=pltpu.SMEM),
                  pl.BlockSpec((tm, d), lambda i, c, p: (i, 0))],
        out_specs=pl.BlockSpec(memory_space=pl.ANY),
        scratch_shapes=[pltpu.VMEM((8, d), F32), pltpu.SemaphoreType.DMA(()), pltpu.SemaphoreType.DMA(())],
    )
    return pl.pallas_call(
        _scatter_kernel,
        grid_spec=grid_spec,
        out_shape=jax.ShapeDtypeStruct((n_rows, d), F32),
        compiler_params=_params(("arbitrary",)),
    )(counts, pstart, dest3, h2d)


def _moe_mlp_kernel(ie_ref, ib_ref, iv_ref, x_ref, w1g_ref, w1l_ref, b1g_ref, b1l_ref, w2_ref, b2_ref, o_ref,
                    wg_ref, wl_ref, w2b_ref):
    i = pl.program_id(0)
    j = pl.program_id(1)
    nvalid = iv_ref[i]

    @pl.when(nvalid > 0)
    def _():
        wg_ref[...] = w1g_ref[0].astype(BF16)
        wl_ref[...] = w1l_ref[0].astype(BF16)
        w2b_ref[...] = w2_ref[0].astype(BF16)

    for sub in range(MOE_ROW_TILE // MOE_SUB):
        @pl.when(sub * MOE_SUB < nvalid)
        def _():
            rows = pl.ds(sub * MOE_SUB, MOE_SUB)
            xb = x_ref[rows, :].astype(BF16)
            glu = jnp.dot(xb, wg_ref[...], preferred_element_type=F32) + b1g_ref[0]
            lin = jnp.dot(xb, wl_ref[...], preferred_element_type=F32) + b1l_ref[0]
            glu = jnp.minimum(glu, SWIGLU_LIMIT)
            lin = jnp.clip(lin, -SWIGLU_LIMIT, SWIGLU_LIMIT)
            act = glu * _sigmoid(SWIGLU_ALPHA * glu) * (lin + 1.0)
            y = jnp.dot(act.astype(BF16), w2b_ref[...], preferred_element_type=F32)

            @pl.when(j == 0)
            def _():
                o_ref[rows, :] = y + b2_ref[0]

            @pl.when(j > 0)
            def _():
                o_ref[rows, :] = o_ref[rows, :] + y


def _moe_mlp(x_rows, item_e, item_blk, item_valid, w1, b1, w2, b2):
    n_rows, d = x_rows.shape
    n_items = item_e.shape[0]
    d_ff = w2.shape[1]
    tf = MOE_F_TILE
    nf = d_ff // tf
    tmr = MOE_ROW_TILE

    def jj(i, j, iv):
        return jnp.where(iv[i] > 0, j, nf - 1)

    grid_spec = pltpu.PrefetchScalarGridSpec(
        num_scalar_prefetch=3,
        grid=(n_items, nf),
        in_specs=[pl.BlockSpec((tmr, d), lambda i, j, ie, ib, iv: (ib[i], 0)),
                  pl.BlockSpec((1, d, tf), lambda i, j, ie, ib, iv: (ie[i], 0, jj(i, j, iv))),
                  pl.BlockSpec((1, d, tf), lambda i, j, ie, ib, iv: (ie[i], 0, nf + jj(i, j, iv))),
                  pl.BlockSpec((1, 1, tf), lambda i, j, ie, ib, iv: (ie[i], 0, jj(i, j, iv))),
                  pl.BlockSpec((1, 1, tf), lambda i, j, ie, ib, iv: (ie[i], 0, nf + jj(i, j, iv))),
                  pl.BlockSpec((1, tf, d), lambda i, j, ie, ib, iv: (ie[i], jj(i, j, iv), 0)),
                  pl.BlockSpec((1, 1, d), lambda i, j, ie, ib, iv: (ie[i], 0, 0))],
        out_specs=pl.BlockSpec((tmr, d), lambda i, j, ie, ib, iv: (ib[i], 0)),
        scratch_shapes=[pltpu.VMEM((d, tf), BF16), pltpu.VMEM((d, tf), BF16), pltpu.VMEM((tf, d), BF16)],
    )
    return pl.pallas_call(
        _moe_mlp_kernel,
        grid_spec=grid_spec,
        out_shape=jax.ShapeDtypeStruct((n_rows, d), F32),
        compiler_params=_params(("arbitrary", "arbitrary")),
    )(item_e, item_blk, item_valid, x_rows, w1, w1, b1.reshape(N_EXPERTS, 1, -1), b1.reshape(N_EXPERTS, 1, -1),
      w2, b2.reshape(N_EXPERTS, 1, -1))


def _combine_kernel(dest_ref, h_ref, route_ref, yrows_ref, g_ref, b_ref, o_ref, buf_ref, sem, *, alpha):
    tm = h_ref.shape[0]

    def start(t, _):
        for k in range(TOP_K):
            d = dest_ref[0, 0, t * TOP_K + k]
            _row_copy(yrows_ref.at[pl.ds(d, 1)], buf_ref.at[k, pl.ds(t, 1)], sem).start()
        return 0

    def wait(t, _):
        for k in range(TOP_K):
            d = dest_ref[0, 0, t * TOP_K + k]
            _row_copy(yrows_ref.at[pl.ds(d, 1)], buf_ref.at[k, pl.ds(t, 1)], sem).wait()
        return 0

    lax.fori_loop(0, tm, start, 0)
    lax.fori_loop(0, tm, wait, 0)
    route = route_ref[...]
    acc = route[:, TOP_K:TOP_K + 1] * buf_ref[0]
    for k in range(1, TOP_K):
        acc = acc + route[:, TOP_K + k:TOP_K + k + 1] * buf_ref[k]
    o_ref[...] = _layer_norm(alpha * h_ref[...] + acc, g_ref[...], b_ref[...])


def _moe_combine(h2d, route, dest, y_rows, ln_g, ln_b, alpha):
    m, d = h2d.shape
    tm = min(128, m)
    nt = m // tm
    dest3 = dest.reshape(nt, 1, tm * TOP_K)
    kern = functools.partial(_combine_kernel, alpha=alpha)
    return pl.pallas_call(
        kern,
        grid=(nt,),
        in_specs=[pl.BlockSpec((1, 1, tm * TOP_K), lambda i: (i, 0, 0), memory_space=pltpu.SMEM),
                  pl.BlockSpec((tm, d), lambda i: (i, 0)),
                  pl.BlockSpec((tm, LANES), lambda i: (i, 0)),
                  pl.BlockSpec(memory_space=pl.ANY),
                  pl.BlockSpec((1, d), lambda i: (0, 0)), pl.BlockSpec((1, d), lambda i: (0, 0))],
        out_specs=pl.BlockSpec((tm, d), lambda i: (i, 0)),
        out_shape=jax.ShapeDtypeStruct((m, d), F32),
        scratch_shapes=[pltpu.VMEM((TOP_K, tm, d), F32), pltpu.SemaphoreType.DMA(())],
        compiler_params=_params(("arbitrary",)),
    )(dest3, h2d, route, y_rows, ln_g.reshape(1, d), ln_b.reshape(1, d))


def _moe_ln2(h2d, route, counts_f, w1, b1, w2, b2, ln_g, ln_b, alpha):
    m, d = h2d.shape
    tmr = MOE_ROW_TILE
    idx = route[:, 0:TOP_K].astype(jnp.int32)
    rank = route[:, 2 * TOP_K:3 * TOP_K].astype(jnp.int32)
    counts = counts_f.reshape(-1).astype(jnp.int32)
    tiles = (counts + tmr - 1) // tmr
    tile_end = jnp.cumsum(tiles)
    tile_start = tile_end - tiles
    pstart = tile_start * tmr
    dest = (pstart[idx] + rank).reshape(-1)
    n_items = -(-(m * TOP_K) // tmr) + N_EXPERTS
    n_rows = n_items * tmr
    it = jnp.arange(n_items, dtype=jnp.int32)
    total = tile_end[-1]
    it_c = jnp.minimum(it, total - 1)
    item_e = jnp.minimum(jnp.searchsorted(tile_end, it_c, side='right'), N_EXPERTS - 1).astype(jnp.int32)
    item_r = it_c - tile_start[item_e]
    item_blk = (tile_start[item_e] + item_r).astype(jnp.int32)
    item_valid = jnp.where(it < total, jnp.clip(counts[item_e] - item_r * tmr, 0, tmr), 0).astype(jnp.int32)

    x_rows = _moe_scatter(h2d, dest, counts, pstart.astype(jnp.int32), n_rows)
    y_rows = _moe_mlp(x_rows, item_e, item_blk, item_valid, w1, b1, w2, b2)
    return _moe_combine(h2d, route, dest, y_rows, ln_g, ln_b, alpha)


def kernel(x_prompt, x_sample, cache_k, cache_v, page_table, state_ssm, state_conv, w_in, w_out, lambda_qk,
           attn_subln_w, conv_w, conv_b, dt_bias, a_log, d_skip, ssd_norm_w, ln1_g, ln1_b, w_router, b_router,
           w_mlp1, b_mlp1, w_mlp2, b_mlp2, ln2_g, ln2_b):
    bp, sp, d = x_prompt.shape
    bd, sd, _ = x_sample.shape
    depth = w_in.shape[0]
    assert depth == 1 and sd == 1, "kernel supports the single-layer, single-token-decode configuration"
    alpha = (2.0 * depth) ** 0.25
    slopes = 2.0 ** (-8.0 * jnp.arange(1, ATT_HEADS + 1, dtype=F32) / ATT_HEADS)
    l = 0
    lam_init = 0.8 - 0.6 * math.exp(-0.3 * l)

    w_in_b = _cast_bf16(w_in[l])
    w_out_b = _cast_bf16(w_out[l])
    moe_w = (w_mlp1[l], b_mlp1[l], w_mlp2[l], b_mlp2[l], ln2_g[l], ln2_b[l])
    ssd_w = (conv_w[l], conv_b[l], dt_bias[l], a_log[l], d_skip[l], ssd_norm_w[l])

    xp = x_prompt.reshape(bp * sp, d)
    q, k, v, z, xbc, dt = _inproj(xp, w_in_b, BF16)
    att = _attn_prompt(q.reshape(bp, sp, -1), k.reshape(bp, sp, -1), v.reshape(bp, sp, -1), lambda_qk[l],
                       attn_subln_w[l], slopes, lam_init)
    ssm_zero = jnp.zeros((bp, SSD_HEADS, SSD_HEADDIM, D_STATE), F32)
    conv_zero = jnp.zeros((bp, CONV_W - 1, CONV_DIM), F32)
    ssd, ssm_p, conv_p = _ssd_prompt(xbc.reshape(bp, sp, -1), dt.reshape(bp, sp, -1), z.reshape(bp, sp, -1),
                                     ssm_zero, conv_zero, *ssd_w)
    h, route, counts = _outproj_router(att.reshape(bp * sp, -1), ssd.reshape(bp * sp, -1), xp, w_out_b,
                                       ln1_g[l], ln1_b[l], w_router[l], b_router[l], alpha)
    y_prompt = _moe_ln2(h, route, counts, *moe_w, alpha).reshape(bp, sp, d)

    xs = x_sample.reshape(bd, d)
    qs, ks, vs, zs, xbcs, dts = _inproj(xs, w_in_b, F32)
    att_s = _attn_decode(qs, ks, vs, cache_k[l], cache_v[l], page_table, lambda_qk[l], attn_subln_w[l], slopes,
                         lam_init)
    ssd_s, ssm_s, conv_s = _ssd_sample(xbcs, dts, zs, state_ssm[l], state_conv[l], *ssd_w)
    hs, route_s, counts_s = _outproj_router(att_s, ssd_s, xs, w_out_b, ln1_g[l], ln1_b[l], w_router[l],
                                            b_router[l], alpha)
    y_sample = _moe_ln2(hs, route_s, counts_s, *moe_w, alpha).reshape(bd, sd, d)

    hshape = (ATT_HEADS, 2 * ATT_DH)
    return (y_prompt, y_sample,
            k.reshape(1, bp, sp, *hshape), v.reshape(1, bp, sp, *hshape), ssm_p[None], conv_p[None],
            ks.reshape(1, bd, sd, *hshape), vs.reshape(1, bd, sd, *hshape), ssm_s[None], conv_s[None])
```

```python
import functools
import math

import jax
import jax.numpy as jnp
from jax import lax
from jax.experimental import pallas as pl
from jax.experimental.pallas import tpu as pltpu

F32 = jnp.float32
BF16 = jnp.bfloat16

ATT_DH = 64
ATT_HEADS = 8
ATT_WIDTH = ATT_HEADS * 2 * ATT_DH
ATT_SCALE = ATT_DH ** -0.5
SSD_HEADDIM = 64
SSD_HEADS = 16
SSD_WIDTH = SSD_HEADS * SSD_HEADDIM
SSD_GROUPS = 2
D_STATE = 128
CONV_W = 4
CONV_DIM = SSD_WIDTH + 2 * SSD_GROUPS * D_STATE
SSD_CHUNK = 128
N_EXPERTS = 32
TOP_K = 4
SWIGLU_ALPHA = 1.702
SWIGLU_LIMIT = 7.0
LN_EPS = 1e-5
RMS_EPS = 1e-5
PAGE_SIZE = 128

V7X_VMEM_LIMIT_BYTES = 56 * 1024 * 1024
LANES = 128

MOE_ROW_TILE = 1024
MOE_SUB = 256
MOE_F_TILE = 256


def _params(sem, vmem=V7X_VMEM_LIMIT_BYTES):
    return pltpu.CompilerParams(dimension_semantics=sem, vmem_limit_bytes=vmem)


def _sigmoid(x):
    return 1.0 / (1.0 + jnp.exp(-x))


def _silu(x):
    return x * _sigmoid(x)


def _softplus(x):
    return jnp.maximum(x, 0.0) + jnp.log1p(jnp.exp(-jnp.abs(x)))


def _layer_norm(x, g, b):
    mu = jnp.mean(x, axis=-1, keepdims=True)
    xc = x - mu
    var = jnp.mean(xc * xc, axis=-1, keepdims=True)
    return xc * lax.rsqrt(var + LN_EPS) * g + b


def _split3(x):
    a = x.astype(BF16)
    r = x - a.astype(F32)
    b = r.astype(BF16)
    c = (r - b.astype(F32)).astype(BF16)
    return a, b, c


def _lam(lq):
    s01 = jnp.sum(lq[0:1, :] * lq[1:2, :], axis=-1, keepdims=True)
    s23 = jnp.sum(lq[2:3, :] * lq[3:4, :], axis=-1, keepdims=True)
    return jnp.exp(s01) - jnp.exp(s23)


def _cast_kernel(w_ref, o_ref):
    o_ref[...] = w_ref[...].astype(BF16)


def _cast_bf16(w, row_tile=256):
    r, c = w.shape
    rt = min(row_tile, r)
    return pl.pallas_call(
        _cast_kernel,
        grid=(r // rt,),
        in_specs=[pl.BlockSpec((rt, c), lambda i: (i, 0))],
        out_specs=pl.BlockSpec((rt, c), lambda i: (i, 0)),
        out_shape=jax.ShapeDtypeStruct((r, c), BF16),
        compiler_params=_params(("arbitrary",)),
    )(w)


def _inproj_kernel(x_ref, w_ref, q_ref, k_ref, v_ref, z_ref, xbc_ref, dt_ref):
    xb = x_ref[...].astype(BF16)

    def mm(c0, c1):
        return jnp.dot(xb, w_ref[:, c0:c1], preferred_element_type=F32)

    a = ATT_WIDTH
    q_ref[...] = (mm(0, a) * ATT_SCALE).astype(q_ref.dtype)
    k_ref[...] = mm(a, 2 * a)
    v_ref[...] = mm(2 * a, 3 * a)
    z_ref[...] = mm(3 * a, 3 * a + SSD_WIDTH)
    c0 = 3 * a + SSD_WIDTH
    xbc_ref[...] = mm(c0, c0 + CONV_DIM)
    dt_ref[...] = mm(c0 + CONV_DIM, c0 + CONV_DIM + SSD_HEADS)


def _inproj(x2d, w_in_bf16, q_dtype):
    m, d = x2d.shape
    ncol = w_in_bf16.shape[1]
    tm = min(256, m)
    row = lambda i: (i, 0)
    widths = (ATT_WIDTH, ATT_WIDTH, ATT_WIDTH, SSD_WIDTH, CONV_DIM, SSD_HEADS)
    dtypes = (q_dtype, F32, F32, F32, F32, F32)
    return pl.pallas_call(
        _inproj_kernel,
        grid=(m // tm,),
        in_specs=[pl.BlockSpec((tm, d), row),
                  pl.BlockSpec((d, ncol), lambda i: (0, 0), pipeline_mode=pl.Buffered(1))],
        out_specs=[pl.BlockSpec((tm, w), row) for w in widths],
        out_shape=[jax.ShapeDtypeStruct((m, w), dt) for w, dt in zip(widths, dtypes)],
        compiler_params=_params(("arbitrary",)),
    )(x2d, w_in_bf16)


def _attn_prompt_kernel(slopes_ref, qt_ref, k_ref, v_ref, lq_ref, w_ref, o_ref, kb_ref, vt_ref, *, tq, lam_init):
    h = pl.program_id(1)
    qi = pl.program_id(2)
    hd = 2 * ATT_DH
    ones_rows = vt_ref.shape[0] - hd

    @pl.when(qi == 0)
    def _():
        kb_ref[...] = k_ref[0].astype(BF16)
        vt_ref[0:hd, :] = v_ref[0].T.astype(BF16)
        vt_ref[hd:hd + ones_rows, :] = jnp.ones((ones_rows, vt_ref.shape[1]), BF16)

    slope = slopes_ref[h]
    qt = qt_ref[0]
    drow = lax.broadcasted_iota(jnp.int32, (hd, tq), 0)
    zero = jnp.zeros_like(qt)
    qts = (jnp.where(drow < ATT_DH, qt, zero), jnp.where(drow >= ATT_DH, qt, zero))
    kr = lax.broadcasted_iota(jnp.int32, (tq, tq), 0)
    qc = lax.broadcasted_iota(jnp.int32, (tq, tq), 1)
    base = (qc - kr).astype(F32) * slope
    future = kr > qc

    def scores(j):
        kb = kb_ref[pl.ds(pl.multiple_of(j * tq, tq), tq), :]
        return tuple(jnp.dot(kb, qts[mi], preferred_element_type=F32) for mi in range(2))

    def block(j, raw, stats, diag):
        vta = vt_ref[:, pl.ds(pl.multiple_of(j * tq, tq), tq)]
        off = jnp.full((1, 1), (qi - j) * tq, jnp.int32).astype(F32) * slope
        bias = base + off
        out = []
        for mi in range(2):
            m, l, a = stats[mi]
            s = raw[mi] - bias
            if diag:
                s = jnp.where(future, -jnp.inf, s)
            mn = jnp.maximum(m, jnp.max(s, axis=0, keepdims=True))
            p = jnp.exp(s - mn)
            al = jnp.exp(m - mn)
            pv = jnp.dot(vta, p.astype(BF16), preferred_element_type=F32)
            l = al * l + pv[hd:hd + 1, :]
            a = al * a + pv[0:hd, :]
            out.append((mn, l, a))
        return tuple(out)

    def step(j, carry):
        raw, stats = carry
        nxt = scores(j + 1)
        return nxt, block(j, raw, stats, False)

    init1 = (jnp.full((1, tq), -1e30, F32), jnp.zeros((1, tq), F32), jnp.zeros((hd, tq), F32))
    raw, stats = lax.fori_loop(0, qi, step, (scores(0), (init1, init1)))
    (_, l0, a0), (_, l1, a1) = block(qi, raw, stats, True)
    lam = _lam(lq_ref[...]) + lam_init
    o = a0 / l0 - lam * (a1 / l1)
    o = o * lax.rsqrt(jnp.mean(o * o, axis=0, keepdims=True) + RMS_EPS) * w_ref[...] * (1.0 - lam_init)
    o_ref[0] = o.T.astype(BF16)


def _attn_prompt(q, k, v, lambda_qk, subln_w, slopes, lam_init):
    b, t, _ = k.shape
    tq = min(256, t)
    hd = 2 * ATT_DH
    ones_rows = 16
    qt = jnp.swapaxes(q, 1, 2)
    kern = functools.partial(_attn_prompt_kernel, tq=tq, lam_init=lam_init)
    return pl.pallas_call(
        kern,
        grid=(b, ATT_HEADS, t // tq),
        in_specs=[pl.BlockSpec(memory_space=pltpu.SMEM),
                  pl.BlockSpec((1, hd, tq), lambda bi, h, qi: (bi, h, qi)),
                  pl.BlockSpec((1, t, hd), lambda bi, h, qi: (bi, 0, h)),
                  pl.BlockSpec((1, t, hd), lambda bi, h, qi: (bi, 0, h)),
                  pl.BlockSpec((4, ATT_DH), lambda bi, h, qi: (0, 0)),
                  pl.BlockSpec((hd, 1), lambda bi, h, qi: (0, 0))],
        out_specs=pl.BlockSpec((1, tq, hd), lambda bi, h, qi: (bi, qi, h)),
        out_shape=jax.ShapeDtypeStruct((b, t, ATT_WIDTH), BF16),
        scratch_shapes=[pltpu.VMEM((t, hd), BF16), pltpu.VMEM((hd + ones_rows, t), BF16)],
        compiler_params=_params(("arbitrary", "arbitrary", "arbitrary")),
    )(slopes, qt, k, v, lambda_qk, subln_w.reshape(hd, 1))


def _attn_decode_kernel(pt_ref, q_ref, kn_ref, vn_ref, slope_ref, lq_ref, w_ref, *rest, pps, past_len, lam_init):
    kp_refs, vp_refs = rest[:pps], rest[pps:2 * pps]
    o_ref, qbd_ref, s_ref, a_ref, m_ref, snew_ref, anew_ref, acc_ref = rest[2 * pps:]
    ph = pl.program_id(1)
    p = pl.program_id(2)
    n_steps = pl.num_programs(2)
    nh = ATT_HEADS
    nrow = 2 * nh
    hd = 2 * ATT_DH

    @pl.when((ph == 0) & (p == 0))
    def _():
        rowi = lax.broadcasted_iota(jnp.int32, (nrow, ATT_WIDTH), 0)
        lanei = lax.broadcasted_iota(jnp.int32, (nrow, ATT_WIDTH), 1)
        qf = jnp.broadcast_to(q_ref[0], (nrow, ATT_WIDTH))
        qbd = jnp.where((lanei // ATT_DH) == 2 * (rowi % nh) + rowi // nh, qf, 0.0)
        qbd_ref[...] = qbd.astype(BF16)
        prod = qbd * kn_ref[0]
        ones = jnp.ones((8, ATT_WIDTH), BF16)
        s_new = sum(lax.dot_general(ones, part, (((1,), (1,)), ((), ())), preferred_element_type=F32)
                    for part in _split3(prod))[0:1, :]
        snew_ref[...] = s_new
        m_ref[...] = s_new

    @pl.when(ph == 0)
    def _():
        for i in range(pps):
            kb = jnp.concatenate([kp_refs[i][0, :, h, :] for h in range(nh)], axis=-1).astype(BF16)
            s = lax.dot_general(kb, qbd_ref[...], (((1,), (1,)), ((), ())), preferred_element_type=F32)
            page = p * pps + i
            pos = page * PAGE_SIZE + lax.broadcasted_iota(jnp.int32, (PAGE_SIZE, 1), 0)
            s = s - (past_len - pos).astype(F32) * slope_ref[...]
            s_ref[pl.ds(pl.multiple_of(page * PAGE_SIZE, PAGE_SIZE), PAGE_SIZE), :] = s
            m_ref[...] = jnp.maximum(m_ref[...], jnp.max(s, axis=0, keepdims=True))

    @pl.when((ph == 1) & (p == 0))
    def _():
        m = m_ref[...]
        e = jnp.exp(s_ref[...] - m)
        e_new = jnp.exp(snew_ref[...] - m)
        den = jnp.sum(e, axis=0, keepdims=True) + e_new
        lam = _lam(lq_ref[...]) + lam_init
        pn = e / den
        a_ref[...] = pn[:, 0:nh] - lam * pn[:, nh:nrow]
        pn_new = e_new / den
        anew_ref[...] = pn_new[:, 0:nh] - lam * pn_new[:, nh:nrow]
        acc_ref[...] = jnp.zeros_like(acc_ref)

    @pl.when(ph == 1)
    def _():
        for i in range(pps):
            page = p * pps + i
            rows = pl.ds(pl.multiple_of(page * PAGE_SIZE, PAGE_SIZE), PAGE_SIZE)
            for h in range(nh):
                acc_ref[:, h * hd:(h + 1) * hd] += a_ref[rows, h:h + 1] * vp_refs[i][0, :, h, :]

    @pl.when((ph == 1) & (p == n_steps - 1))
    def _():
        for h in range(nh):
            o = (jnp.sum(acc_ref[:, h * hd:(h + 1) * hd], axis=0, keepdims=True)
                 + anew_ref[:, h:h + 1] * vn_ref[0, :, h * hd:(h + 1) * hd])
            o = o * lax.rsqrt(jnp.mean(o * o, axis=-1, keepdims=True) + RMS_EPS) * w_ref[...] * (1.0 - lam_init)
            o_ref[0, :, h * hd:(h + 1) * hd] = o.astype(BF16)


def _attn_decode(q, k_new, v_new, cache_k, cache_v, page_table, lambda_qk, subln_w, slopes, lam_init):
    bd = q.shape[0]
    n_pages = page_table.shape[1]
    nrow = 2 * ATT_HEADS
    hd = 2 * ATT_DH
    pps = max(c for c in (1, 2, 4) if n_pages % c == 0)
    n_steps = n_pages // pps
    kp = cache_k.reshape(-1, PAGE_SIZE, ATT_HEADS, hd)
    vp = cache_v.reshape(-1, PAGE_SIZE, ATT_HEADS, hd)
    slope_cols = jnp.tile(slopes, 2).reshape(1, nrow)
    past_len = n_pages * PAGE_SIZE
    kern = functools.partial(_attn_decode_kernel, pps=pps, past_len=past_len, lam_init=lam_init)
    row3 = lambda b, ph, p, pt: (b, 0, 0)
    const2 = lambda b, ph, p, pt: (0, 0)

    def kpage(i):
        return lambda b, ph, p, pt: (pt[b * n_pages + jnp.where(ph == 0, p, n_steps - 1) * pps + i], 0, 0, 0)

    def vpage(i):
        return lambda b, ph, p, pt: (pt[b * n_pages + jnp.where(ph == 0, 0, p) * pps + i], 0, 0, 0)

    page_block = (1, PAGE_SIZE, ATT_HEADS, hd)
    grid_spec = pltpu.PrefetchScalarGridSpec(
        num_scalar_prefetch=1,
        grid=(bd, 2, n_steps),
        in_specs=[pl.BlockSpec((1, 1, ATT_WIDTH), row3),
                  pl.BlockSpec((1, 1, ATT_WIDTH), row3),
                  pl.BlockSpec((1, 1, ATT_WIDTH), row3),
                  pl.BlockSpec((1, nrow), const2),
                  pl.BlockSpec((4, ATT_DH), const2),
                  pl.BlockSpec((1, hd), const2)]
                 + [pl.BlockSpec(page_block, kpage(i)) for i in range(pps)]
                 + [pl.BlockSpec(page_block, vpage(i)) for i in range(pps)],
        out_specs=pl.BlockSpec((1, 1, ATT_WIDTH), row3),
        scratch_shapes=[pltpu.VMEM((nrow, ATT_WIDTH), BF16), pltpu.VMEM((past_len, nrow), F32),
                        pltpu.VMEM((past_len, ATT_HEADS), F32), pltpu.VMEM((1, nrow), F32),
                        pltpu.VMEM((1, nrow), F32), pltpu.VMEM((1, ATT_HEADS), F32),
                        pltpu.VMEM((PAGE_SIZE, ATT_WIDTH), F32)],
    )
    out = pl.pallas_call(
        kern,
        grid_spec=grid_spec,
        out_shape=jax.ShapeDtypeStruct((bd, 1, ATT_WIDTH), BF16),
        compiler_params=_params(("arbitrary", "arbitrary", "arbitrary")),
    )(page_table.reshape(-1), q.reshape(bd, 1, ATT_WIDTH), k_new.reshape(bd, 1, ATT_WIDTH),
      v_new.reshape(bd, 1, ATT_WIDTH), slope_cols, lambda_qk, subln_w.reshape(1, hd), *([kp] * pps), *([vp] * pps))
    return out.reshape(bd, ATT_WIDTH)


def _gated_group_norm(y, z, w):
    yg = y * _silu(z)
    gw = SSD_WIDTH // SSD_GROUPS
    parts = []
    for g in range(SSD_GROUPS):
        v = yg[:, g * gw:(g + 1) * gw]
        parts.append(v * lax.rsqrt(jnp.mean(v * v, axis=-1, keepdims=True) + RMS_EPS) * w[:, g * gw:(g + 1) * gw])
    return parts


def _ssd_prompt_kernel(xbc_ref, dt_ref, dtt_ref, z_ref, cw_ref, cb_ref, dtb_ref, dtbt_ref, al_ref, alt_ref,
                       dsk_ref, nw_ref, h0_ref, c0_ref, y_ref, st_ref, cv_ref, xpad_ref, ysc_ref, xdd_ref):
    c = pl.program_id(1)
    nc = pl.num_programs(1)
    L = SSD_CHUNK
    P = SSD_HEADDIM
    hpg = SSD_HEADS // SSD_GROUPS
    halo = 8

    @pl.when(c == 0)
    def _():
        st_ref[...] = h0_ref[...]
        xpad_ref[0:halo, :] = jnp.zeros((halo, CONV_DIM), F32)
        xpad_ref[halo - (CONV_W - 1):halo, :] = c0_ref[0]

    xc = xbc_ref[0]
    xpad_ref[halo:halo + L, :] = xc
    conv = cb_ref[...]
    for i in range(CONV_W - 1):
        sh = CONV_W - 1 - i
        conv = conv + xpad_ref[halo - sh:halo - sh + L, :] * cw_ref[i:i + 1, :]
    conv = conv + xc * cw_ref[CONV_W - 1:CONV_W, :]
    tail = xc[L - (CONV_W - 1):L, :]
    xpad_ref[halo - (CONV_W - 1):halo, :] = tail

    @pl.when(c == nc - 1)
    def _():
        cv_ref[0] = tail

    act = _silu(conv)
    xs = act[:, :SSD_WIDTH]
    bmat = [act[:, SSD_WIDTH + g * D_STATE:SSD_WIDTH + (g + 1) * D_STATE].astype(BF16) for g in range(SSD_GROUPS)]
    c_off = SSD_WIDTH + SSD_GROUPS * D_STATE
    cmat = [act[:, c_off + g * D_STATE:c_off + (g + 1) * D_STATE].astype(BF16) for g in range(SSD_GROUPS)]

    dtp = _softplus(dt_ref[0] + dtb_ref[...])
    dtpt = _softplus(dtt_ref[0] + dtbt_ref[...])
    da = dtp * (-jnp.exp(al_ref[...]))
    dat = dtpt * (-jnp.exp(alt_ref[...]))
    ri = lax.broadcasted_iota(jnp.int32, (L, L), 0)
    ci = lax.broadcasted_iota(jnp.int32, (L, L), 1)
    causal = ri >= ci
    tri = jnp.where(causal, 1.0, 0.0).astype(BF16)
    trit = jnp.where(ci >= ri, 1.0, 0.0).astype(BF16)
    cs = sum(jnp.dot(tri, part, preferred_element_type=F32) for part in _split3(da))
    cst = sum(jnp.dot(part, trit, preferred_element_type=F32) for part in _split3(dat))

    cb = [lax.dot_general(cmat[g], bmat[g], (((1,), (1,)), ((), ())), preferred_element_type=F32)
          for g in range(SSD_GROUPS)]
    dsk = dsk_ref[...]

    for h in range(SSD_HEADS):
        g = h // hpg
        cs_col = cs[:, h:h + 1]
        diff = cs_col - cst[h:h + 1, :]
        lmat = jnp.exp(jnp.where(causal, diff, -jnp.inf))
        mmat = (cb[g] * lmat).astype(BF16)
        xs_h = xs[:, h * P:(h + 1) * P]
        xd_h = xs_h * dtp[:, h:h + 1]
        y = jnp.dot(mmat, xd_h.astype(BF16), preferred_element_type=F32)
        st = st_ref[0, h]
        yoff = lax.dot_general(cmat[g], st.astype(BF16), (((1,), (1,)), ((), ())), preferred_element_type=F32)
        y = y + jnp.exp(cs_col) * yoff + dsk[:, h:h + 1] * xs_h
        ysc_ref[:, h * P:(h + 1) * P] = y
        cs_last = cs[L - 1:L, h:h + 1]
        xdd_ref[:, h * P:(h + 1) * P] = xd_h * jnp.exp(cs_last - cs_col)

    xddt = xdd_ref[...].T
    for h in range(SSD_HEADS):
        g = h // hpg
        new = jnp.dot(xddt[h * P:(h + 1) * P, :].astype(BF16), bmat[g], preferred_element_type=F32)
        cs_last = cs[L - 1:L, h:h + 1]
        st_ref[0, h] = jnp.exp(cs_last) * st_ref[0, h] + new

    parts = _gated_group_norm(ysc_ref[...], z_ref[0], nw_ref[...])
    gw = SSD_WIDTH // SSD_GROUPS
    for g in range(SSD_GROUPS):
        y_ref[0, :, g * gw:(g + 1) * gw] = parts[g].astype(BF16)


def _ssd_prompt(xbc, dt, z, h0, conv0, conv_w, conv_b, dt_bias, a_log, d_skip, norm_w):
    b, t, _ = xbc.shape
    L = SSD_CHUNK
    nc = t // L
    dtt = jnp.swapaxes(dt, 1, 2)
    seq = lambda bi, ci: (bi, ci, 0)
    const2 = lambda bi, ci: (0, 0)
    full2 = lambda shp: pl.BlockSpec(shp, const2)
    return pl.pallas_call(
        _ssd_prompt_kernel,
        grid=(b, nc),
        in_specs=[pl.BlockSpec((1, L, CONV_DIM), seq),
                  pl.BlockSpec((1, L, SSD_HEADS), seq),
                  pl.BlockSpec((1, SSD_HEADS, L), lambda bi, ci: (bi, 0, ci)),
                  pl.BlockSpec((1, L, SSD_WIDTH), seq),
                  full2((CONV_W, CONV_DIM)), full2((1, CONV_DIM)),
                  full2((1, SSD_HEADS)), full2((SSD_HEADS, 1)),
                  full2((1, SSD_HEADS)), full2((SSD_HEADS, 1)),
                  full2((1, SSD_HEADS)), full2((1, SSD_WIDTH)),
                  pl.BlockSpec((1, SSD_HEADS, SSD_HEADDIM, D_STATE), lambda bi, ci: (bi, 0, 0, 0)),
                  pl.BlockSpec((1, CONV_W - 1, CONV_DIM), lambda bi, ci: (bi, 0, 0))],
        out_specs=[pl.BlockSpec((1, L, SSD_WIDTH), seq),
                   pl.BlockSpec((1, SSD_HEADS, SSD_HEADDIM, D_STATE), lambda bi, ci: (bi, 0, 0, 0)),
                   pl.BlockSpec((1, CONV_W - 1, CONV_DIM), lambda bi, ci: (bi, 0, 0))],
        out_shape=[jax.ShapeDtypeStruct((b, t, SSD_WIDTH), BF16),
                   jax.ShapeDtypeStruct((b, SSD_HEADS, SSD_HEADDIM, D_STATE), F32),
                   jax.ShapeDtypeStruct((b, CONV_W - 1, CONV_DIM), F32)],
        scratch_shapes=[pltpu.VMEM((8 + L, CONV_DIM), F32), pltpu.VMEM((L, SSD_WIDTH), F32),
                        pltpu.VMEM((L, SSD_WIDTH), F32)],
        compiler_params=_params(("arbitrary", "arbitrary")),
    )(xbc, dt, dtt, z, conv_w, conv_b.reshape(1, -1), dt_bias.reshape(1, -1), dt_bias.reshape(-1, 1),
      a_log.reshape(1, -1), a_log.reshape(-1, 1), d_skip.reshape(1, -1), norm_w.reshape(1, -1), h0, conv0)


def _bf16_round(v):
    return v.astype(BF16).astype(F32)


def _ssd_sample_kernel(xbc_ref, ci_ref, dt_ref, z_ref, cw_ref, cb_ref, dtb_ref, al_ref, dsk_ref, nw_ref, h0_ref,
                       xbc8_ref, ci8_ref, dt8_ref, cw8_ref, cb8_ref, dtb8_ref, y_ref, st_ref, cv_ref, ysc_ref):
    P = SSD_HEADDIM
    hpg = SSD_HEADS // SSD_GROUPS
    xrow = xbc_ref[0]
    hist = ci_ref[0]
    conv = cb_ref[...]
    for i in range(CONV_W - 1):
        conv = conv + hist[i:i + 1, :] * cw_ref[i:i + 1, :]
    conv = conv + xrow * cw_ref[CONV_W - 1:CONV_W, :]
    cv_ref[0, 0:CONV_W - 2, :] = hist[1:CONV_W - 1, :]
    cv_ref[0, CONV_W - 2:CONV_W - 1, :] = xrow
    act = _silu(conv)
    xs = act[:, :SSD_WIDTH]
    c_off = SSD_WIDTH + SSD_GROUPS * D_STATE
    dtp = _softplus(dt_ref[0] + dtb_ref[...])
    decay = jnp.exp(dtp * (-jnp.exp(al_ref[...])))
    dsk = dsk_ref[...]

    nx = SSD_WIDTH // LANES
    conv8 = cb8_ref[0:nx, :]
    for i in range(CONV_W - 1):
        conv8 = conv8 + ci8_ref[0, i, 0:nx, :] * cw8_ref[i, 0:nx, :]
    conv8 = conv8 + xbc8_ref[0, 0:nx, :] * cw8_ref[CONV_W - 1, 0:nx, :]
    xd8 = _silu(conv8) * _softplus(dt8_ref[0] + dtb8_ref[...])
    xdt = xd8.T

    brows = [act[:, SSD_WIDTH + g * D_STATE:SSD_WIDTH + (g + 1) * D_STATE] for g in range(SSD_GROUPS)]
    crows = [act[:, c_off + g * D_STATE:c_off + (g + 1) * D_STATE] for g in range(SSD_GROUPS)]
    cbs = [jnp.sum(_bf16_round(brows[g]) * _bf16_round(crows[g]), axis=-1, keepdims=True) for g in range(SSD_GROUPS)]
    c8s = [jnp.broadcast_to(crows[g], (8, D_STATE)).astype(BF16) for g in range(SSD_GROUPS)]
    hpr = LANES // P
    for h in range(SSD_HEADS):
        g = h // hpg
        xs_h = xs[:, h * P:(h + 1) * P]
        xd_h = xs_h * dtp[:, h:h + 1]
        xcol = xdt[(h % hpr) * P:(h % hpr + 1) * P, h // hpr:h // hpr + 1]
        dec = decay[:, h:h + 1]
        h0q = h0_ref[0, h].astype(BF16)
        st_ref[0, h] = _bf16_round(dec) * h0q.astype(F32) + _bf16_round(xcol * brows[g])
        yoff = lax.dot_general(c8s[g], h0q, (((1,), (1,)), ((), ())), preferred_element_type=F32)[0:1, :]
        ysc_ref[:, h * P:(h + 1) * P] = (cbs[g] * xd_h + dec * yoff) + dsk[:, h:h + 1] * xs_h
    parts = _gated_group_norm(ysc_ref[...], z_ref[0], nw_ref[...])
    gw = SSD_WIDTH // SSD_GROUPS
    for g in range(SSD_GROUPS):
        y_ref[0, :, g * gw:(g + 1) * gw] = parts[g].astype(BF16)


def _ssd_sample(xbc, dt, z, h0, conv0, conv_w, conv_b, dt_bias, a_log, d_skip, norm_w):
    bd = xbc.shape[0]
    nr = CONV_DIM // LANES
    nx = SSD_WIDTH // LANES
    rep = SSD_HEADDIM
    row3 = lambda b: (b, 0, 0)
    const2 = lambda b: (0, 0)
    full2 = lambda shp: pl.BlockSpec(shp, const2)
    state = pl.BlockSpec((1, SSD_HEADS, SSD_HEADDIM, D_STATE), lambda b: (b, 0, 0, 0))
    y, st, cv = pl.pallas_call(
        _ssd_sample_kernel,
        grid=(bd,),
        in_specs=[pl.BlockSpec((1, 1, CONV_DIM), row3),
                  pl.BlockSpec((1, CONV_W - 1, CONV_DIM), row3),
                  pl.BlockSpec((1, 1, SSD_HEADS), row3),
                  pl.BlockSpec((1, 1, SSD_WIDTH), row3),
                  full2((CONV_W, CONV_DIM)), full2((1, CONV_DIM)), full2((1, SSD_HEADS)), full2((1, SSD_HEADS)),
                  full2((1, SSD_HEADS)), full2((1, SSD_WIDTH)), state,
                  pl.BlockSpec((1, nr, LANES), row3),
                  pl.BlockSpec((1, CONV_W - 1, nr, LANES), lambda b: (b, 0, 0, 0)),
                  pl.BlockSpec((1, nx, LANES), row3),
                  pl.BlockSpec((CONV_W, nr, LANES), lambda b: (0, 0, 0)),
                  full2((nr, LANES)), full2((nx, LANES))],
        out_specs=[pl.BlockSpec((1, 1, SSD_WIDTH), row3), state,
                   pl.BlockSpec((1, CONV_W - 1, CONV_DIM), row3)],
        out_shape=[jax.ShapeDtypeStruct((bd, 1, SSD_WIDTH), BF16),
                   jax.ShapeDtypeStruct((bd, SSD_HEADS, SSD_HEADDIM, D_STATE), F32),
                   jax.ShapeDtypeStruct((bd, CONV_W - 1, CONV_DIM), F32)],
        scratch_shapes=[pltpu.VMEM((1, SSD_WIDTH), F32)],
        compiler_params=_params(("arbitrary",)),
    )(xbc.reshape(bd, 1, CONV_DIM), conv0, dt.reshape(bd, 1, SSD_HEADS), z.reshape(bd, 1, SSD_WIDTH),
      conv_w, conv_b.reshape(1, -1), dt_bias.reshape(1, -1), a_log.reshape(1, -1), d_skip.reshape(1, -1),
      norm_w.reshape(1, -1), h0,
      xbc.reshape(bd, nr, LANES), conv0.reshape(bd, CONV_W - 1, nr, LANES),
      jnp.repeat(dt, rep, axis=-1).reshape(bd, nx, LANES), conv_w.reshape(CONV_W, nr, LANES),
      conv_b.reshape(nr, LANES), jnp.repeat(dt_bias, rep).reshape(nx, LANES))
    return y.reshape(bd, SSD_WIDTH), st, cv


def _outproj_kernel(att_ref, ssd_ref, x_ref, w_ref, g_ref, b_ref, wr_ref, br_ref, h_ref, route_ref, cnt_ref,
                    carry_ref, *, alpha):
    i = pl.program_id(0)
    tm = x_ref.shape[0]

    @pl.when(i == 0)
    def _():
        carry_ref[...] = jnp.zeros_like(carry_ref)

    mix = jnp.dot(att_ref[...], w_ref[0:ATT_WIDTH, :], preferred_element_type=F32)
    mix = mix + jnp.dot(ssd_ref[...], w_ref[ATT_WIDTH:ATT_WIDTH + SSD_WIDTH, :], preferred_element_type=F32)
    hval = _layer_norm(alpha * x_ref[...] + mix, g_ref[...], b_ref[...])
    h_ref[...] = hval

    logits = jnp.dot(hval.astype(BF16), wr_ref[...].astype(BF16), preferred_element_type=F32) + br_ref[...]

    lane = lax.broadcasted_iota(jnp.int32, (tm, N_EXPERTS), 1).astype(F32)
    work = logits
    chosen = jnp.zeros((tm, N_EXPERTS), F32)
    vals, idxs = [], []
    for _ in range(TOP_K):
        mk = jnp.max(work, axis=-1, keepdims=True)
        ik = jnp.min(jnp.where(work == mk, lane, float(N_EXPERTS)), axis=-1, keepdims=True)
        sel = lane == ik
        work = jnp.where(sel, -jnp.inf, work)
        chosen = jnp.where(sel, 1.0, chosen)
        vals.append(mk)
        idxs.append(ik)
    es = [jnp.exp(v - vals[0]) for v in vals]
    den = es[0] + es[1] + es[2] + es[3]

    ri = lax.broadcasted_iota(jnp.int32, (tm, tm), 0)
    ci = lax.broadcasted_iota(jnp.int32, (tm, tm), 1)
    before = jnp.where(ci < ri, 1.0, 0.0).astype(BF16)
    prefix = jnp.dot(before, chosen.astype(BF16), preferred_element_type=F32) + carry_ref[...]
    carry_ref[...] = carry_ref[...] + jnp.sum(chosen, axis=0, keepdims=True)
    cnt_ref[...] = carry_ref[...]

    olane = lax.broadcasted_iota(jnp.int32, (tm, LANES), 1)
    route = jnp.zeros((tm, LANES), F32)
    for k in range(TOP_K):
        rank_k = jnp.sum(jnp.where(lane == idxs[k], prefix, 0.0), axis=-1, keepdims=True)
        route = jnp.where(olane == k, idxs[k], route)
        route = jnp.where(olane == TOP_K + k, es[k] / den, route)
        route = jnp.where(olane == 2 * TOP_K + k, rank_k, route)
    route_ref[...] = route


def _outproj_router(att, ssd, x2d, w_out_bf16, ln_g, ln_b, w_router, b_router, alpha):
    m, d = x2d.shape
    tm = min(256, m)
    row = lambda i: (i, 0)
    const = lambda i: (0, 0)
    kern = functools.partial(_outproj_kernel, alpha=alpha)
    return pl.pallas_call(
        kern,
        grid=(m // tm,),
        in_specs=[pl.BlockSpec((tm, ATT_WIDTH), row), pl.BlockSpec((tm, SSD_WIDTH), row), pl.BlockSpec((tm, d), row),
                  pl.BlockSpec((ATT_WIDTH + SSD_WIDTH, d), const, pipeline_mode=pl.Buffered(1)),
                  pl.BlockSpec((1, d), const), pl.BlockSpec((1, d), const),
                  pl.BlockSpec((d, N_EXPERTS), const), pl.BlockSpec((1, N_EXPERTS), const)],
        out_specs=[pl.BlockSpec((tm, d), row), pl.BlockSpec((tm, LANES), row), pl.BlockSpec((1, N_EXPERTS), const)],
        out_shape=[jax.ShapeDtypeStruct((m, d), F32), jax.ShapeDtypeStruct((m, LANES), F32),
                   jax.ShapeDtypeStruct((1, N_EXPERTS), F32)],
        scratch_shapes=[pltpu.VMEM((1, N_EXPERTS), F32)],
        compiler_params=_params(("arbitrary",)),
    )(att, ssd, x2d, w_out_bf16, ln_g.reshape(1, d), ln_b.reshape(1, d), w_router, b_router.reshape(1, -1))


def _row_copy(src, dst, sem):
    return pltpu.make_async_copy(src, dst, sem)


def _scatter_kernel(cnt_ref, pst_ref, dest_ref, h_ref, *rest, first):
    xrows_ref, zero_ref, sem, zsem = rest[-4:]
    i = pl.program_id(0)
    tm = h_ref.shape[0]

    @pl.when((i == 0) & first)
    def _():
        zero_ref[...] = jnp.zeros_like(zero_ref)

        def per_expert(e, _):
            n = cnt_ref[e]
            base = pst_ref[e]
            end = (n + MOE_SUB - 1) // MOE_SUB * MOE_SUB

            def start(r, _):
                _row_copy(zero_ref.at[pl.ds(0, 1)], xrows_ref.at[pl.ds(base + r, 1)], zsem).start()
                return 0

            def wait(r, _):
                _row_copy(zero_ref.at[pl.ds(0, 1)], xrows_ref.at[pl.ds(base + r, 1)], zsem).wait()
                return 0

            lax.fori_loop(n, end, start, 0)
            lax.fori_loop(n, end, wait, 0)
            return 0

        lax.fori_loop(0, N_EXPERTS, per_expert, 0)

    def start(t, _):
        for k in range(TOP_K):
            d = dest_ref[0, 0, t * TOP_K + k]
            _row_copy(h_ref.at[pl.ds(t, 1)], xrows_ref.at[pl.ds(d, 1)], sem).start()
        return 0

    def wait(t, _):
        for k in range(TOP_K):
            d = dest_ref[0, 0, t * TOP_K + k]
            _row_copy(h_ref.at[pl.ds(t, 1)], xrows_ref.at[pl.ds(d, 1)], sem).wait()
        return 0

    lax.fori_loop(0, tm, start, 0)
    lax.fori_loop(0, tm, wait, 0)


def _moe_scatter(h2d, dest, counts, pstart, n_rows, x_rows=None):
    m, d = h2d.shape
    tm = min(256, m)
    nt = m // tm
    dest3 = dest.reshape(nt, 1, tm * TOP_K)
    first = x_rows is None
    in_specs = [pl.BlockSpec((1, 1, tm * TOP_K), lambda i, c, p: (i, 0, 0), memory_space=pltpu.SMEM),
                pl.BlockSpec((tm, d), lambda i, c, p: (i, 0))]
    args = [counts, pstart, dest3, h2d]
    aliases = {}
    if not first:
        in_specs.append(pl.BlockSpec(memory_space=pl.ANY))
        args.append(x_rows)
        aliases = {len(args) - 1: 0}
    grid_spec = pltpu.PrefetchScalarGridSpec(
        num_scalar_prefetch=2,
        grid=(nt,),
        in_specs=in_specs,
        out_specs=pl.BlockSpec(memory_space=pl.ANY),
        scratch_shapes=[pltpu.VMEM((8, d), F32), pltpu.SemaphoreType.DMA(()), pltpu.SemaphoreType.DMA(())],
    )
    return pl.pallas_call(
        functools.partial(_scatter_kernel, first=first),
        grid_spec=grid_spec,
        out_shape=jax.ShapeDtypeStruct((n_rows, d), F32),
        input_output_aliases=aliases,
        compiler_params=_params(("arbitrary",)),
    )(*args)


def _moe_mlp_kernel(ie_ref, ib_ref, iv_ref, x_ref, w1g_ref, w1l_ref, b1g_ref, b1l_ref, w2_ref, b2_ref, o_ref,
                    wg_ref, wl_ref, w2b_ref):
    i = pl.program_id(0)
    j = pl.program_id(1)
    nvalid = iv_ref[i]

    @pl.when(nvalid > 0)
    def _():
        wg_ref[...] = w1g_ref[0].astype(BF16)
        wl_ref[...] = w1l_ref[0].astype(BF16)
        w2b_ref[...] = w2_ref[0].astype(BF16)

    for sub in range(MOE_ROW_TILE // MOE_SUB):
        @pl.when(sub * MOE_SUB < nvalid)
        def _():
            rows = pl.ds(sub * MOE_SUB, MOE_SUB)
            xb = x_ref[rows, :].astype(BF16)
            glu = jnp.dot(xb, wg_ref[...], preferred_element_type=F32) + b1g_ref[0]
            lin = jnp.dot(xb, wl_ref[...], preferred_element_type=F32) + b1l_ref[0]
            glu = jnp.minimum(glu, SWIGLU_LIMIT)
            lin = jnp.clip(lin, -SWIGLU_LIMIT, SWIGLU_LIMIT)
            act = glu * _sigmoid(SWIGLU_ALPHA * glu) * (lin + 1.0)
            y = jnp.dot(act.astype(BF16), w2b_ref[...], preferred_element_type=F32)

            @pl.when(j == 0)
            def _():
                o_ref[rows, :] = y + b2_ref[0]

            @pl.when(j > 0)
            def _():
                o_ref[rows, :] = o_ref[rows, :] + y


def _moe_mlp(x_rows, item_e, item_blk, item_valid, w1, b1, w2, b2):
    n_rows, d = x_rows.shape
    n_items = item_e.shape[0]
    d_ff = w2.shape[1]
    tf = MOE_F_TILE
    nf = d_ff // tf
    tmr = MOE_ROW_TILE

    def jj(i, j, iv):
        return jnp.where(iv[i] > 0, j, nf - 1)

    grid_spec = pltpu.PrefetchScalarGridSpec(
        num_scalar_prefetch=3,
        grid=(n_items, nf),
        in_specs=[pl.BlockSpec((tmr, d), lambda i, j, ie, ib, iv: (ib[i], 0)),
                  pl.BlockSpec((1, d, tf), lambda i, j, ie, ib, iv: (ie[i], 0, jj(i, j, iv))),
                  pl.BlockSpec((1, d, tf), lambda i, j, ie, ib, iv: (ie[i], 0, nf + jj(i, j, iv))),
                  pl.BlockSpec((1, 1, tf), lambda i, j, ie, ib, iv: (ie[i], 0, jj(i, j, iv))),
                  pl.BlockSpec((1, 1, tf), lambda i, j, ie, ib, iv: (ie[i], 0, nf + jj(i, j, iv))),
                  pl.BlockSpec((1, tf, d), lambda i, j, ie, ib, iv: (ie[i], jj(i, j, iv), 0)),
                  pl.BlockSpec((1, 1, d), lambda i, j, ie, ib, iv: (ie[i], 0, 0))],
        out_specs=pl.BlockSpec((tmr, d), lambda i, j, ie, ib, iv: (ib[i], 0)),
        scratch_shapes=[pltpu.VMEM((d, tf), BF16), pltpu.VMEM((d, tf), BF16), pltpu.VMEM((tf, d), BF16)],
    )
    return pl.pallas_call(
        _moe_mlp_kernel,
        grid_spec=grid_spec,
        out_shape=jax.ShapeDtypeStruct((n_rows, d), F32),
        compiler_params=_params(("arbitrary", "arbitrary")),
    )(item_e, item_blk, item_valid, x_rows, w1, w1, b1.reshape(N_EXPERTS, 1, -1), b1.reshape(N_EXPERTS, 1, -1),
      w2, b2.reshape(N_EXPERTS, 1, -1))


def _combine_kernel(dest_ref, h_ref, route_ref, yrows_ref, g_ref, b_ref, o_ref, buf_ref, sem, *, alpha):
    tm = h_ref.shape[0]

    def start(t, _):
        for k in range(TOP_K):
            d = dest_ref[0, 0, t * TOP_K + k]
            _row_copy(yrows_ref.at[pl.ds(d, 1)], buf_ref.at[k, pl.ds(t, 1)], sem).start()
        return 0

    def wait(t, _):
        for k in range(TOP_K):
            d = dest_ref[0, 0, t * TOP_K + k]
            _row_copy(yrows_ref.at[pl.ds(d, 1)], buf_ref.at[k, pl.ds(t, 1)], sem).wait()
        return 0

    lax.fori_loop(0, tm, start, 0)
    lax.fori_loop(0, tm, wait, 0)
    route = route_ref[...]
    acc = route[:, TOP_K:TOP_K + 1] * buf_ref[0]
    for k in range(1, TOP_K):
        acc = acc + route[:, TOP_K + k:TOP_K + k + 1] * buf_ref[k]
    o_ref[...] = _layer_norm(alpha * h_ref[...] + acc, g_ref[...], b_ref[...])


def _moe_combine(h2d, route, dest, y_rows, ln_g, ln_b, alpha):
    m, d = h2d.shape
    tm = min(128, m)
    nt = m // tm
    dest3 = dest.reshape(nt, 1, tm * TOP_K)
    kern = functools.partial(_combine_kernel, alpha=alpha)
    return pl.pallas_call(
        kern,
        grid=(nt,),
        in_specs=[pl.BlockSpec((1, 1, tm * TOP_K), lambda i: (i, 0, 0), memory_space=pltpu.SMEM),
                  pl.BlockSpec((tm, d), lambda i: (i, 0)),
                  pl.BlockSpec((tm, LANES), lambda i: (i, 0)),
                  pl.BlockSpec(memory_space=pl.ANY),
                  pl.BlockSpec((1, d), lambda i: (0, 0)), pl.BlockSpec((1, d), lambda i: (0, 0))],
        out_specs=pl.BlockSpec((tm, d), lambda i: (i, 0)),
        out_shape=jax.ShapeDtypeStruct((m, d), F32),
        scratch_shapes=[pltpu.VMEM((TOP_K, tm, d), F32), pltpu.SemaphoreType.DMA(())],
        compiler_params=_params(("arbitrary",)),
    )(dest3, h2d, route, y_rows, ln_g.reshape(1, d), ln_b.reshape(1, d))


def _moe_ln2(groups, w1, b1, w2, b2, ln_g, ln_b, alpha):
    tmr = MOE_ROW_TILE
    group_counts = [c.reshape(-1).astype(jnp.int32) for _, _, c in groups]
    counts = sum(group_counts)
    tiles = (counts + tmr - 1) // tmr
    tile_end = jnp.cumsum(tiles)
    tile_start = tile_end - tiles
    pstart = (tile_start * tmr).astype(jnp.int32)
    n_tok = sum(h.shape[0] for h, _, _ in groups)
    n_items = -(-(n_tok * TOP_K) // tmr) + N_EXPERTS
    n_rows = n_items * tmr
    it = jnp.arange(n_items, dtype=jnp.int32)
    total = tile_end[-1]
    it_c = jnp.minimum(it, total - 1)
    item_e = jnp.minimum(jnp.searchsorted(tile_end, it_c, side='right'), N_EXPERTS - 1).astype(jnp.int32)
    item_r = it_c - tile_start[item_e]
    item_blk = (tile_start[item_e] + item_r).astype(jnp.int32)
    item_valid = jnp.where(it < total, jnp.clip(counts[item_e] - item_r * tmr, 0, tmr), 0).astype(jnp.int32)

    dests = []
    earlier = jnp.zeros_like(counts)
    x_rows = None
    for (h2d, route, _), gc in zip(groups, group_counts):
        idx = route[:, 0:TOP_K].astype(jnp.int32)
        rank = route[:, 2 * TOP_K:3 * TOP_K].astype(jnp.int32)
        dest = (pstart[idx] + earlier[idx] + rank).reshape(-1)
        dests.append(dest)
        earlier = earlier + gc
        x_rows = _moe_scatter(h2d, dest, counts, pstart, n_rows, x_rows)
    y_rows = _moe_mlp(x_rows, item_e, item_blk, item_valid, w1, b1, w2, b2)
    return [_moe_combine(h2d, route, dest, y_rows, ln_g, ln_b, alpha)
            for (h2d, route, _), dest in zip(groups, dests)]


def kernel(x_prompt, x_sample, cache_k, cache_v, page_table, state_ssm, state_conv, w_in, w_out, lambda_qk,
           attn_subln_w, conv_w, conv_b, dt_bias, a_log, d_skip, ssd_norm_w, ln1_g, ln1_b, w_router, b_router,
           w_mlp1, b_mlp1, w_mlp2, b_mlp2, ln2_g, ln2_b):
    bp, sp, d = x_prompt.shape
    bd, sd, _ = x_sample.shape
    depth = w_in.shape[0]
    assert depth == 1 and sd == 1, "kernel supports the single-layer, single-token-decode configuration"
    alpha = (2.0 * depth) ** 0.25
    slopes = 2.0 ** (-8.0 * jnp.arange(1, ATT_HEADS + 1, dtype=F32) / ATT_HEADS)
    l = 0
    lam_init = 0.8 - 0.6 * math.exp(-0.3 * l)

    w_in_b = _cast_bf16(w_in[l])
    w_out_b = _cast_bf16(w_out[l])
    moe_w = (w_mlp1[l], b_mlp1[l], w_mlp2[l], b_mlp2[l], ln2_g[l], ln2_b[l])
    ssd_w = (conv_w[l], conv_b[l], dt_bias[l], a_log[l], d_skip[l], ssd_norm_w[l])

    xp = x_prompt.reshape(bp * sp, d)
    q, k, v, z, xbc, dt = _inproj(xp, w_in_b, BF16)
    att = _attn_prompt(q.reshape(bp, sp, -1), k.reshape(bp, sp, -1), v.reshape(bp, sp, -1), lambda_qk[l],
                       attn_subln_w[l], slopes, lam_init)
    ssm_zero = jnp.zeros((bp, SSD_HEADS, SSD_HEADDIM, D_STATE), F32)
    conv_zero = jnp.zeros((bp, CONV_W - 1, CONV_DIM), F32)
    ssd, ssm_p, conv_p = _ssd_prompt(xbc.reshape(bp, sp, -1), dt.reshape(bp, sp, -1), z.reshape(bp, sp, -1),
                                     ssm_zero, conv_zero, *ssd_w)
    h, route, counts = _outproj_router(att.reshape(bp * sp, -1), ssd.reshape(bp * sp, -1), xp, w_out_b,
                                       ln1_g[l], ln1_b[l], w_router[l], b_router[l], alpha)

    xs = x_sample.reshape(bd, d)
    qs, ks, vs, zs, xbcs, dts = _inproj(xs, w_in_b, F32)
    att_s = _attn_decode(qs, ks, vs, cache_k, cache_v, page_table, lambda_qk[l], attn_subln_w[l], slopes,
                         lam_init)
    ssd_s, ssm_s, conv_s = _ssd_sample(xbcs, dts, zs, state_ssm[l], state_conv[l], *ssd_w)
    hs, route_s, counts_s = _outproj_router(att_s, ssd_s, xs, w_out_b, ln1_g[l], ln1_b[l], w_router[l],
                                            b_router[l], alpha)
    y_prompt, y_sample = _moe_ln2([(h, route, counts), (hs, route_s, counts_s)], *moe_w, alpha)
    y_prompt = y_prompt.reshape(bp, sp, d)
    y_sample = y_sample.reshape(bd, sd, d)

    hshape = (ATT_HEADS, 2 * ATT_DH)
    return (y_prompt, y_sample,
            k.reshape(1, bp, sp, *hshape), v.reshape(1, bp, sp, *hshape), ssm_p[None], conv_p[None],
            ks.reshape(1, bd, sd, *hshape), vs.reshape(1, bd, sd, *hshape), ssm_s[None], conv_s[None])
```

```python
import functools
import math

import jax
import jax.numpy as jnp
from jax import lax
from jax.experimental import pallas as pl
from jax.experimental.pallas import tpu as pltpu

F32 = jnp.float32
BF16 = jnp.bfloat16

ATT_DH = 64
ATT_HEADS = 8
ATT_WIDTH = ATT_HEADS * 2 * ATT_DH
ATT_SCALE = ATT_DH ** -0.5
SSD_HEADDIM = 64
SSD_HEADS = 16
SSD_WIDTH = SSD_HEADS * SSD_HEADDIM
SSD_GROUPS = 2
D_STATE = 128
CONV_W = 4
CONV_DIM = SSD_WIDTH + 2 * SSD_GROUPS * D_STATE
SSD_CHUNK = 128
N_EXPERTS = 32
TOP_K = 4
SWIGLU_ALPHA = 1.702
SWIGLU_LIMIT = 7.0
LN_EPS = 1e-5
RMS_EPS = 1e-5
PAGE_SIZE = 128

V7X_VMEM_LIMIT_BYTES = 56 * 1024 * 1024
LANES = 128

MOE_ROW_TILE = 1024
MOE_SUB = 256
MOE_F_TILE = 256


def _params(sem, vmem=V7X_VMEM_LIMIT_BYTES):
    return pltpu.CompilerParams(dimension_semantics=sem, vmem_limit_bytes=vmem)


def _sigmoid(x):
    return 1.0 / (1.0 + jnp.exp(-x))


def _silu(x):
    return x * _sigmoid(x)


def _softplus(x):
    return jnp.maximum(x, 0.0) + jnp.log1p(jnp.exp(-jnp.abs(x)))


def _layer_norm(x, g, b):
    mu = jnp.mean(x, axis=-1, keepdims=True)
    xc = x - mu
    var = jnp.mean(xc * xc, axis=-1, keepdims=True)
    return xc * lax.rsqrt(var + LN_EPS) * g + b


def _split3(x):
    a = x.astype(BF16)
    r = x - a.astype(F32)
    b = r.astype(BF16)
    c = (r - b.astype(F32)).astype(BF16)
    return a, b, c


def _lam(lq):
    s01 = jnp.sum(lq[0:1, :] * lq[1:2, :], axis=-1, keepdims=True)
    s23 = jnp.sum(lq[2:3, :] * lq[3:4, :], axis=-1, keepdims=True)
    return jnp.exp(s01) - jnp.exp(s23)


def _cast_kernel(w_ref, o_ref):
    o_ref[...] = w_ref[...].astype(BF16)


def _cast_bf16(w, row_tile=256):
    r, c = w.shape
    rt = min(row_tile, r)
    return pl.pallas_call(
        _cast_kernel,
        grid=(r // rt,),
        in_specs=[pl.BlockSpec((rt, c), lambda i: (i, 0))],
        out_specs=pl.BlockSpec((rt, c), lambda i: (i, 0)),
        out_shape=jax.ShapeDtypeStruct((r, c), BF16),
        compiler_params=_params(("arbitrary",)),
    )(w)


def _inproj_kernel(x_ref, w_ref, q_ref, k_ref, v_ref, z_ref, xbc_ref, dt_ref):
    xb = x_ref[...].astype(BF16)

    def mm(c0, c1):
        return jnp.dot(xb, w_ref[:, c0:c1], preferred_element_type=F32)

    a = ATT_WIDTH
    q_ref[...] = (mm(0, a) * ATT_SCALE).astype(q_ref.dtype)
    k_ref[...] = mm(a, 2 * a)
    v_ref[...] = mm(2 * a, 3 * a)
    z_ref[...] = mm(3 * a, 3 * a + SSD_WIDTH)
    c0 = 3 * a + SSD_WIDTH
    xbc_ref[...] = mm(c0, c0 + CONV_DIM)
    dt_ref[...] = mm(c0 + CONV_DIM, c0 + CONV_DIM + SSD_HEADS)


def _inproj(x2d, w_in_bf16, q_dtype):
    m, d = x2d.shape
    ncol = w_in_bf16.shape[1]
    tm = min(256, m)
    row = lambda i: (i, 0)
    widths = (ATT_WIDTH, ATT_WIDTH, ATT_WIDTH, SSD_WIDTH, CONV_DIM, SSD_HEADS)
    dtypes = (q_dtype, F32, F32, F32, F32, F32)
    return pl.pallas_call(
        _inproj_kernel,
        grid=(m // tm,),
        in_specs=[pl.BlockSpec((tm, d), row),
                  pl.BlockSpec((d, ncol), lambda i: (0, 0), pipeline_mode=pl.Buffered(1))],
        out_specs=[pl.BlockSpec((tm, w), row) for w in widths],
        out_shape=[jax.ShapeDtypeStruct((m, w), dt) for w, dt in zip(widths, dtypes)],
        compiler_params=_params(("arbitrary",)),
    )(x2d, w_in_bf16)


def _attn_prompt_kernel(slopes_ref, qt_ref, k_ref, v_ref, lq_ref, w_ref, o_ref, kb_ref, vt_ref, *, tq, lam_init):
    h = pl.program_id(1)
    qi = pl.program_id(2)
    hd = 2 * ATT_DH
    ones_rows = vt_ref.shape[0] - hd

    @pl.when(qi == 0)
    def _():
        kb_ref[...] = k_ref[0].astype(BF16)
        vt_ref[0:hd, :] = v_ref[0].T.astype(BF16)
        vt_ref[hd:hd + ones_rows, :] = jnp.ones((ones_rows, vt_ref.shape[1]), BF16)

    slope = slopes_ref[h]
    qt = qt_ref[0]
    drow = lax.broadcasted_iota(jnp.int32, (hd, tq), 0)
    zero = jnp.zeros_like(qt)
    qts = (jnp.where(drow < ATT_DH, qt, zero), jnp.where(drow >= ATT_DH, qt, zero))
    kr = lax.broadcasted_iota(jnp.int32, (tq, tq), 0)
    qc = lax.broadcasted_iota(jnp.int32, (tq, tq), 1)
    base = (qc - kr).astype(F32) * slope
    future = kr > qc

    def scores(j):
        kb = kb_ref[pl.ds(pl.multiple_of(j * tq, tq), tq), :]
        return tuple(jnp.dot(kb, qts[mi], preferred_element_type=F32) for mi in range(2))

    def block(j, raw, stats, diag):
        vta = vt_ref[:, pl.ds(pl.multiple_of(j * tq, tq), tq)]
        off = jnp.full((1, 1), (qi - j) * tq, jnp.int32).astype(F32) * slope
        bias = base + off
        out = []
        for mi in range(2):
            m, l, a = stats[mi]
            s = raw[mi] - bias
            if diag:
                s = jnp.where(future, -jnp.inf, s)
            mn = jnp.maximum(m, jnp.max(s, axis=0, keepdims=True))
            p = jnp.exp(s - mn)
            al = jnp.exp(m - mn)
            pv = jnp.dot(vta, p.astype(BF16), preferred_element_type=F32)
            l = al * l + pv[hd:hd + 1, :]
            a = al * a + pv[0:hd, :]
            out.append((mn, l, a))
        return tuple(out)

    def step(j, carry):
        raw, stats = carry
        nxt = scores(j + 1)
        return nxt, block(j, raw, stats, False)

    init1 = (jnp.full((1, tq), -1e30, F32), jnp.zeros((1, tq), F32), jnp.zeros((hd, tq), F32))
    raw, stats = lax.fori_loop(0, qi, step, (scores(0), (init1, init1)))
    (_, l0, a0), (_, l1, a1) = block(qi, raw, stats, True)
    lam = _lam(lq_ref[...]) + lam_init
    o = a0 / l0 - lam * (a1 / l1)
    o = o * lax.rsqrt(jnp.mean(o * o, axis=0, keepdims=True) + RMS_EPS) * w_ref[...] * (1.0 - lam_init)
    o_ref[0] = o.T.astype(BF16)


def _attn_prompt(q, k, v, lambda_qk, subln_w, slopes, lam_init):
    b, t, _ = k.shape
    tq = min(256, t)
    hd = 2 * ATT_DH
    ones_rows = 16
    qt = jnp.swapaxes(q, 1, 2)
    kern = functools.partial(_attn_prompt_kernel, tq=tq, lam_init=lam_init)
    return pl.pallas_call(
        kern,
        grid=(b, ATT_HEADS, t // tq),
        in_specs=[pl.BlockSpec(memory_space=pltpu.SMEM),
                  pl.BlockSpec((1, hd, tq), lambda bi, h, qi: (bi, h, qi)),
                  pl.BlockSpec((1, t, hd), lambda bi, h, qi: (bi, 0, h)),
                  pl.BlockSpec((1, t, hd), lambda bi, h, qi: (bi, 0, h)),
                  pl.BlockSpec((4, ATT_DH), lambda bi, h, qi: (0, 0)),
                  pl.BlockSpec((hd, 1), lambda bi, h, qi: (0, 0))],
        out_specs=pl.BlockSpec((1, tq, hd), lambda bi, h, qi: (bi, qi, h)),
        out_shape=jax.ShapeDtypeStruct((b, t, ATT_WIDTH), BF16),
        scratch_shapes=[pltpu.VMEM((t, hd), BF16), pltpu.VMEM((hd + ones_rows, t), BF16)],
        compiler_params=_params(("arbitrary", "arbitrary", "arbitrary")),
    )(slopes, qt, k, v, lambda_qk, subln_w.reshape(hd, 1))


def _attn_decode_kernel(pt_ref, q_ref, kn_ref, vn_ref, slope_ref, lq_ref, w_ref, *rest, pps, past_len, lam_init):
    kp_refs, vp_refs = rest[:pps], rest[pps:2 * pps]
    o_ref, qt_ref, s_ref, a_ref, m_ref, snew_ref, anew_ref, acc_ref = rest[2 * pps:]
    ph = pl.program_id(1)
    p = pl.program_id(2)
    n_steps = pl.num_programs(2)
    nh = ATT_HEADS
    nrow = 2 * nh
    hd = 2 * ATT_DH
    plane = PAGE_SIZE * nh
    n_pages = past_len // PAGE_SIZE
    per_vreg = LANES // nh

    def page_lanes(page):
        return pl.ds(pl.multiple_of(page * plane, plane), plane)

    @pl.when((ph == 0) & (p == 0))
    def _():
        q8 = q_ref[0]
        lane = lax.broadcasted_iota(jnp.int32, (nh, hd), 1)
        qt = jnp.concatenate([jnp.where(lane < ATT_DH, q8, 0.0), jnp.where(lane >= ATT_DH, q8, 0.0)], axis=0)
        qt_ref[...] = qt.astype(BF16)
        kn = jnp.concatenate([kn_ref[0], kn_ref[0]], axis=0)
        s_new = jnp.sum(qt * kn, axis=-1, keepdims=True)
        snew_ref[...] = s_new
        m_ref[...] = s_new

    @pl.when(ph == 0)
    def _():
        lane = lax.broadcasted_iota(jnp.int32, (nrow, plane), 1)
        row = lax.broadcasted_iota(jnp.int32, (nrow, plane), 0)
        own_head = (lane % nh) == (row % nh)
        for i in range(pps):
            page = p * pps + i
            kflat = kp_refs[i][0].reshape(plane, hd).astype(BF16)
            s = lax.dot_general(qt_ref[...], kflat, (((1,), (1,)), ((), ())), preferred_element_type=F32)
            dist = (past_len - page * PAGE_SIZE - lane // nh).astype(F32)
            s = jnp.where(own_head, s - slope_ref[...] * dist, -jnp.inf)
            s_ref[:, page_lanes(page)] = s
            m_ref[...] = jnp.maximum(m_ref[...], jnp.max(s, axis=-1, keepdims=True))

    @pl.when((ph == 1) & (p == 0))
    def _():
        m = m_ref[...]
        e_new = jnp.exp(snew_ref[...] - m)

        def expsum(g, part):
            for u in range(pps):
                e = jnp.exp(s_ref[:, page_lanes(g * pps + u)] - m)
                s_ref[:, page_lanes(g * pps + u)] = e
                for c in range(plane // LANES):
                    part = part + e[:, c * LANES:(c + 1) * LANES]
            return part

        part = lax.fori_loop(0, n_pages // pps, expsum, jnp.zeros((nrow, LANES), F32))
        den = jnp.sum(part, axis=-1, keepdims=True) + e_new
        inv = 1.0 / den
        lam = _lam(lq_ref[...]) + lam_init

        def combine(g, _):
            for u in range(pps):
                pn = s_ref[:, page_lanes(g * pps + u)] * inv
                a_ref[:, page_lanes(g * pps + u)] = pn[0:nh] - lam * pn[nh:nrow]
            return 0

        lax.fori_loop(0, n_pages // pps, combine, 0)
        pn_new = e_new * inv
        anew_ref[...] = pn_new[0:nh] - lam * pn_new[nh:nrow]
        acc_ref[...] = jnp.zeros_like(acc_ref)

    @pl.when(ph == 1)
    def _():
        lane = lax.broadcasted_iota(jnp.int32, (nh, LANES), 1)
        n_acc = acc_ref.shape[0]
        for i in range(pps):
            page = p * pps + i

            accs = [acc_ref[k] for k in range(n_acc)]
            for g in range(plane // LANES):
                av = a_ref[:, pl.ds(pl.multiple_of(page * plane + g * LANES, LANES), LANES)]
                for jj in range(per_vreg):
                    sel = (lane >= jj * nh) & (lane < (jj + 1) * nh)
                    wcol = jnp.sum(jnp.where(sel, av, 0.0), axis=-1, keepdims=True)
                    accs[jj % n_acc] = accs[jj % n_acc] + wcol * vp_refs[i][0, g * per_vreg + jj]
            for k in range(n_acc):
                acc_ref[k] = accs[k]

    @pl.when((ph == 1) & (p == n_steps - 1))
    def _():
        o = anew_ref[...] * vn_ref[0]
        for k in range(acc_ref.shape[0]):
            o = o + acc_ref[k]
        o = o * lax.rsqrt(jnp.mean(o * o, axis=-1, keepdims=True) + RMS_EPS) * w_ref[...] * (1.0 - lam_init)
        o_ref[0] = o.astype(BF16)


def _attn_decode(q, k_new, v_new, cache_k, cache_v, page_table, lambda_qk, subln_w, slopes, lam_init):
    bd = q.shape[0]
    n_pages = page_table.shape[1]
    nh = ATT_HEADS
    nrow = 2 * nh
    hd = 2 * ATT_DH
    pps = max(c for c in (1, 2, 4) if n_pages % c == 0)
    n_steps = n_pages // pps
    kp = cache_k.reshape(-1, PAGE_SIZE, nh, hd)
    vp = cache_v.reshape(-1, PAGE_SIZE, nh, hd)
    slope_rows = jnp.tile(slopes, 2).reshape(nrow, 1)
    past_len = n_pages * PAGE_SIZE
    kern = functools.partial(_attn_decode_kernel, pps=pps, past_len=past_len, lam_init=lam_init)
    head3 = lambda b, ph, p, pt: (b, 0, 0)
    const2 = lambda b, ph, p, pt: (0, 0)

    def kpage(i):
        return lambda b, ph, p, pt: (pt[b * n_pages + jnp.where(ph == 0, p, n_steps - 1) * pps + i], 0, 0, 0)

    def vpage(i):
        return lambda b, ph, p, pt: (pt[b * n_pages + jnp.where(ph == 0, 0, p) * pps + i], 0, 0, 0)

    page_block = (1, PAGE_SIZE, nh, hd)
    n_acc = 4
    grid_spec = pltpu.PrefetchScalarGridSpec(
        num_scalar_prefetch=1,
        grid=(bd, 2, n_steps),
        in_specs=[pl.BlockSpec((1, nh, hd), head3),
                  pl.BlockSpec((1, nh, hd), head3),
                  pl.BlockSpec((1, nh, hd), head3),
                  pl.BlockSpec((nrow, 1), const2),
                  pl.BlockSpec((4, ATT_DH), const2),
                  pl.BlockSpec((1, hd), const2)]
                 + [pl.BlockSpec(page_block, kpage(i)) for i in range(pps)]
                 + [pl.BlockSpec(page_block, vpage(i)) for i in range(pps)],
        out_specs=pl.BlockSpec((1, nh, hd), head3),
        scratch_shapes=[pltpu.VMEM((nrow, hd), BF16), pltpu.VMEM((nrow, past_len * nh), F32),
                        pltpu.VMEM((nh, past_len * nh), F32), pltpu.VMEM((nrow, 1), F32),
                        pltpu.VMEM((nrow, 1), F32), pltpu.VMEM((nh, 1), F32),
                        pltpu.VMEM((n_acc, nh, hd), F32)],
    )
    out = pl.pallas_call(
        kern,
        grid_spec=grid_spec,
        out_shape=jax.ShapeDtypeStruct((bd, nh, hd), BF16),
        compiler_params=_params(("arbitrary", "arbitrary", "arbitrary")),
    )(page_table.reshape(-1), q.reshape(bd, nh, hd), k_new.reshape(bd, nh, hd), v_new.reshape(bd, nh, hd),
      slope_rows, lambda_qk, subln_w.reshape(1, hd), *([kp] * pps), *([vp] * pps))
    return out.reshape(bd, ATT_WIDTH)


def _gated_group_norm(y, z, w):
    yg = y * _silu(z)
    gw = SSD_WIDTH // SSD_GROUPS
    parts = []
    for g in range(SSD_GROUPS):
        v = yg[:, g * gw:(g + 1) * gw]
        parts.append(v * lax.rsqrt(jnp.mean(v * v, axis=-1, keepdims=True) + RMS_EPS) * w[:, g * gw:(g + 1) * gw])
    return parts


def _ssd_prompt_kernel(xbc_ref, dt_ref, dtt_ref, z_ref, cw_ref, cb_ref, dtb_ref, dtbt_ref, al_ref, alt_ref,
                       dsk_ref, nw_ref, h0_ref, c0_ref, y_ref, st_ref, cv_ref, xpad_ref, ysc_ref, xdd_ref):
    c = pl.program_id(1)
    nc = pl.num_programs(1)
    L = SSD_CHUNK
    P = SSD_HEADDIM
    hpg = SSD_HEADS // SSD_GROUPS
    halo = 8

    @pl.when(c == 0)
    def _():
        st_ref[...] = h0_ref[...]
        xpad_ref[0:halo, :] = jnp.zeros((halo, CONV_DIM), F32)
        xpad_ref[halo - (CONV_W - 1):halo, :] = c0_ref[0]

    xc = xbc_ref[0]
    xpad_ref[halo:halo + L, :] = xc
    conv = cb_ref[...]
    for i in range(CONV_W - 1):
        sh = CONV_W - 1 - i
        conv = conv + xpad_ref[halo - sh:halo - sh + L, :] * cw_ref[i:i + 1, :]
    conv = conv + xc * cw_ref[CONV_W - 1:CONV_W, :]
    tail = xc[L - (CONV_W - 1):L, :]
    xpad_ref[halo - (CONV_W - 1):halo, :] = tail

    @pl.when(c == nc - 1)
    def _():
        cv_ref[0] = tail

    act = _silu(conv)
    xs = act[:, :SSD_WIDTH]
    bmat = [act[:, SSD_WIDTH + g * D_STATE:SSD_WIDTH + (g + 1) * D_STATE].astype(BF16) for g in range(SSD_GROUPS)]
    c_off = SSD_WIDTH + SSD_GROUPS * D_STATE
    cmat = [act[:, c_off + g * D_STATE:c_off + (g + 1) * D_STATE].astype(BF16) for g in range(SSD_GROUPS)]

    dtp = _softplus(dt_ref[0] + dtb_ref[...])
    dtpt = _softplus(dtt_ref[0] + dtbt_ref[...])
    da = dtp * (-jnp.exp(al_ref[...]))
    dat = dtpt * (-jnp.exp(alt_ref[...]))
    ri = lax.broadcasted_iota(jnp.int32, (L, L), 0)
    ci = lax.broadcasted_iota(jnp.int32, (L, L), 1)
    causal = ri >= ci
    tri = jnp.where(causal, 1.0, 0.0).astype(BF16)
    trit = jnp.where(ci >= ri, 1.0, 0.0).astype(BF16)
    cs = sum(jnp.dot(tri, part, preferred_element_type=F32) for part in _split3(da))
    cst = sum(jnp.dot(part, trit, preferred_element_type=F32) for part in _split3(dat))

    cb = [lax.dot_general(cmat[g], bmat[g], (((1,), (1,)), ((), ())), preferred_element_type=F32)
          for g in range(SSD_GROUPS)]
    dsk = dsk_ref[...]

    for h in range(SSD_HEADS):
        g = h // hpg
        cs_col = cs[:, h:h + 1]
        diff = cs_col - cst[h:h + 1, :]
        lmat = jnp.exp(jnp.where(causal, diff, -jnp.inf))
        mmat = (cb[g] * lmat).astype(BF16)
        xs_h = xs[:, h * P:(h + 1) * P]
        xd_h = xs_h * dtp[:, h:h + 1]
        y = jnp.dot(mmat, xd_h.astype(BF16), preferred_element_type=F32)
        st = st_ref[0, h]
        yoff = lax.dot_general(cmat[g], st.astype(BF16), (((1,), (1,)), ((), ())), preferred_element_type=F32)
        y = y + jnp.exp(cs_col) * yoff + dsk[:, h:h + 1] * xs_h
        ysc_ref[:, h * P:(h + 1) * P] = y
        cs_last = cs[L - 1:L, h:h + 1]
        xdd_ref[:, h * P:(h + 1) * P] = xd_h * jnp.exp(cs_last - cs_col)

    xddt = xdd_ref[...].T
    for h in range(SSD_HEADS):
        g = h // hpg
        new = jnp.dot(xddt[h * P:(h + 1) * P, :].astype(BF16), bmat[g], preferred_element_type=F32)
        cs_last = cs[L - 1:L, h:h + 1]
        st_ref[0, h] = jnp.exp(cs_last) * st_ref[0, h] + new

    parts = _gated_group_norm(ysc_ref[...], z_ref[0], nw_ref[...])
    gw = SSD_WIDTH // SSD_GROUPS
    for g in range(SSD_GROUPS):
        y_ref[0, :, g * gw:(g + 1) * gw] = parts[g].astype(BF16)


def _ssd_prompt(xbc, dt, z, h0, conv0, conv_w, conv_b, dt_bias, a_log, d_skip, norm_w):
    b, t, _ = xbc.shape
    L = SSD_CHUNK
    nc = t // L
    dtt = jnp.swapaxes(dt, 1, 2)
    seq = lambda bi, ci: (bi, ci, 0)
    const2 = lambda bi, ci: (0, 0)
    full2 = lambda shp: pl.BlockSpec(shp, const2)
    return pl.pallas_call(
        _ssd_prompt_kernel,
        grid=(b, nc),
        in_specs=[pl.BlockSpec((1, L, CONV_DIM), seq),
                  pl.BlockSpec((1, L, SSD_HEADS), seq),
                  pl.BlockSpec((1, SSD_HEADS, L), lambda bi, ci: (bi, 0, ci)),
                  pl.BlockSpec((1, L, SSD_WIDTH), seq),
                  full2((CONV_W, CONV_DIM)), full2((1, CONV_DIM)),
                  full2((1, SSD_HEADS)), full2((SSD_HEADS, 1)),
                  full2((1, SSD_HEADS)), full2((SSD_HEADS, 1)),
                  full2((1, SSD_HEADS)), full2((1, SSD_WIDTH)),
                  pl.BlockSpec((1, SSD_HEADS, SSD_HEADDIM, D_STATE), lambda bi, ci: (bi, 0, 0, 0)),
                  pl.BlockSpec((1, CONV_W - 1, CONV_DIM), lambda bi, ci: (bi, 0, 0))],
        out_specs=[pl.BlockSpec((1, L, SSD_WIDTH), seq),
                   pl.BlockSpec((1, SSD_HEADS, SSD_HEADDIM, D_STATE), lambda bi, ci: (bi, 0, 0, 0)),
                   pl.BlockSpec((1, CONV_W - 1, CONV_DIM), lambda bi, ci: (bi, 0, 0))],
        out_shape=[jax.ShapeDtypeStruct((b, t, SSD_WIDTH), BF16),
                   jax.ShapeDtypeStruct((b, SSD_HEADS, SSD_HEADDIM, D_STATE), F32),
                   jax.ShapeDtypeStruct((b, CONV_W - 1, CONV_DIM), F32)],
        scratch_shapes=[pltpu.VMEM((8 + L, CONV_DIM), F32), pltpu.VMEM((L, SSD_WIDTH), F32),
                        pltpu.VMEM((L, SSD_WIDTH), F32)],
        compiler_params=_params(("arbitrary", "arbitrary")),
    )(xbc, dt, dtt, z, conv_w, conv_b.reshape(1, -1), dt_bias.reshape(1, -1), dt_bias.reshape(-1, 1),
      a_log.reshape(1, -1), a_log.reshape(-1, 1), d_skip.reshape(1, -1), norm_w.reshape(1, -1), h0, conv0)


def _bf16_round(v):
    return v.astype(BF16).astype(F32)


def _ssd_sample_kernel(xbc_ref, ci_ref, dt_ref, z_ref, cw_ref, cb_ref, dtb_ref, al_ref, dsk_ref, nw_ref, h0_ref,
                       xbc8_ref, ci8_ref, dt8_ref, cw8_ref, cb8_ref, dtb8_ref, y_ref, st_ref, cv_ref, ysc_ref):
    P = SSD_HEADDIM
    hpg = SSD_HEADS // SSD_GROUPS
    xrow = xbc_ref[0]
    hist = ci_ref[0]
    conv = cb_ref[...]
    for i in range(CONV_W - 1):
        conv = conv + hist[i:i + 1, :] * cw_ref[i:i + 1, :]
    conv = conv + xrow * cw_ref[CONV_W - 1:CONV_W, :]
    cv_ref[0, 0:CONV_W - 2, :] = hist[1:CONV_W - 1, :]
    cv_ref[0, CONV_W - 2:CONV_W - 1, :] = xrow
    act = _silu(conv)
    xs = act[:, :SSD_WIDTH]
    c_off = SSD_WIDTH + SSD_GROUPS * D_STATE
    dtp = _softplus(dt_ref[0] + dtb_ref[...])
    decay = jnp.exp(dtp * (-jnp.exp(al_ref[...])))
    dsk = dsk_ref[...]

    nx = SSD_WIDTH // LANES
    conv8 = cb8_ref[0:nx, :]
    for i in range(CONV_W - 1):
        conv8 = conv8 + ci8_ref[0, i, 0:nx, :] * cw8_ref[i, 0:nx, :]
    conv8 = conv8 + xbc8_ref[0, 0:nx, :] * cw8_ref[CONV_W - 1, 0:nx, :]
    xd8 = _silu(conv8) * _softplus(dt8_ref[0] + dtb8_ref[...])
    xdt = xd8.T

    brows = [act[:, SSD_WIDTH + g * D_STATE:SSD_WIDTH + (g + 1) * D_STATE] for g in range(SSD_GROUPS)]
    crows = [act[:, c_off + g * D_STATE:c_off + (g + 1) * D_STATE] for g in range(SSD_GROUPS)]
    cbs = [jnp.sum(_bf16_round(brows[g]) * _bf16_round(crows[g]), axis=-1, keepdims=True) for g in range(SSD_GROUPS)]
    c8s = [jnp.broadcast_to(crows[g], (8, D_STATE)).astype(BF16) for g in range(SSD_GROUPS)]
    hpr = LANES // P
    for h in range(SSD_HEADS):
        g = h // hpg
        xs_h = xs[:, h * P:(h + 1) * P]
        xd_h = xs_h * dtp[:, h:h + 1]
        xcol = xdt[(h % hpr) * P:(h % hpr + 1) * P, h // hpr:h // hpr + 1]
        dec = decay[:, h:h + 1]
        h0q = h0_ref[0, h].astype(BF16)
        st_ref[0, h] = _bf16_round(dec) * h0q.astype(F32) + _bf16_round(xcol * brows[g])
        yoff = lax.dot_general(c8s[g], h0q, (((1,), (1,)), ((), ())), preferred_element_type=F32)[0:1, :]
        ysc_ref[:, h * P:(h + 1) * P] = (cbs[g] * xd_h + dec * yoff) + dsk[:, h:h + 1] * xs_h
    parts = _gated_group_norm(ysc_ref[...], z_ref[0], nw_ref[...])
    gw = SSD_WIDTH // SSD_GROUPS
    for g in range(SSD_GROUPS):
        y_ref[0, :, g * gw:(g + 1) * gw] = parts[g].astype(BF16)


def _ssd_sample(xbc, dt, z, h0, conv0, conv_w, conv_b, dt_bias, a_log, d_skip, norm_w):
    bd = xbc.shape[0]
    nr = CONV_DIM // LANES
    nx = SSD_WIDTH // LANES
    rep = SSD_HEADDIM
    row3 = lambda b: (b, 0, 0)
    const2 = lambda b: (0, 0)
    full2 = lambda shp: pl.BlockSpec(shp, const2)
    state = pl.BlockSpec((1, SSD_HEADS, SSD_HEADDIM, D_STATE), lambda b: (b, 0, 0, 0))
    y, st, cv = pl.pallas_call(
        _ssd_sample_kernel,
        grid=(bd,),
        in_specs=[pl.BlockSpec((1, 1, CONV_DIM), row3),
                  pl.BlockSpec((1, CONV_W - 1, CONV_DIM), row3),
                  pl.BlockSpec((1, 1, SSD_HEADS), row3),
                  pl.BlockSpec((1, 1, SSD_WIDTH), row3),
                  full2((CONV_W, CONV_DIM)), full2((1, CONV_DIM)), full2((1, SSD_HEADS)), full2((1, SSD_HEADS)),
                  full2((1, SSD_HEADS)), full2((1, SSD_WIDTH)), state,
                  pl.BlockSpec((1, nr, LANES), row3),
                  pl.BlockSpec((1, CONV_W - 1, nr, LANES), lambda b: (b, 0, 0, 0)),
                  pl.BlockSpec((1, nx, LANES), row3),
                  pl.BlockSpec((CONV_W, nr, LANES), lambda b: (0, 0, 0)),
                  full2((nr, LANES)), full2((nx, LANES))],
        out_specs=[pl.BlockSpec((1, 1, SSD_WIDTH), row3), state,
                   pl.BlockSpec((1, CONV_W - 1, CONV_DIM), row3)],
        out_shape=[jax.ShapeDtypeStruct((bd, 1, SSD_WIDTH), BF16),
                   jax.ShapeDtypeStruct((bd, SSD_HEADS, SSD_HEADDIM, D_STATE), F32),
                   jax.ShapeDtypeStruct((bd, CONV_W - 1, CONV_DIM), F32)],
        scratch_shapes=[pltpu.VMEM((1, SSD_WIDTH), F32)],
        compiler_params=_params(("arbitrary",)),
    )(xbc.reshape(bd, 1, CONV_DIM), conv0, dt.reshape(bd, 1, SSD_HEADS), z.reshape(bd, 1, SSD_WIDTH),
      conv_w, conv_b.reshape(1, -1), dt_bias.reshape(1, -1), a_log.reshape(1, -1), d_skip.reshape(1, -1),
      norm_w.reshape(1, -1), h0,
      xbc.reshape(bd, nr, LANES), conv0.reshape(bd, CONV_W - 1, nr, LANES),
      jnp.repeat(dt, rep, axis=-1).reshape(bd, nx, LANES), conv_w.reshape(CONV_W, nr, LANES),
      conv_b.reshape(nr, LANES), jnp.repeat(dt_bias, rep).reshape(nx, LANES))
    return y.reshape(bd, SSD_WIDTH), st, cv


def _outproj_kernel(att_ref, ssd_ref, x_ref, w_ref, g_ref, b_ref, wr_ref, br_ref, h_ref, route_ref, cnt_ref,
                    carry_ref, *, alpha):
    i = pl.program_id(0)
    tm = x_ref.shape[0]

    @pl.when(i == 0)
    def _():
        carry_ref[...] = jnp.zeros_like(carry_ref)

    mix = jnp.dot(att_ref[...], w_ref[0:ATT_WIDTH, :], preferred_element_type=F32)
    mix = mix + jnp.dot(ssd_ref[...], w_ref[ATT_WIDTH:ATT_WIDTH + SSD_WIDTH, :], preferred_element_type=F32)
    hval = _layer_norm(alpha * x_ref[...] + mix, g_ref[...], b_ref[...])
    h_ref[...] = hval

    logits = jnp.dot(hval.astype(BF16), wr_ref[...].astype(BF16), preferred_element_type=F32) + br_ref[...]

    lane = lax.broadcasted_iota(jnp.int32, (tm, N_EXPERTS), 1).astype(F32)
    work = logits
    chosen = jnp.zeros((tm, N_EXPERTS), F32)
    vals, idxs = [], []
    for _ in range(TOP_K):
        mk = jnp.max(work, axis=-1, keepdims=True)
        ik = jnp.min(jnp.where(work == mk, lane, float(N_EXPERTS)), axis=-1, keepdims=True)
        sel = lane == ik
        work = jnp.where(sel, -jnp.inf, work)
        chosen = jnp.where(sel, 1.0, chosen)
        vals.append(mk)
        idxs.append(ik)
    es = [jnp.exp(v - vals[0]) for v in vals]
    den = es[0] + es[1] + es[2] + es[3]

    ri = lax.broadcasted_iota(jnp.int32, (tm, tm), 0)
    ci = lax.broadcasted_iota(jnp.int32, (tm, tm), 1)
    before = jnp.where(ci < ri, 1.0, 0.0).astype(BF16)
    prefix = jnp.dot(before, chosen.astype(BF16), preferred_element_type=F32) + carry_ref[...]
    carry_ref[...] = carry_ref[...] + jnp.sum(chosen, axis=0, keepdims=True)
    cnt_ref[...] = carry_ref[...]

    olane = lax.broadcasted_iota(jnp.int32, (tm, LANES), 1)
    route = jnp.zeros((tm, LANES), F32)
    for k in range(TOP_K):
        rank_k = jnp.sum(jnp.where(lane == idxs[k], prefix, 0.0), axis=-1, keepdims=True)
        route = jnp.where(olane == k, idxs[k], route)
        route = jnp.where(olane == TOP_K + k, es[k] / den, route)
        route = jnp.where(olane == 2 * TOP_K + k, rank_k, route)
    route_ref[...] = route


def _outproj_router(att, ssd, x2d, w_out_bf16, ln_g, ln_b, w_router, b_router, alpha):
    m, d = x2d.shape
    tm = min(256, m)
    row = lambda i: (i, 0)
    const = lambda i: (0, 0)
    kern = functools.partial(_outproj_kernel, alpha=alpha)
    return pl.pallas_call(
        kern,
        grid=(m // tm,),
        in_specs=[pl.BlockSpec((tm, ATT_WIDTH), row), pl.BlockSpec((tm, SSD_WIDTH), row), pl.BlockSpec((tm, d), row),
                  pl.BlockSpec((ATT_WIDTH + SSD_WIDTH, d), const, pipeline_mode=pl.Buffered(1)),
                  pl.BlockSpec((1, d), const), pl.BlockSpec((1, d), const),
                  pl.BlockSpec((d, N_EXPERTS), const), pl.BlockSpec((1, N_EXPERTS), const)],
        out_specs=[pl.BlockSpec((tm, d), row), pl.BlockSpec((tm, LANES), row), pl.BlockSpec((1, N_EXPERTS), const)],
        out_shape=[jax.ShapeDtypeStruct((m, d), F32), jax.ShapeDtypeStruct((m, LANES), F32),
                   jax.ShapeDtypeStruct((1, N_EXPERTS), F32)],
        scratch_shapes=[pltpu.VMEM((1, N_EXPERTS), F32)],
        compiler_params=_params(("arbitrary",)),
    )(att, ssd, x2d, w_out_bf16, ln_g.reshape(1, d), ln_b.reshape(1, d), w_router, b_router.reshape(1, -1))


def _row_copy(src, dst, sem):
    return pltpu.make_async_copy(src, dst, sem)


def _scatter_kernel(cnt_ref, pst_ref, dest_ref, h_ref, *rest, first):
    xrows_ref, zero_ref, sem, zsem = rest[-4:]
    i = pl.program_id(0)
    tm = h_ref.shape[0]

    @pl.when((i == 0) & first)
    def _():
        zero_ref[...] = jnp.zeros_like(zero_ref)

        def per_expert(e, _):
            n = cnt_ref[e]
            base = pst_ref[e]
            end = (n + MOE_SUB - 1) // MOE_SUB * MOE_SUB

            def start(r, _):
                _row_copy(zero_ref.at[pl.ds(0, 1)], xrows_ref.at[pl.ds(base + r, 1)], zsem).start()
                return 0

            def wait(r, _):
                _row_copy(zero_ref.at[pl.ds(0, 1)], xrows_ref.at[pl.ds(base + r, 1)], zsem).wait()
                return 0

            lax.fori_loop(n, end, start, 0)
            lax.fori_loop(n, end, wait, 0)
            return 0

        lax.fori_loop(0, N_EXPERTS, per_expert, 0)

    def start(t, _):
        for k in range(TOP_K):
            d = dest_ref[0, 0, t * TOP_K + k]
            _row_copy(h_ref.at[pl.ds(t, 1)], xrows_ref.at[pl.ds(d, 1)], sem).start()
        return 0

    def wait(t, _):
        for k in range(TOP_K):
            d = dest_ref[0, 0, t * TOP_K + k]
            _row_copy(h_ref.at[pl.ds(t, 1)], xrows_ref.at[pl.ds(d, 1)], sem).wait()
        return 0

    lax.fori_loop(0, tm, start, 0)
    lax.fori_loop(0, tm, wait, 0)


def _moe_scatter(h2d, dest, counts, pstart, n_rows, x_rows=None):
    m, d = h2d.shape
    tm = min(256, m)
    nt = m // tm
    dest3 = dest.reshape(nt, 1, tm * TOP_K)
    first = x_rows is None
    in_specs = [pl.BlockSpec((1, 1, tm * TOP_K), lambda i, c, p: (i, 0, 0), memory_space=pltpu.SMEM),
                pl.BlockSpec((tm, d), lambda i, c, p: (i, 0))]
    args = [counts, pstart, dest3, h2d]
    aliases = {}
    if not first:
        in_specs.append(pl.BlockSpec(memory_space=pl.ANY))
        args.append(x_rows)
        aliases = {len(args) - 1: 0}
    grid_spec = pltpu.PrefetchScalarGridSpec(
        num_scalar_prefetch=2,
        grid=(nt,),
        in_specs=in_specs,
        out_specs=pl.BlockSpec(memory_space=pl.ANY),
        scratch_shapes=[pltpu.VMEM((8, d), F32), pltpu.SemaphoreType.DMA(()), pltpu.SemaphoreType.DMA(())],
    )
    return pl.pallas_call(
        functools.partial(_scatter_kernel, first=first),
        grid_spec=grid_spec,
        out_shape=jax.ShapeDtypeStruct((n_rows, d), F32),
        input_output_aliases=aliases,
        compiler_params=_params(("arbitrary",)),
    )(*args)


def _moe_mlp_kernel(ie_ref, ib_ref, iv_ref, x_ref, w1g_ref, w1l_ref, b1g_ref, b1l_ref, w2_ref, b2_ref, o_ref,
                    xb_ref, wg_ref, wl_ref, w2b_ref):
    i = pl.program_id(0)
    j = pl.program_id(1)
    nvalid = iv_ref[i]
    nsub = (nvalid + MOE_SUB - 1) // MOE_SUB
    d = x_ref.shape[1]

    def rows(s):
        return pl.ds(pl.multiple_of(s * MOE_SUB, MOE_SUB), MOE_SUB)

    @pl.when(nvalid > 0)
    def _():
        wg_ref[...] = w1g_ref[0].astype(BF16)
        wl_ref[...] = w1l_ref[0].astype(BF16)

        @pl.when(j == 0)
        def _():
            def init(s, _):
                xb_ref[rows(s), :] = x_ref[rows(s), :].astype(BF16)
                o_ref[rows(s), :] = jnp.broadcast_to(b2_ref[0], (MOE_SUB, d))
                return 0

            lax.fori_loop(0, nsub, init, 0)

        def hidden(s):
            xb = xb_ref[rows(s), :]
            return (jnp.dot(xb, wg_ref[...], preferred_element_type=F32),
                    jnp.dot(xb, wl_ref[...], preferred_element_type=F32))

        def finish(s, hid):
            glu = jnp.minimum(hid[0] + b1g_ref[0], SWIGLU_LIMIT)
            lin = jnp.clip(hid[1] + b1l_ref[0], -SWIGLU_LIMIT, SWIGLU_LIMIT)
            act = glu * _sigmoid(SWIGLU_ALPHA * glu) * (lin + 1.0)
            o_ref[rows(s), :] += jnp.dot(act.astype(BF16), w2b_ref[...], preferred_element_type=F32)

        def step(s, hid):
            nxt = hidden(s + 1)
            finish(s, hid)
            return nxt

        hid = hidden(0)
        w2b_ref[...] = w2_ref[0].astype(BF16)
        hid = lax.fori_loop(0, nsub - 1, step, hid)
        finish(nsub - 1, hid)


def _moe_mlp(x_rows, item_e, item_blk, item_valid, w1, b1, w2, b2):
    n_rows, d = x_rows.shape
    n_items = item_e.shape[0]
    d_ff = w2.shape[1]
    tf = MOE_F_TILE
    nf = d_ff // tf
    tmr = MOE_ROW_TILE

    def jj(i, j, iv):
        return jnp.where(iv[i] > 0, j, nf - 1)

    grid_spec = pltpu.PrefetchScalarGridSpec(
        num_scalar_prefetch=3,
        grid=(n_items, nf),
        in_specs=[pl.BlockSpec((tmr, d), lambda i, j, ie, ib, iv: (ib[i], 0)),
                  pl.BlockSpec((1, d, tf), lambda i, j, ie, ib, iv: (ie[i], 0, jj(i, j, iv))),
                  pl.BlockSpec((1, d, tf), lambda i, j, ie, ib, iv: (ie[i], 0, nf + jj(i, j, iv))),
                  pl.BlockSpec((1, 1, tf), lambda i, j, ie, ib, iv: (ie[i], 0, jj(i, j, iv))),
                  pl.BlockSpec((1, 1, tf), lambda i, j, ie, ib, iv: (ie[i], 0, nf + jj(i, j, iv))),
                  pl.BlockSpec((1, tf, d), lambda i, j, ie, ib, iv: (ie[i], jj(i, j, iv), 0)),
                  pl.BlockSpec((1, 1, d), lambda i, j, ie, ib, iv: (ie[i], 0, 0))],
        out_specs=pl.BlockSpec((tmr, d), lambda i, j, ie, ib, iv: (ib[i], 0)),
        scratch_shapes=[pltpu.VMEM((tmr, d), BF16), pltpu.VMEM((d, tf), BF16), pltpu.VMEM((d, tf), BF16),
                        pltpu.VMEM((tf, d), BF16)],
    )
    return pl.pallas_call(
        _moe_mlp_kernel,
        grid_spec=grid_spec,
        out_shape=jax.ShapeDtypeStruct((n_rows, d), F32),
        compiler_params=_params(("arbitrary", "arbitrary")),
    )(item_e, item_blk, item_valid, x_rows, w1, w1, b1.reshape(N_EXPERTS, 1, -1), b1.reshape(N_EXPERTS, 1, -1),
      w2, b2.reshape(N_EXPERTS, 1, -1))


def _combine_kernel(dest_ref, h_ref, route_ref, yrows_ref, g_ref, b_ref, o_ref, buf_ref, sem, *, alpha):
    tm = h_ref.shape[0]

    def start(t, _):
        for k in range(TOP_K):
            d = dest_ref[0, 0, t * TOP_K + k]
            _row_copy(yrows_ref.at[pl.ds(d, 1)], buf_ref.at[k, pl.ds(t, 1)], sem).start()
        return 0

    def wait(t, _):
        for k in range(TOP_K):
            d = dest_ref[0, 0, t * TOP_K + k]
            _row_copy(yrows_ref.at[pl.ds(d, 1)], buf_ref.at[k, pl.ds(t, 1)], sem).wait()
        return 0

    lax.fori_loop(0, tm, start, 0)
    lax.fori_loop(0, tm, wait, 0)
    route = route_ref[...]
    acc = route[:, TOP_K:TOP_K + 1] * buf_ref[0]
    for k in range(1, TOP_K):
        acc = acc + route[:, TOP_K + k:TOP_K + k + 1] * buf_ref[k]
    o_ref[...] = _layer_norm(alpha * h_ref[...] + acc, g_ref[...], b_ref[...])


def _moe_combine(h2d, route, dest, y_rows, ln_g, ln_b, alpha):
    m, d = h2d.shape
    tm = min(128, m)
    nt = m // tm
    dest3 = dest.reshape(nt, 1, tm * TOP_K)
    kern = functools.partial(_combine_kernel, alpha=alpha)
    return pl.pallas_call(
        kern,
        grid=(nt,),
        in_specs=[pl.BlockSpec((1, 1, tm * TOP_K), lambda i: (i, 0, 0), memory_space=pltpu.SMEM),
                  pl.BlockSpec((tm, d), lambda i: (i, 0)),
                  pl.BlockSpec((tm, LANES), lambda i: (i, 0)),
                  pl.BlockSpec(memory_space=pl.ANY),
                  pl.BlockSpec((1, d), lambda i: (0, 0)), pl.BlockSpec((1, d), lambda i: (0, 0))],
        out_specs=pl.BlockSpec((tm, d), lambda i: (i, 0)),
        out_shape=jax.ShapeDtypeStruct((m, d), F32),
        scratch_shapes=[pltpu.VMEM((TOP_K, tm, d), F32), pltpu.SemaphoreType.DMA(())],
        compiler_params=_params(("arbitrary",)),
    )(dest3, h2d, route, y_rows, ln_g.reshape(1, d), ln_b.reshape(1, d))


def _moe_ln2(groups, w1, b1, w2, b2, ln_g, ln_b, alpha):
    tmr = MOE_ROW_TILE
    group_counts = [c.reshape(-1).astype(jnp.int32) for _, _, c in groups]
    counts = sum(group_counts)
    tiles = (counts + tmr - 1) // tmr
    tile_end = jnp.cumsum(tiles)
    tile_start = tile_end - tiles
    pstart = (tile_start * tmr).astype(jnp.int32)
    n_tok = sum(h.shape[0] for h, _, _ in groups)
    n_items = -(-(n_tok * TOP_K) // tmr) + N_EXPERTS
    n_rows = n_items * tmr
    it = jnp.arange(n_items, dtype=jnp.int32)
    total = tile_end[-1]
    it_c = jnp.minimum(it, total - 1)
    item_e = jnp.minimum(jnp.searchsorted(tile_end, it_c, side='right'), N_EXPERTS - 1).astype(jnp.int32)
    item_r = it_c - tile_start[item_e]
    item_blk = (tile_start[item_e] + item_r).astype(jnp.int32)
    item_valid = jnp.where(it < total, jnp.clip(counts[item_e] - item_r * tmr, 0, tmr), 0).astype(jnp.int32)

    dests = []
    earlier = jnp.zeros_like(counts)
    x_rows = None
    for (h2d, route, _), gc in zip(groups, group_counts):
        idx = route[:, 0:TOP_K].astype(jnp.int32)
        rank = route[:, 2 * TOP_K:3 * TOP_K].astype(jnp.int32)
        dest = (pstart[idx] + earlier[idx] + rank).reshape(-1)
        dests.append(dest)
        earlier = earlier + gc
        x_rows = _moe_scatter(h2d, dest, counts, pstart, n_rows, x_rows)
    y_rows = _moe_mlp(x_rows, item_e, item_blk, item_valid, w1, b1, w2, b2)
    return [_moe_combine(h2d, route, dest, y_rows, ln_g, ln_b, alpha)
            for (h2d, route, _), dest in zip(groups, dests)]


def kernel(x_prompt, x_sample, cache_k, cache_v, page_table, state_ssm, state_conv, w_in, w_out, lambda_qk,
           attn_subln_w, conv_w, conv_b, dt_bias, a_log, d_skip, ssd_norm_w, ln1_g, ln1_b, w_router, b_router,
           w_mlp1, b_mlp1, w_mlp2, b_mlp2, ln2_g, ln2_b):
    bp, sp, d = x_prompt.shape
    bd, sd, _ = x_sample.shape
    depth = w_in.shape[0]
    assert depth == 1 and sd == 1, "kernel supports the single-layer, single-token-decode configuration"
    alpha = (2.0 * depth) ** 0.25
    slopes = 2.0 ** (-8.0 * jnp.arange(1, ATT_HEADS + 1, dtype=F32) / ATT_HEADS)
    l = 0
    lam_init = 0.8 - 0.6 * math.exp(-0.3 * l)

    w_in_b = _cast_bf16(w_in[l])
    w_out_b = _cast_bf16(w_out[l])
    moe_w = (w_mlp1[l], b_mlp1[l], w_mlp2[l], b_mlp2[l], ln2_g[l], ln2_b[l])
    ssd_w = (conv_w[l], conv_b[l], dt_bias[l], a_log[l], d_skip[l], ssd_norm_w[l])

    xp = x_prompt.reshape(bp * sp, d)
    q, k, v, z, xbc, dt = _inproj(xp, w_in_b, BF16)
    att = _attn_prompt(q.reshape(bp, sp, -1), k.reshape(bp, sp, -1), v.reshape(bp, sp, -1), lambda_qk[l],
                       attn_subln_w[l], slopes, lam_init)
    ssm_zero = jnp.zeros((bp, SSD_HEADS, SSD_HEADDIM, D_STATE), F32)
    conv_zero = jnp.zeros((bp, CONV_W - 1, CONV_DIM), F32)
    ssd, ssm_p, conv_p = _ssd_prompt(xbc.reshape(bp, sp, -1), dt.reshape(bp, sp, -1), z.reshape(bp, sp, -1),
                                     ssm_zero, conv_zero, *ssd_w)
    h, route, counts = _outproj_router(att.reshape(bp * sp, -1), ssd.reshape(bp * sp, -1), xp, w_out_b,
                                       ln1_g[l], ln1_b[l], w_router[l], b_router[l], alpha)

    xs = x_sample.reshape(bd, d)
    qs, ks, vs, zs, xbcs, dts = _inproj(xs, w_in_b, F32)
    att_s = _attn_decode(qs, ks, vs, cache_k, cache_v, page_table, lambda_qk[l], attn_subln_w[l], slopes,
                         lam_init)
    ssd_s, ssm_s, conv_s = _ssd_sample(xbcs, dts, zs, state_ssm[l], state_conv[l], *ssd_w)
    hs, route_s, counts_s = _outproj_router(att_s, ssd_s, xs, w_out_b, ln1_g[l], ln1_b[l], w_router[l],
                                            b_router[l], alpha)
    y_prompt, y_sample = _moe_ln2([(h, route, counts), (hs, route_s, counts_s)], *moe_w, alpha)
    y_prompt = y_prompt.reshape(bp, sp, d)
    y_sample = y_sample.reshape(bd, sd, d)

    hshape = (ATT_HEADS, 2 * ATT_DH)
    return (y_prompt, y_sample,
            k.reshape(1, bp, sp, *hshape), v.reshape(1, bp, sp, *hshape), ssm_p[None], conv_p[None],
            ks.reshape(1, bd, sd, *hshape), vs.reshape(1, bd, sd, *hshape), ssm_s[None], conv_s[None])
```

```python
import functools
import math

import jax
import jax.numpy as jnp
from jax import lax
from jax.experimental import pallas as pl
from jax.experimental.pallas import tpu as pltpu

F32 = jnp.float32
BF16 = jnp.bfloat16

ATT_DH = 64
ATT_HEADS = 8
ATT_WIDTH = ATT_HEADS * 2 * ATT_DH
ATT_SCALE = ATT_DH ** -0.5
SSD_HEADDIM = 64
SSD_HEADS = 16
SSD_WIDTH = SSD_HEADS * SSD_HEADDIM
SSD_GROUPS = 2
D_STATE = 128
CONV_W = 4
CONV_DIM = SSD_WIDTH + 2 * SSD_GROUPS * D_STATE
SSD_CHUNK = 128
N_EXPERTS = 32
TOP_K = 4
SWIGLU_ALPHA = 1.702
SWIGLU_LIMIT = 7.0
LN_EPS = 1e-5
RMS_EPS = 1e-5
PAGE_SIZE = 128

V7X_VMEM_LIMIT_BYTES = 56 * 1024 * 1024
LANES = 128

MOE_ROW_TILE = 1024
MOE_SUB = 256
MOE_F_TILE = 256


def _params(sem, vmem=V7X_VMEM_LIMIT_BYTES):
    return pltpu.CompilerParams(dimension_semantics=sem, vmem_limit_bytes=vmem)


def _sigmoid(x):
    return 1.0 / (1.0 + jnp.exp(-x))


def _silu(x):
    return x * _sigmoid(x)


def _softplus(x):
    return jnp.maximum(x, 0.0) + jnp.log1p(jnp.exp(-jnp.abs(x)))


def _layer_norm(x, g, b):
    mu = jnp.mean(x, axis=-1, keepdims=True)
    xc = x - mu
    var = jnp.mean(xc * xc, axis=-1, keepdims=True)
    return xc * lax.rsqrt(var + LN_EPS) * g + b


def _split3(x):
    a = x.astype(BF16)
    r = x - a.astype(F32)
    b = r.astype(BF16)
    c = (r - b.astype(F32)).astype(BF16)
    return a, b, c


def _lam(lq):
    s01 = jnp.sum(lq[0:1, :] * lq[1:2, :], axis=-1, keepdims=True)
    s23 = jnp.sum(lq[2:3, :] * lq[3:4, :], axis=-1, keepdims=True)
    return jnp.exp(s01) - jnp.exp(s23)


def _cast_kernel(w_ref, o_ref):
    o_ref[...] = w_ref[...].astype(BF16)


def _cast_bf16(w, row_tile=256):
    r, c = w.shape
    rt = min(row_tile, r)
    return pl.pallas_call(
        _cast_kernel,
        grid=(r // rt,),
        in_specs=[pl.BlockSpec((rt, c), lambda i: (i, 0))],
        out_specs=pl.BlockSpec((rt, c), lambda i: (i, 0)),
        out_shape=jax.ShapeDtypeStruct((r, c), BF16),
        compiler_params=_params(("arbitrary",)),
    )(w)


def _inproj_kernel(x_ref, w_ref, q_ref, k_ref, v_ref, z_ref, xbc_ref, dt_ref):
    xb = x_ref[...].astype(BF16)

    def mm(c0, c1):
        return jnp.dot(xb, w_ref[:, c0:c1], preferred_element_type=F32)

    a = ATT_WIDTH
    q_ref[...] = (mm(0, a) * ATT_SCALE).astype(q_ref.dtype)
    k_ref[...] = mm(a, 2 * a)
    v_ref[...] = mm(2 * a, 3 * a)
    z_ref[...] = mm(3 * a, 3 * a + SSD_WIDTH)
    c0 = 3 * a + SSD_WIDTH
    xbc_ref[...] = mm(c0, c0 + CONV_DIM)
    dt_ref[...] = mm(c0 + CONV_DIM, c0 + CONV_DIM + SSD_HEADS)


def _inproj(x2d, w_in_bf16, q_dtype):
    m, d = x2d.shape
    ncol = w_in_bf16.shape[1]
    tm = min(256, m)
    row = lambda i: (i, 0)
    widths = (ATT_WIDTH, ATT_WIDTH, ATT_WIDTH, SSD_WIDTH, CONV_DIM, SSD_HEADS)
    dtypes = (q_dtype, F32, F32, F32, F32, F32)
    return pl.pallas_call(
        _inproj_kernel,
        grid=(m // tm,),
        in_specs=[pl.BlockSpec((tm, d), row),
                  pl.BlockSpec((d, ncol), lambda i: (0, 0), pipeline_mode=pl.Buffered(1))],
        out_specs=[pl.BlockSpec((tm, w), row) for w in widths],
        out_shape=[jax.ShapeDtypeStruct((m, w), dt) for w, dt in zip(widths, dtypes)],
        compiler_params=_params(("arbitrary",)),
    )(x2d, w_in_bf16)


def _attn_prompt_kernel(slopes_ref, qt_ref, k_ref, v_ref, lq_ref, w_ref, o_ref, kb_ref, vt_ref, *, tq, lam_init):
    h = pl.program_id(1)
    qi = pl.program_id(2)
    hd = 2 * ATT_DH
    ones_rows = vt_ref.shape[0] - hd

    @pl.when(qi == 0)
    def _():
        kb_ref[...] = k_ref[0].astype(BF16)
        vt_ref[0:hd, :] = v_ref[0].T.astype(BF16)
        vt_ref[hd:hd + ones_rows, :] = jnp.ones((ones_rows, vt_ref.shape[1]), BF16)

    slope = slopes_ref[h]
    qt = qt_ref[0]
    drow = lax.broadcasted_iota(jnp.int32, (hd, tq), 0)
    zero = jnp.zeros_like(qt)
    qts = (jnp.where(drow < ATT_DH, qt, zero), jnp.where(drow >= ATT_DH, qt, zero))
    kr = lax.broadcasted_iota(jnp.int32, (tq, tq), 0)
    qc = lax.broadcasted_iota(jnp.int32, (tq, tq), 1)
    base = (qc - kr).astype(F32) * slope
    future = kr > qc

    def scores(j):
        kb = kb_ref[pl.ds(pl.multiple_of(j * tq, tq), tq), :]
        return tuple(jnp.dot(kb, qts[mi], preferred_element_type=F32) for mi in range(2))

    def block(j, raw, stats, diag):
        vta = vt_ref[:, pl.ds(pl.multiple_of(j * tq, tq), tq)]
        off = jnp.full((1, 1), (qi - j) * tq, jnp.int32).astype(F32) * slope
        bias = base + off
        out = []
        for mi in range(2):
            m, l, a = stats[mi]
            s = raw[mi] - bias
            if diag:
                s = jnp.where(future, -jnp.inf, s)
            mn = jnp.maximum(m, jnp.max(s, axis=0, keepdims=True))
            p = jnp.exp(s - mn)
            al = jnp.exp(m - mn)
            pv = jnp.dot(vta, p.astype(BF16), preferred_element_type=F32)
            l = al * l + pv[hd:hd + 1, :]
            a = al * a + pv[0:hd, :]
            out.append((mn, l, a))
        return tuple(out)

    def step(j, carry):
        raw, stats = carry
        nxt = scores(j + 1)
        return nxt, block(j, raw, stats, False)

    init1 = (jnp.full((1, tq), -1e30, F32), jnp.zeros((1, tq), F32), jnp.zeros((hd, tq), F32))
    raw, stats = lax.fori_loop(0, qi, step, (scores(0), (init1, init1)))
    (_, l0, a0), (_, l1, a1) = block(qi, raw, stats, True)
    lam = _lam(lq_ref[...]) + lam_init
    o = a0 / l0 - lam * (a1 / l1)
    o = o * lax.rsqrt(jnp.mean(o * o, axis=0, keepdims=True) + RMS_EPS) * w_ref[...] * (1.0 - lam_init)
    o_ref[0] = o.T.astype(BF16)


def _attn_prompt(q, k, v, lambda_qk, subln_w, slopes, lam_init):
    b, t, _ = k.shape
    tq = min(256, t)
    hd = 2 * ATT_DH
    ones_rows = 16
    qt = jnp.swapaxes(q, 1, 2)
    kern = functools.partial(_attn_prompt_kernel, tq=tq, lam_init=lam_init)
    return pl.pallas_call(
        kern,
        grid=(b, ATT_HEADS, t // tq),
        in_specs=[pl.BlockSpec(memory_space=pltpu.SMEM),
                  pl.BlockSpec((1, hd, tq), lambda bi, h, qi: (bi, h, qi)),
                  pl.BlockSpec((1, t, hd), lambda bi, h, qi: (bi, 0, h)),
                  pl.BlockSpec((1, t, hd), lambda bi, h, qi: (bi, 0, h)),
                  pl.BlockSpec((4, ATT_DH), lambda bi, h, qi: (0, 0)),
                  pl.BlockSpec((hd, 1), lambda bi, h, qi: (0, 0))],
        out_specs=pl.BlockSpec((1, tq, hd), lambda bi, h, qi: (bi, qi, h)),
        out_shape=jax.ShapeDtypeStruct((b, t, ATT_WIDTH), BF16),
        scratch_shapes=[pltpu.VMEM((t, hd), BF16), pltpu.VMEM((hd + ones_rows, t), BF16)],
        compiler_params=_params(("arbitrary", "arbitrary", "arbitrary")),
    )(slopes, qt, k, v, lambda_qk, subln_w.reshape(hd, 1))


def _attn_decode_kernel(pt_ref, q_ref, kn_ref, vn_ref, slope_ref, lq_ref, w_ref, *rest, pps, past_len, lam_init):
    kp_refs, vp_refs = rest[:pps], rest[pps:2 * pps]
    o_ref, qt_ref, s_ref, a_ref, m_ref, snew_ref, anew_ref, acc_ref = rest[2 * pps:]
    ph = pl.program_id(1)
    p = pl.program_id(2)
    n_steps = pl.num_programs(2)
    nh = ATT_HEADS
    nrow = 2 * nh
    hd = 2 * ATT_DH
    plane = PAGE_SIZE * nh
    n_pages = past_len // PAGE_SIZE
    per_vreg = LANES // nh

    def page_lanes(page):
        return pl.ds(pl.multiple_of(page * plane, plane), plane)

    @pl.when((ph == 0) & (p == 0))
    def _():
        q8 = q_ref[0]
        lane = lax.broadcasted_iota(jnp.int32, (nh, hd), 1)
        qt = jnp.concatenate([jnp.where(lane < ATT_DH, q8, 0.0), jnp.where(lane >= ATT_DH, q8, 0.0)], axis=0)
        qt_ref[...] = qt.astype(BF16)
        kn = jnp.concatenate([kn_ref[0], kn_ref[0]], axis=0)
        s_new = jnp.sum(qt * kn, axis=-1, keepdims=True)
        snew_ref[...] = s_new
        m_ref[...] = s_new

    @pl.when(ph == 0)
    def _():
        lane = lax.broadcasted_iota(jnp.int32, (nrow, plane), 1)
        row = lax.broadcasted_iota(jnp.int32, (nrow, plane), 0)
        own_head = (lane % nh) == (row % nh)
        for i in range(pps):
            page = p * pps + i
            kflat = kp_refs[i][0].reshape(plane, hd).astype(BF16)
            s = lax.dot_general(qt_ref[...], kflat, (((1,), (1,)), ((), ())), preferred_element_type=F32)
            dist = (past_len - page * PAGE_SIZE - lane // nh).astype(F32)
            s = jnp.where(own_head, s - slope_ref[...] * dist, -jnp.inf)
            s_ref[:, page_lanes(page)] = s
            m_ref[...] = jnp.maximum(m_ref[...], jnp.max(s, axis=-1, keepdims=True))

    @pl.when((ph == 1) & (p == 0))
    def _():
        m = m_ref[...]
        e_new = jnp.exp(snew_ref[...] - m)

        def expsum(g, part):
            for u in range(pps):
                e = jnp.exp(s_ref[:, page_lanes(g * pps + u)] - m)
                s_ref[:, page_lanes(g * pps + u)] = e
                for c in range(plane // LANES):
                    part = part + e[:, c * LANES:(c + 1) * LANES]
            return part

        part = lax.fori_loop(0, n_pages // pps, expsum, jnp.zeros((nrow, LANES), F32))
        den = jnp.sum(part, axis=-1, keepdims=True) + e_new
        inv = 1.0 / den
        lam = _lam(lq_ref[...]) + lam_init

        def combine(g, _):
            for u in range(pps):
                pn = s_ref[:, page_lanes(g * pps + u)] * inv
                a_ref[:, page_lanes(g * pps + u)] = pn[0:nh] - lam * pn[nh:nrow]
            return 0

        lax.fori_loop(0, n_pages // pps, combine, 0)
        pn_new = e_new * inv
        anew_ref[...] = pn_new[0:nh] - lam * pn_new[nh:nrow]
        acc_ref[...] = jnp.zeros_like(acc_ref)

    @pl.when(ph == 1)
    def _():
        lane = lax.broadcasted_iota(jnp.int32, (nh, LANES), 1)
        n_acc = acc_ref.shape[0]
        for i in range(pps):
            page = p * pps + i

            accs = [acc_ref[k] for k in range(n_acc)]
            for g in range(plane // LANES):
                av = a_ref[:, pl.ds(pl.multiple_of(page * plane + g * LANES, LANES), LANES)]
                for jj in range(per_vreg):
                    sel = (lane >= jj * nh) & (lane < (jj + 1) * nh)
                    wcol = jnp.sum(jnp.where(sel, av, 0.0), axis=-1, keepdims=True)
                    accs[jj % n_acc] = accs[jj % n_acc] + wcol * vp_refs[i][0, g * per_vreg + jj]
            for k in range(n_acc):
                acc_ref[k] = accs[k]

    @pl.when((ph == 1) & (p == n_steps - 1))
    def _():
        o = anew_ref[...] * vn_ref[0]
        for k in range(acc_ref.shape[0]):
            o = o + acc_ref[k]
        o = o * lax.rsqrt(jnp.mean(o * o, axis=-1, keepdims=True) + RMS_EPS) * w_ref[...] * (1.0 - lam_init)
        o_ref[0] = o.astype(BF16)


def _attn_decode(q, k_new, v_new, cache_k, cache_v, page_table, lambda_qk, subln_w, slopes, lam_init):
    bd = q.shape[0]
    n_pages = page_table.shape[1]
    nh = ATT_HEADS
    nrow = 2 * nh
    hd = 2 * ATT_DH
    pps = max(c for c in (1, 2, 4, 8) if n_pages % c == 0)
    n_steps = n_pages // pps
    kp = cache_k.reshape(-1, PAGE_SIZE, nh, hd)
    vp = cache_v.reshape(-1, PAGE_SIZE, nh, hd)
    slope_rows = jnp.tile(slopes, 2).reshape(nrow, 1)
    past_len = n_pages * PAGE_SIZE
    kern = functools.partial(_attn_decode_kernel, pps=pps, past_len=past_len, lam_init=lam_init)
    head3 = lambda b, ph, p, pt: (b, 0, 0)
    const2 = lambda b, ph, p, pt: (0, 0)

    def kpage(i):
        return lambda b, ph, p, pt: (pt[b * n_pages + jnp.where(ph == 0, p, n_steps - 1) * pps + i], 0, 0, 0)

    def vpage(i):
        return lambda b, ph, p, pt: (pt[b * n_pages + jnp.where(ph == 0, 0, p) * pps + i], 0, 0, 0)

    page_block = (1, PAGE_SIZE, nh, hd)
    n_acc = 4
    grid_spec = pltpu.PrefetchScalarGridSpec(
        num_scalar_prefetch=1,
        grid=(bd, 2, n_steps),
        in_specs=[pl.BlockSpec((1, nh, hd), head3),
                  pl.BlockSpec((1, nh, hd), head3),
                  pl.BlockSpec((1, nh, hd), head3),
                  pl.BlockSpec((nrow, 1), const2),
                  pl.BlockSpec((4, ATT_DH), const2),
                  pl.BlockSpec((1, hd), const2)]
                 + [pl.BlockSpec(page_block, kpage(i)) for i in range(pps)]
                 + [pl.BlockSpec(page_block, vpage(i)) for i in range(pps)],
        out_specs=pl.BlockSpec((1, nh, hd), head3),
        scratch_shapes=[pltpu.VMEM((nrow, hd), BF16), pltpu.VMEM((nrow, past_len * nh), F32),
                        pltpu.VMEM((nh, past_len * nh), F32), pltpu.VMEM((nrow, 1), F32),
                        pltpu.VMEM((nrow, 1), F32), pltpu.VMEM((nh, 1), F32),
                        pltpu.VMEM((n_acc, nh, hd), F32)],
    )
    out = pl.pallas_call(
        kern,
        grid_spec=grid_spec,
        out_shape=jax.ShapeDtypeStruct((bd, nh, hd), BF16),
        compiler_params=_params(("arbitrary", "arbitrary", "arbitrary")),
    )(page_table.reshape(-1), q.reshape(bd, nh, hd), k_new.reshape(bd, nh, hd), v_new.reshape(bd, nh, hd),
      slope_rows, lambda_qk, subln_w.reshape(1, hd), *([kp] * pps), *([vp] * pps))
    return out.reshape(bd, ATT_WIDTH)


def _gated_group_norm(y, z, w):
    yg = y * _silu(z)
    gw = SSD_WIDTH // SSD_GROUPS
    parts = []
    for g in range(SSD_GROUPS):
        v = yg[:, g * gw:(g + 1) * gw]
        parts.append(v * lax.rsqrt(jnp.mean(v * v, axis=-1, keepdims=True) + RMS_EPS) * w[:, g * gw:(g + 1) * gw])
    return parts


def _ssd_prompt_kernel(xbc_ref, dt_ref, dtt_ref, z_ref, cw_ref, cb_ref, dtb_ref, dtbt_ref, al_ref, alt_ref,
                       dsk_ref, nw_ref, h0_ref, c0_ref, y_ref, st_ref, cv_ref, xpad_ref, ysc_ref, xdd_ref):
    c = pl.program_id(1)
    nc = pl.num_programs(1)
    L = SSD_CHUNK
    P = SSD_HEADDIM
    hpg = SSD_HEADS // SSD_GROUPS
    halo = 8

    @pl.when(c == 0)
    def _():
        st_ref[...] = h0_ref[...]
        xpad_ref[0:halo, :] = jnp.zeros((halo, CONV_DIM), F32)
        xpad_ref[halo - (CONV_W - 1):halo, :] = c0_ref[0]

    xc = xbc_ref[0]
    xpad_ref[halo:halo + L, :] = xc
    conv = cb_ref[...]
    for i in range(CONV_W - 1):
        sh = CONV_W - 1 - i
        conv = conv + xpad_ref[halo - sh:halo - sh + L, :] * cw_ref[i:i + 1, :]
    conv = conv + xc * cw_ref[CONV_W - 1:CONV_W, :]
    tail = xc[L - (CONV_W - 1):L, :]
    xpad_ref[halo - (CONV_W - 1):halo, :] = tail

    @pl.when(c == nc - 1)
    def _():
        cv_ref[0] = tail

    act = _silu(conv)
    xs = act[:, :SSD_WIDTH]
    bmat = [act[:, SSD_WIDTH + g * D_STATE:SSD_WIDTH + (g + 1) * D_STATE].astype(BF16) for g in range(SSD_GROUPS)]
    c_off = SSD_WIDTH + SSD_GROUPS * D_STATE
    cmat = [act[:, c_off + g * D_STATE:c_off + (g + 1) * D_STATE].astype(BF16) for g in range(SSD_GROUPS)]

    dtp = _softplus(dt_ref[0] + dtb_ref[...])
    dtpt = _softplus(dtt_ref[0] + dtbt_ref[...])
    da = dtp * (-jnp.exp(al_ref[...]))
    dat = dtpt * (-jnp.exp(alt_ref[...]))
    ri = lax.broadcasted_iota(jnp.int32, (L, L), 0)
    ci = lax.broadcasted_iota(jnp.int32, (L, L), 1)
    causal = ri >= ci
    tri = jnp.where(causal, 1.0, 0.0).astype(BF16)
    trit = jnp.where(ci >= ri, 1.0, 0.0).astype(BF16)
    cs = sum(jnp.dot(tri, part, preferred_element_type=F32) for part in _split3(da))
    cst = sum(jnp.dot(part, trit, preferred_element_type=F32) for part in _split3(dat))

    cb = [lax.dot_general(cmat[g], bmat[g], (((1,), (1,)), ((), ())), preferred_element_type=F32)
          for g in range(SSD_GROUPS)]
    dsk = dsk_ref[...]

    for h in range(SSD_HEADS):
        g = h // hpg
        cs_col = cs[:, h:h + 1]
        diff = cs_col - cst[h:h + 1, :]
        lmat = jnp.exp(jnp.where(causal, diff, -jnp.inf))
        mmat = (cb[g] * lmat).astype(BF16)
        xs_h = xs[:, h * P:(h + 1) * P]
        xd_h = xs_h * dtp[:, h:h + 1]
        y = jnp.dot(mmat, xd_h.astype(BF16), preferred_element_type=F32)
        st = st_ref[0, h]
        yoff = lax.dot_general(cmat[g], st.astype(BF16), (((1,), (1,)), ((), ())), preferred_element_type=F32)
        y = y + jnp.exp(cs_col) * yoff + dsk[:, h:h + 1] * xs_h
        ysc_ref[:, h * P:(h + 1) * P] = y
        cs_last = cs[L - 1:L, h:h + 1]
        xdd_ref[:, h * P:(h + 1) * P] = xd_h * jnp.exp(cs_last - cs_col)

    xddt = xdd_ref[...].T
    for h in range(SSD_HEADS):
        g = h // hpg
        new = jnp.dot(xddt[h * P:(h + 1) * P, :].astype(BF16), bmat[g], preferred_element_type=F32)
        cs_last = cs[L - 1:L, h:h + 1]
        st_ref[0, h] = jnp.exp(cs_last) * st_ref[0, h] + new

    parts = _gated_group_norm(ysc_ref[...], z_ref[0], nw_ref[...])
    gw = SSD_WIDTH // SSD_GROUPS
    for g in range(SSD_GROUPS):
        y_ref[0, :, g * gw:(g + 1) * gw] = parts[g].astype(BF16)


def _ssd_prompt(xbc, dt, z, h0, conv0, conv_w, conv_b, dt_bias, a_log, d_skip, norm_w):
    b, t, _ = xbc.shape
    L = SSD_CHUNK
    nc = t // L
    dtt = jnp.swapaxes(dt, 1, 2)
    seq = lambda bi, ci: (bi, ci, 0)
    const2 = lambda bi, ci: (0, 0)
    full2 = lambda shp: pl.BlockSpec(shp, const2)
    return pl.pallas_call(
        _ssd_prompt_kernel,
        grid=(b, nc),
        in_specs=[pl.BlockSpec((1, L, CONV_DIM), seq),
                  pl.BlockSpec((1, L, SSD_HEADS), seq),
                  pl.BlockSpec((1, SSD_HEADS, L), lambda bi, ci: (bi, 0, ci)),
                  pl.BlockSpec((1, L, SSD_WIDTH), seq),
                  full2((CONV_W, CONV_DIM)), full2((1, CONV_DIM)),
                  full2((1, SSD_HEADS)), full2((SSD_HEADS, 1)),
                  full2((1, SSD_HEADS)), full2((SSD_HEADS, 1)),
                  full2((1, SSD_HEADS)), full2((1, SSD_WIDTH)),
                  pl.BlockSpec((1, SSD_HEADS, SSD_HEADDIM, D_STATE), lambda bi, ci: (bi, 0, 0, 0)),
                  pl.BlockSpec((1, CONV_W - 1, CONV_DIM), lambda bi, ci: (bi, 0, 0))],
        out_specs=[pl.BlockSpec((1, L, SSD_WIDTH), seq),
                   pl.BlockSpec((1, SSD_HEADS, SSD_HEADDIM, D_STATE), lambda bi, ci: (bi, 0, 0, 0)),
                   pl.BlockSpec((1, CONV_W - 1, CONV_DIM), lambda bi, ci: (bi, 0, 0))],
        out_shape=[jax.ShapeDtypeStruct((b, t, SSD_WIDTH), BF16),
                   jax.ShapeDtypeStruct((b, SSD_HEADS, SSD_HEADDIM, D_STATE), F32),
                   jax.ShapeDtypeStruct((b, CONV_W - 1, CONV_DIM), F32)],
        scratch_shapes=[pltpu.VMEM((8 + L, CONV_DIM), F32), pltpu.VMEM((L, SSD_WIDTH), F32),
                        pltpu.VMEM((L, SSD_WIDTH), F32)],
        compiler_params=_params(("arbitrary", "arbitrary")),
    )(xbc, dt, dtt, z, conv_w, conv_b.reshape(1, -1), dt_bias.reshape(1, -1), dt_bias.reshape(-1, 1),
      a_log.reshape(1, -1), a_log.reshape(-1, 1), d_skip.reshape(1, -1), norm_w.reshape(1, -1), h0, conv0)


def _bf16_round(v):
    return v.astype(BF16).astype(F32)


def _ssd_sample_kernel(xbc_ref, ci_ref, dt_ref, z_ref, cw_ref, cb_ref, dtb_ref, al_ref, dsk_ref, nw_ref, h0_ref,
                       xbc8_ref, ci8_ref, dt8_ref, cw8_ref, cb8_ref, dtb8_ref, y_ref, st_ref, cv_ref, ysc_ref):
    P = SSD_HEADDIM
    hpg = SSD_HEADS // SSD_GROUPS
    xrow = xbc_ref[0]
    hist = ci_ref[0]
    conv = cb_ref[...]
    for i in range(CONV_W - 1):
        conv = conv + hist[i:i + 1, :] * cw_ref[i:i + 1, :]
    conv = conv + xrow * cw_ref[CONV_W - 1:CONV_W, :]
    cv_ref[0, 0:CONV_W - 2, :] = hist[1:CONV_W - 1, :]
    cv_ref[0, CONV_W - 2:CONV_W - 1, :] = xrow
    act = _silu(conv)
    xs = act[:, :SSD_WIDTH]
    c_off = SSD_WIDTH + SSD_GROUPS * D_STATE
    dtp = _softplus(dt_ref[0] + dtb_ref[...])
    decay = jnp.exp(dtp * (-jnp.exp(al_ref[...])))
    dsk = dsk_ref[...]

    nx = SSD_WIDTH // LANES
    conv8 = cb8_ref[0:nx, :]
    for i in range(CONV_W - 1):
        conv8 = conv8 + ci8_ref[0, i, 0:nx, :] * cw8_ref[i, 0:nx, :]
    conv8 = conv8 + xbc8_ref[0, 0:nx, :] * cw8_ref[CONV_W - 1, 0:nx, :]
    xd8 = _silu(conv8) * _softplus(dt8_ref[0] + dtb8_ref[...])
    xdt = xd8.T

    brows = [act[:, SSD_WIDTH + g * D_STATE:SSD_WIDTH + (g + 1) * D_STATE] for g in range(SSD_GROUPS)]
    crows = [act[:, c_off + g * D_STATE:c_off + (g + 1) * D_STATE] for g in range(SSD_GROUPS)]
    cbs = [jnp.sum(_bf16_round(brows[g]) * _bf16_round(crows[g]), axis=-1, keepdims=True) for g in range(SSD_GROUPS)]
    c8s = [jnp.broadcast_to(crows[g], (8, D_STATE)).astype(BF16) for g in range(SSD_GROUPS)]
    hpr = LANES // P
    for h in range(SSD_HEADS):
        g = h // hpg
        xs_h = xs[:, h * P:(h + 1) * P]
        xd_h = xs_h * dtp[:, h:h + 1]
        xcol = xdt[(h % hpr) * P:(h % hpr + 1) * P, h // hpr:h // hpr + 1]
        dec = decay[:, h:h + 1]
        h0q = h0_ref[0, h].astype(BF16)
        st_ref[0, h] = _bf16_round(dec) * h0q.astype(F32) + _bf16_round(xcol * brows[g])
        yoff = lax.dot_general(c8s[g], h0q, (((1,), (1,)), ((), ())), preferred_element_type=F32)[0:1, :]
        ysc_ref[:, h * P:(h + 1) * P] = (cbs[g] * xd_h + dec * yoff) + dsk[:, h:h + 1] * xs_h
    parts = _gated_group_norm(ysc_ref[...], z_ref[0], nw_ref[...])
    gw = SSD_WIDTH // SSD_GROUPS
    for g in range(SSD_GROUPS):
        y_ref[0, :, g * gw:(g + 1) * gw] = parts[g].astype(BF16)


def _ssd_sample(xbc, dt, z, h0, conv0, conv_w, conv_b, dt_bias, a_log, d_skip, norm_w):
    bd = xbc.shape[0]
    nr = CONV_DIM // LANES
    nx = SSD_WIDTH // LANES
    rep = SSD_HEADDIM
    row3 = lambda b: (b, 0, 0)
    const2 = lambda b: (0, 0)
    full2 = lambda shp: pl.BlockSpec(shp, const2)
    state = pl.BlockSpec((1, SSD_HEADS, SSD_HEADDIM, D_STATE), lambda b: (b, 0, 0, 0))
    y, st, cv = pl.pallas_call(
        _ssd_sample_kernel,
        grid=(bd,),
        in_specs=[pl.BlockSpec((1, 1, CONV_DIM), row3),
                  pl.BlockSpec((1, CONV_W - 1, CONV_DIM), row3),
                  pl.BlockSpec((1, 1, SSD_HEADS), row3),
                  pl.BlockSpec((1, 1, SSD_WIDTH), row3),
                  full2((CONV_W, CONV_DIM)), full2((1, CONV_DIM)), full2((1, SSD_HEADS)), full2((1, SSD_HEADS)),
                  full2((1, SSD_HEADS)), full2((1, SSD_WIDTH)), state,
                  pl.BlockSpec((1, nr, LANES), row3),
                  pl.BlockSpec((1, CONV_W - 1, nr, LANES), lambda b: (b, 0, 0, 0)),
                  pl.BlockSpec((1, nx, LANES), row3),
                  pl.BlockSpec((CONV_W, nr, LANES), lambda b: (0, 0, 0)),
                  full2((nr, LANES)), full2((nx, LANES))],
        out_specs=[pl.BlockSpec((1, 1, SSD_WIDTH), row3), state,
                   pl.BlockSpec((1, CONV_W - 1, CONV_DIM), row3)],
        out_shape=[jax.ShapeDtypeStruct((bd, 1, SSD_WIDTH), BF16),
                   jax.ShapeDtypeStruct((bd, SSD_HEADS, SSD_HEADDIM, D_STATE), F32),
                   jax.ShapeDtypeStruct((bd, CONV_W - 1, CONV_DIM), F32)],
        scratch_shapes=[pltpu.VMEM((1, SSD_WIDTH), F32)],
        compiler_params=_params(("arbitrary",)),
    )(xbc.reshape(bd, 1, CONV_DIM), conv0, dt.reshape(bd, 1, SSD_HEADS), z.reshape(bd, 1, SSD_WIDTH),
      conv_w, conv_b.reshape(1, -1), dt_bias.reshape(1, -1), a_log.reshape(1, -1), d_skip.reshape(1, -1),
      norm_w.reshape(1, -1), h0,
      xbc.reshape(bd, nr, LANES), conv0.reshape(bd, CONV_W - 1, nr, LANES),
      jnp.repeat(dt, rep, axis=-1).reshape(bd, nx, LANES), conv_w.reshape(CONV_W, nr, LANES),
      conv_b.reshape(nr, LANES), jnp.repeat(dt_bias, rep).reshape(nx, LANES))
    return y.reshape(bd, SSD_WIDTH), st, cv


def _outproj_kernel(att_ref, ssd_ref, x_ref, w_ref, g_ref, b_ref, wr_ref, br_ref, h_ref, route_ref, cnt_ref,
                    carry_ref, *, alpha):
    i = pl.program_id(0)
    tm = x_ref.shape[0]

    @pl.when(i == 0)
    def _():
        carry_ref[...] = jnp.zeros_like(carry_ref)

    mix = jnp.dot(att_ref[...], w_ref[0:ATT_WIDTH, :], preferred_element_type=F32)
    mix = mix + jnp.dot(ssd_ref[...], w_ref[ATT_WIDTH:ATT_WIDTH + SSD_WIDTH, :], preferred_element_type=F32)
    hval = _layer_norm(alpha * x_ref[...] + mix, g_ref[...], b_ref[...])
    h_ref[...] = hval

    logits = jnp.dot(hval.astype(BF16), wr_ref[...].astype(BF16), preferred_element_type=F32) + br_ref[...]

    lane = lax.broadcasted_iota(jnp.int32, (tm, N_EXPERTS), 1).astype(F32)
    work = logits
    chosen = jnp.zeros((tm, N_EXPERTS), F32)
    vals, idxs = [], []
    for _ in range(TOP_K):
        mk = jnp.max(work, axis=-1, keepdims=True)
        ik = jnp.min(jnp.where(work == mk, lane, float(N_EXPERTS)), axis=-1, keepdims=True)
        sel = lane == ik
        work = jnp.where(sel, -jnp.inf, work)
        chosen = jnp.where(sel, 1.0, chosen)
        vals.append(mk)
        idxs.append(ik)
    es = [jnp.exp(v - vals[0]) for v in vals]
    den = es[0] + es[1] + es[2] + es[3]

    ri = lax.broadcasted_iota(jnp.int32, (tm, tm), 0)
    ci = lax.broadcasted_iota(jnp.int32, (tm, tm), 1)
    before = jnp.where(ci < ri, 1.0, 0.0).astype(BF16)
    prefix = jnp.dot(before, chosen.astype(BF16), preferred_element_type=F32) + carry_ref[...]
    carry_ref[...] = carry_ref[...] + jnp.sum(chosen, axis=0, keepdims=True)
    cnt_ref[...] = carry_ref[...]

    olane = lax.broadcasted_iota(jnp.int32, (tm, LANES), 1)
    route = jnp.zeros((tm, LANES), F32)
    for k in range(TOP_K):
        rank_k = jnp.sum(jnp.where(lane == idxs[k], prefix, 0.0), axis=-1, keepdims=True)
        route = jnp.where(olane == k, idxs[k], route)
        route = jnp.where(olane == TOP_K + k, es[k] / den, route)
        route = jnp.where(olane == 2 * TOP_K + k, rank_k, route)
    route_ref[...] = route


def _outproj_router(att, ssd, x2d, w_out_bf16, ln_g, ln_b, w_router, b_router, alpha):
    m, d = x2d.shape
    tm = min(256, m)
    row = lambda i: (i, 0)
    const = lambda i: (0, 0)
    kern = functools.partial(_outproj_kernel, alpha=alpha)
    return pl.pallas_call(
        kern,
        grid=(m // tm,),
        in_specs=[pl.BlockSpec((tm, ATT_WIDTH), row), pl.BlockSpec((tm, SSD_WIDTH), row), pl.BlockSpec((tm, d), row),
                  pl.BlockSpec((ATT_WIDTH + SSD_WIDTH, d), const, pipeline_mode=pl.Buffered(1)),
                  pl.BlockSpec((1, d), const), pl.BlockSpec((1, d), const),
                  pl.BlockSpec((d, N_EXPERTS), const), pl.BlockSpec((1, N_EXPERTS), const)],
        out_specs=[pl.BlockSpec((tm, d), row), pl.BlockSpec((tm, LANES), row), pl.BlockSpec((1, N_EXPERTS), const)],
        out_shape=[jax.ShapeDtypeStruct((m, d), F32), jax.ShapeDtypeStruct((m, LANES), F32),
                   jax.ShapeDtypeStruct((1, N_EXPERTS), F32)],
        scratch_shapes=[pltpu.VMEM((1, N_EXPERTS), F32)],
        compiler_params=_params(("arbitrary",)),
    )(att, ssd, x2d, w_out_bf16, ln_g.reshape(1, d), ln_b.reshape(1, d), w_router, b_router.reshape(1, -1))


def _row_copy(src, dst, sem):
    return pltpu.make_async_copy(src, dst, sem)


def _scatter_kernel(cnt_ref, pst_ref, dest_ref, h_ref, *rest, first):
    xrows_ref, zero_ref, sem, zsem = rest[-4:]
    i = pl.program_id(0)
    tm = h_ref.shape[0]

    @pl.when((i == 0) & first)
    def _():
        zero_ref[...] = jnp.zeros_like(zero_ref)

        def per_expert(e, _):
            n = cnt_ref[e]
            base = pst_ref[e]
            end = (n + MOE_SUB - 1) // MOE_SUB * MOE_SUB

            def start(r, _):
                _row_copy(zero_ref.at[pl.ds(0, 1)], xrows_ref.at[pl.ds(base + r, 1)], zsem).start()
                return 0

            def wait(r, _):
                _row_copy(zero_ref.at[pl.ds(0, 1)], xrows_ref.at[pl.ds(base + r, 1)], zsem).wait()
                return 0

            lax.fori_loop(n, end, start, 0)
            lax.fori_loop(n, end, wait, 0)
            return 0

        lax.fori_loop(0, N_EXPERTS, per_expert, 0)

    def start(t, _):
        for k in range(TOP_K):
            d = dest_ref[0, 0, t * TOP_K + k]
            _row_copy(h_ref.at[pl.ds(t, 1)], xrows_ref.at[pl.ds(d, 1)], sem).start()
        return 0

    def wait(t, _):
        for k in range(TOP_K):
            _row_copy(h_ref.at[pl.ds(t, 1)], xrows_ref.at[pl.ds(0, 1)], sem).wait()
        return 0

    for t in range(tm):
        start(t, 0)
    for t in range(tm):
        wait(t, 0)


def _moe_scatter(h2d, dest, counts, pstart, n_rows, x_rows=None):
    m, d = h2d.shape
    tm = min(256, m)
    nt = m // tm
    dest3 = dest.reshape(nt, 1, tm * TOP_K)
    first = x_rows is None
    in_specs = [pl.BlockSpec((1, 1, tm * TOP_K), lambda i, c, p: (i, 0, 0), memory_space=pltpu.SMEM),
                pl.BlockSpec((tm, d), lambda i, c, p: (i, 0))]
    args = [counts, pstart, dest3, h2d]
    aliases = {}
    if not first:
        in_specs.append(pl.BlockSpec(memory_space=pl.ANY))
        args.append(x_rows)
        aliases = {len(args) - 1: 0}
    grid_spec = pltpu.PrefetchScalarGridSpec(
        num_scalar_prefetch=2,
        grid=(nt,),
        in_specs=in_specs,
        out_specs=pl.BlockSpec(memory_space=pl.ANY),
        scratch_shapes=[pltpu.VMEM((8, d), F32), pltpu.SemaphoreType.DMA(()), pltpu.SemaphoreType.DMA(())],
    )
    return pl.pallas_call(
        functools.partial(_scatter_kernel, first=first),
        grid_spec=grid_spec,
        out_shape=jax.ShapeDtypeStruct((n_rows, d), F32),
        input_output_aliases=aliases,
        compiler_params=_params(("arbitrary",)),
    )(*args)


def _moe_mlp_kernel(ie_ref, ib_ref, iv_ref, x_ref, w1g_ref, w1l_ref, b1g_ref, b1l_ref, w2_ref, b2_ref, o_ref,
                    xb_ref, wg_ref, wl_ref, w2b_ref):
    i = pl.program_id(0)
    j = pl.program_id(1)
    nvalid = iv_ref[i]
    nsub = (nvalid + MOE_SUB - 1) // MOE_SUB
    d = x_ref.shape[1]

    def rows(s):
        return pl.ds(pl.multiple_of(s * MOE_SUB, MOE_SUB), MOE_SUB)

    @pl.when(nvalid > 0)
    def _():
        @pl.when(j == 0)
        def _():
            def init(s, _):
                xb_ref[rows(s), :] = x_ref[rows(s), :].astype(BF16)
                o_ref[rows(s), :] = jnp.broadcast_to(b2_ref[0], (MOE_SUB, d))
                return 0

            lax.fori_loop(0, nsub, init, 0)

        def hidden(s):
            xb = xb_ref[rows(s), :]
            return (jnp.dot(xb, wg_ref[...], preferred_element_type=F32),
                    jnp.dot(xb, wl_ref[...], preferred_element_type=F32))

        def finish(s, hid):
            glu = jnp.minimum(hid[0] + b1g_ref[0], SWIGLU_LIMIT)
            lin = jnp.clip(hid[1] + b1l_ref[0], -SWIGLU_LIMIT, SWIGLU_LIMIT)
            act = glu * _sigmoid(SWIGLU_ALPHA * glu) * (lin + 1.0)
            o_ref[rows(s), :] += jnp.dot(act.astype(BF16), w2b_ref[...], preferred_element_type=F32)

        def step(s, hid):
            nxt = hidden(s + 1)
            finish(s, hid)
            return nxt

        xb0 = xb_ref[rows(0), :]
        wg_ref[...] = w1g_ref[0].astype(BF16)
        glu0 = jnp.dot(xb0, wg_ref[...], preferred_element_type=F32)
        wl_ref[...] = w1l_ref[0].astype(BF16)
        lin0 = jnp.dot(xb0, wl_ref[...], preferred_element_type=F32)
        w2b_ref[...] = w2_ref[0].astype(BF16)
        hid = (glu0, lin0)
        hid = lax.fori_loop(0, nsub - 1, step, hid)
        finish(nsub - 1, hid)


def _moe_mlp(x_rows, item_e, item_blk, item_valid, w1, b1, w2, b2):
    n_rows, d = x_rows.shape
    n_items = item_e.shape[0]
    d_ff = w2.shape[1]
    tf = MOE_F_TILE
    nf = d_ff // tf
    tmr = MOE_ROW_TILE

    def jj(i, j, iv):
        return jnp.where(iv[i] > 0, j, nf - 1)

    grid_spec = pltpu.PrefetchScalarGridSpec(
        num_scalar_prefetch=3,
        grid=(n_items, nf),
        in_specs=[pl.BlockSpec((tmr, d), lambda i, j, ie, ib, iv: (ib[i], 0)),
                  pl.BlockSpec((1, d, tf), lambda i, j, ie, ib, iv: (ie[i], 0, jj(i, j, iv))),
                  pl.BlockSpec((1, d, tf), lambda i, j, ie, ib, iv: (ie[i], 0, nf + jj(i, j, iv))),
                  pl.BlockSpec((1, 1, tf), lambda i, j, ie, ib, iv: (ie[i], 0, jj(i, j, iv))),
                  pl.BlockSpec((1, 1, tf), lambda i, j, ie, ib, iv: (ie[i], 0, nf + jj(i, j, iv))),
                  pl.BlockSpec((1, tf, d), lambda i, j, ie, ib, iv: (ie[i], jj(i, j, iv), 0)),
                  pl.BlockSpec((1, 1, d), lambda i, j, ie, ib, iv: (ie[i], 0, 0))],
        out_specs=pl.BlockSpec((tmr, d), lambda i, j, ie, ib, iv: (ib[i], 0)),
        scratch_shapes=[pltpu.VMEM((tmr, d), BF16), pltpu.VMEM((d, tf), BF16), pltpu.VMEM((d, tf), BF16),
                        pltpu.VMEM((tf, d), BF16)],
    )
    return pl.pallas_call(
        _moe_mlp_kernel,
        grid_spec=grid_spec,
        out_shape=jax.ShapeDtypeStruct((n_rows, d), F32),
        compiler_params=_params(("arbitrary", "arbitrary")),
    )(item_e, item_blk, item_valid, x_rows, w1, w1, b1.reshape(N_EXPERTS, 1, -1), b1.reshape(N_EXPERTS, 1, -1),
      w2, b2.reshape(N_EXPERTS, 1, -1))


def _combine_kernel(dest_ref, h_ref, route_ref, yrows_ref, g_ref, b_ref, o_ref, buf_ref, sem, *, alpha):
    tm = h_ref.shape[0]

    def start(t, _):
        for k in range(TOP_K):
            d = dest_ref[0, 0, t * TOP_K + k]
            _row_copy(yrows_ref.at[pl.ds(d, 1)], buf_ref.at[k, pl.ds(t, 1)], sem).start()
        return 0

    def wait(t, _):
        for k in range(TOP_K):
            _row_copy(yrows_ref.at[pl.ds(0, 1)], buf_ref.at[k, pl.ds(t, 1)], sem).wait()
        return 0

    for t in range(tm):
        start(t, 0)
    for t in range(tm):
        wait(t, 0)
    route = route_ref[...]
    acc = route[:, TOP_K:TOP_K + 1] * buf_ref[0]
    for k in range(1, TOP_K):
        acc = acc + route[:, TOP_K + k:TOP_K + k + 1] * buf_ref[k]
    o_ref[...] = _layer_norm(alpha * h_ref[...] + acc, g_ref[...], b_ref[...])


def _moe_combine(h2d, route, dest, y_rows, ln_g, ln_b, alpha):
    m, d = h2d.shape
    tm = min(128, m)
    nt = m // tm
    dest3 = dest.reshape(nt, 1, tm * TOP_K)
    kern = functools.partial(_combine_kernel, alpha=alpha)
    return pl.pallas_call(
        kern,
        grid=(nt,),
        in_specs=[pl.BlockSpec((1, 1, tm * TOP_K), lambda i: (i, 0, 0), memory_space=pltpu.SMEM),
                  pl.BlockSpec((tm, d), lambda i: (i, 0)),
                  pl.BlockSpec((tm, LANES), lambda i: (i, 0)),
                  pl.BlockSpec(memory_space=pl.ANY),
                  pl.BlockSpec((1, d), lambda i: (0, 0)), pl.BlockSpec((1, d), lambda i: (0, 0))],
        out_specs=pl.BlockSpec((tm, d), lambda i: (i, 0)),
        out_shape=jax.ShapeDtypeStruct((m, d), F32),
        scratch_shapes=[pltpu.VMEM((TOP_K, tm, d), F32), pltpu.SemaphoreType.DMA(())],
        compiler_params=_params(("arbitrary",)),
    )(dest3, h2d, route, y_rows, ln_g.reshape(1, d), ln_b.reshape(1, d))


def _moe_ln2(groups, w1, b1, w2, b2, ln_g, ln_b, alpha):
    tmr = MOE_ROW_TILE
    group_counts = [c.reshape(-1).astype(jnp.int32) for _, _, c in groups]
    counts = sum(group_counts)
    tiles = (counts + tmr - 1) // tmr
    tile_end = jnp.cumsum(tiles)
    tile_start = tile_end - tiles
    pstart = (tile_start * tmr).astype(jnp.int32)
    n_tok = sum(h.shape[0] for h, _, _ in groups)
    n_items = -(-(n_tok * TOP_K) // tmr) + N_EXPERTS
    n_rows = n_items * tmr
    it = jnp.arange(n_items, dtype=jnp.int32)
    total = tile_end[-1]
    it_c = jnp.minimum(it, total - 1)
    item_e = jnp.minimum(jnp.searchsorted(tile_end, it_c, side='right'), N_EXPERTS - 1).astype(jnp.int32)
    item_r = it_c - tile_start[item_e]
    item_blk = (tile_start[item_e] + item_r).astype(jnp.int32)
    item_valid = jnp.where(it < total, jnp.clip(counts[item_e] - item_r * tmr, 0, tmr), 0).astype(jnp.int32)

    dests = []
    earlier = jnp.zeros_like(counts)
    x_rows = None
    for (h2d, route, _), gc in zip(groups, group_counts):
        idx = route[:, 0:TOP_K].astype(jnp.int32)
        rank = route[:, 2 * TOP_K:3 * TOP_K].astype(jnp.int32)
        dest = (pstart[idx] + earlier[idx] + rank).reshape(-1)
        dests.append(dest)
        earlier = earlier + gc
        x_rows = _moe_scatter(h2d, dest, counts, pstart, n_rows, x_rows)
    y_rows = _moe_mlp(x_rows, item_e, item_blk, item_valid, w1, b1, w2, b2)
    return [_moe_combine(h2d, route, dest, y_rows, ln_g, ln_b, alpha)
            for (h2d, route, _), dest in zip(groups, dests)]


def kernel(x_prompt, x_sample, cache_k, cache_v, page_table, state_ssm, state_conv, w_in, w_out, lambda_qk,
           attn_subln_w, conv_w, conv_b, dt_bias, a_log, d_skip, ssd_norm_w, ln1_g, ln1_b, w_router, b_router,
           w_mlp1, b_mlp1, w_mlp2, b_mlp2, ln2_g, ln2_b):
    bp, sp, d = x_prompt.shape
    bd, sd, _ = x_sample.shape
    depth = w_in.shape[0]
    assert depth == 1 and sd == 1, "kernel supports the single-layer, single-token-decode configuration"
    alpha = (2.0 * depth) ** 0.25
    slopes = 2.0 ** (-8.0 * jnp.arange(1, ATT_HEADS + 1, dtype=F32) / ATT_HEADS)
    l = 0
    lam_init = 0.8 - 0.6 * math.exp(-0.3 * l)

    w_in_b = _cast_bf16(w_in[l])
    w_out_b = _cast_bf16(w_out[l])
    moe_w = (w_mlp1[l], b_mlp1[l], w_mlp2[l], b_mlp2[l], ln2_g[l], ln2_b[l])
    ssd_w = (conv_w[l], conv_b[l], dt_bias[l], a_log[l], d_skip[l], ssd_norm_w[l])

    xp = x_prompt.reshape(bp * sp, d)
    q, k, v, z, xbc, dt = _inproj(xp, w_in_b, BF16)
    att = _attn_prompt(q.reshape(bp, sp, -1), k.reshape(bp, sp, -1), v.reshape(bp, sp, -1), lambda_qk[l],
                       attn_subln_w[l], slopes, lam_init)
    ssm_zero = jnp.zeros((bp, SSD_HEADS, SSD_HEADDIM, D_STATE), F32)
    conv_zero = jnp.zeros((bp, CONV_W - 1, CONV_DIM), F32)
    ssd, ssm_p, conv_p = _ssd_prompt(xbc.reshape(bp, sp, -1), dt.reshape(bp, sp, -1), z.reshape(bp, sp, -1),
                                     ssm_zero, conv_zero, *ssd_w)
    h, route, counts = _outproj_router(att.reshape(bp * sp, -1), ssd.reshape(bp * sp, -1), xp, w_out_b,
                                       ln1_g[l], ln1_b[l], w_router[l], b_router[l], alpha)

    xs = x_sample.reshape(bd, d)
    qs, ks, vs, zs, xbcs, dts = _inproj(xs, w_in_b, F32)
    att_s = _attn_decode(qs, ks, vs, cache_k, cache_v, page_table, lambda_qk[l], attn_subln_w[l], slopes,
                         lam_init)
    ssd_s, ssm_s, conv_s = _ssd_sample(xbcs, dts, zs, state_ssm[l], state_conv[l], *ssd_w)
    hs, route_s, counts_s = _outproj_router(att_s, ssd_s, xs, w_out_b, ln1_g[l], ln1_b[l], w_router[l],
                                            b_router[l], alpha)
    y_prompt, y_sample = _moe_ln2([(h, route, counts), (hs, route_s, counts_s)], *moe_w, alpha)
    y_prompt = y_prompt.reshape(bp, sp, d)
    y_sample = y_sample.reshape(bd, sd, d)

    hshape = (ATT_HEADS, 2 * ATT_DH)
    return (y_prompt, y_sample,
            k.reshape(1, bp, sp, *hshape), v.reshape(1, bp, sp, *hshape), ssm_p[None], conv_p[None],
            ks.reshape(1, bd, sd, *hshape), vs.reshape(1, bd, sd, *hshape), ssm_s[None], conv_s[None])
```

```python
import functools
import math

import jax
import jax.numpy as jnp
from jax import lax
from jax.experimental import pallas as pl
from jax.experimental.pallas import tpu as pltpu

F32 = jnp.float32
BF16 = jnp.bfloat16

ATT_DH = 64
ATT_HEADS = 8
ATT_WIDTH = ATT_HEADS * 2 * ATT_DH
ATT_SCALE = ATT_DH ** -0.5
SSD_HEADDIM = 64
SSD_HEADS = 16
SSD_WIDTH = SSD_HEADS * SSD_HEADDIM
SSD_GROUPS = 2
D_STATE = 128
CONV_W = 4
CONV_DIM = SSD_WIDTH + 2 * SSD_GROUPS * D_STATE
SSD_CHUNK = 128
N_EXPERTS = 32
TOP_K = 4
SWIGLU_ALPHA = 1.702
SWIGLU_LIMIT = 7.0
LN_EPS = 1e-5
RMS_EPS = 1e-5
PAGE_SIZE = 128

V7X_VMEM_LIMIT_BYTES = 56 * 1024 * 1024
LANES = 128

MOE_ROW_TILE = 1024
MOE_SUB = 256
MOE_F_TILE = 256


def _params(sem, vmem=V7X_VMEM_LIMIT_BYTES):
    return pltpu.CompilerParams(dimension_semantics=sem, vmem_limit_bytes=vmem)


def _sigmoid(x):
    return 1.0 / (1.0 + jnp.exp(-x))


def _silu(x):
    return x * _sigmoid(x)


def _softplus(x):
    return jnp.maximum(x, 0.0) + jnp.log1p(jnp.exp(-jnp.abs(x)))


def _layer_norm(x, g, b):
    mu = jnp.mean(x, axis=-1, keepdims=True)
    xc = x - mu
    var = jnp.mean(xc * xc, axis=-1, keepdims=True)
    return xc * lax.rsqrt(var + LN_EPS) * g + b


def _split3(x):
    a = x.astype(BF16)
    r = x - a.astype(F32)
    b = r.astype(BF16)
    c = (r - b.astype(F32)).astype(BF16)
    return a, b, c


def _lam(lq):
    s01 = jnp.sum(lq[0:1, :] * lq[1:2, :], axis=-1, keepdims=True)
    s23 = jnp.sum(lq[2:3, :] * lq[3:4, :], axis=-1, keepdims=True)
    return jnp.exp(s01) - jnp.exp(s23)


def _cast_kernel(w_ref, o_ref):
    o_ref[...] = w_ref[...].astype(BF16)


def _cast_bf16(w, row_tile=256):
    r, c = w.shape
    rt = min(row_tile, r)
    return pl.pallas_call(
        _cast_kernel,
        grid=(r // rt,),
        in_specs=[pl.BlockSpec((rt, c), lambda i: (i, 0))],
        out_specs=pl.BlockSpec((rt, c), lambda i: (i, 0)),
        out_shape=jax.ShapeDtypeStruct((r, c), BF16),
        compiler_params=_params(("arbitrary",)),
    )(w)


def _inproj_kernel(x_ref, w_ref, q_ref, k_ref, v_ref, z_ref, xbc_ref, dt_ref):
    xb = x_ref[...].astype(BF16)

    def mm(c0, c1):
        return jnp.dot(xb, w_ref[:, c0:c1], preferred_element_type=F32)

    a = ATT_WIDTH
    q_ref[...] = (mm(0, a) * ATT_SCALE).astype(q_ref.dtype)
    k_ref[...] = mm(a, 2 * a)
    v_ref[...] = mm(2 * a, 3 * a)
    z_ref[...] = mm(3 * a, 3 * a + SSD_WIDTH)
    c0 = 3 * a + SSD_WIDTH
    xbc_ref[...] = mm(c0, c0 + CONV_DIM)
    dt_ref[...] = mm(c0 + CONV_DIM, c0 + CONV_DIM + SSD_HEADS)


def _inproj(x2d, w_in_bf16, q_dtype):
    m, d = x2d.shape
    ncol = w_in_bf16.shape[1]
    tm = min(256, m)
    row = lambda i: (i, 0)
    widths = (ATT_WIDTH, ATT_WIDTH, ATT_WIDTH, SSD_WIDTH, CONV_DIM, SSD_HEADS)
    dtypes = (q_dtype, F32, F32, F32, F32, F32)
    return pl.pallas_call(
        _inproj_kernel,
        grid=(m // tm,),
        in_specs=[pl.BlockSpec((tm, d), row),
                  pl.BlockSpec((d, ncol), lambda i: (0, 0), pipeline_mode=pl.Buffered(1))],
        out_specs=[pl.BlockSpec((tm, w), row) for w in widths],
        out_shape=[jax.ShapeDtypeStruct((m, w), dt) for w, dt in zip(widths, dtypes)],
        compiler_params=_params(("arbitrary",)),
    )(x2d, w_in_bf16)


def _attn_prompt_kernel(slopes_ref, qt_ref, k_ref, v_ref, lq_ref, w_ref, o_ref, kb_ref, vt_ref, *, tq, hps,
                        lam_init):
    hp = pl.program_id(1)
    qi = pl.program_id(2)
    hd = 2 * ATT_DH
    vrows = vt_ref.shape[1]

    @pl.when(qi == 0)
    def _():
        kb_ref[...] = k_ref[0].astype(BF16)
        for hh in range(hps):
            vt_ref[hh, 0:hd, :] = v_ref[0, :, hh * hd:(hh + 1) * hd].T.astype(BF16)
            vt_ref[hh, hd:vrows, :] = jnp.ones((vrows - hd, vt_ref.shape[2]), BF16)

    drow = lax.broadcasted_iota(jnp.int32, (hd, tq), 0)
    kr = lax.broadcasted_iota(jnp.int32, (tq, tq), 0)
    qc = lax.broadcasted_iota(jnp.int32, (tq, tq), 1)
    rel = (qc - kr).astype(F32)
    future = kr > qc
    slopes = [slopes_ref[hp * hps + hh] for hh in range(hps)]
    qts = []
    for hh in range(hps):
        qt = qt_ref[0, hh * hd:(hh + 1) * hd, :]
        zero = jnp.zeros_like(qt)
        qts += [jnp.where(drow < ATT_DH, qt, zero), jnp.where(drow >= ATT_DH, qt, zero)]

    def scores(j):
        kb = kb_ref[pl.ds(pl.multiple_of(j * tq, tq), tq), :]
        return tuple(jnp.dot(kb[:, (c // 2) * hd:(c // 2 + 1) * hd], qts[c], preferred_element_type=F32)
                     for c in range(2 * hps))

    def block(j, raw, stats, diag):
        cols = pl.ds(pl.multiple_of(j * tq, tq), tq)
        dist = rel + jnp.full((1, 1), (qi - j) * tq, jnp.int32).astype(F32)
        out = []
        for c in range(2 * hps):
            hh = c // 2
            m, l, a = stats[c]
            s = raw[c] - dist * slopes[hh]
            if diag:
                s = jnp.where(future, -jnp.inf, s)
            mn = jnp.maximum(m, jnp.max(s, axis=0, keepdims=True))
            p = jnp.exp(s - mn)
            al = jnp.exp(m - mn)
            pv = jnp.dot(vt_ref[hh, :, cols], p.astype(BF16), preferred_element_type=F32)
            l = al * l + pv[hd:hd + 1, :]
            a = al * a + pv[0:hd, :]
            out.append((mn, l, a))
        return tuple(out)

    def step(j, carry):
        raw, stats = carry
        nxt = scores(j + 1)
        return nxt, block(j, raw, stats, False)

    init1 = (jnp.full((1, tq), -1e30, F32), jnp.zeros((1, tq), F32), jnp.zeros((hd, tq), F32))
    raw, stats = lax.fori_loop(0, qi, step, (scores(0), (init1,) * (2 * hps)))
    stats = block(qi, raw, stats, True)
    lam = _lam(lq_ref[...]) + lam_init
    for hh in range(hps):
        (_, l0, a0), (_, l1, a1) = stats[2 * hh], stats[2 * hh + 1]
        o = a0 / l0 - lam * (a1 / l1)
        o = o * lax.rsqrt(jnp.mean(o * o, axis=0, keepdims=True) + RMS_EPS) * w_ref[...] * (1.0 - lam_init)
        o_ref[0, :, hh * hd:(hh + 1) * hd] = o.T.astype(BF16)


def _attn_prompt(q, k, v, lambda_qk, subln_w, slopes, lam_init):
    b, t, _ = k.shape
    tq = min(256, t)
    hd = 2 * ATT_DH
    hps = 2
    ones_rows = 16
    qt = jnp.swapaxes(q, 1, 2)
    kern = functools.partial(_attn_prompt_kernel, tq=tq, hps=hps, lam_init=lam_init)
    return pl.pallas_call(
        kern,
        grid=(b, ATT_HEADS // hps, t // tq),
        in_specs=[pl.BlockSpec(memory_space=pltpu.SMEM),
                  pl.BlockSpec((1, hps * hd, tq), lambda bi, h, qi: (bi, h, qi)),
                  pl.BlockSpec((1, t, hps * hd), lambda bi, h, qi: (bi, 0, h)),
                  pl.BlockSpec((1, t, hps * hd), lambda bi, h, qi: (bi, 0, h)),
                  pl.BlockSpec((4, ATT_DH), lambda bi, h, qi: (0, 0)),
                  pl.BlockSpec((hd, 1), lambda bi, h, qi: (0, 0))],
        out_specs=pl.BlockSpec((1, tq, hps * hd), lambda bi, h, qi: (bi, qi, h)),
        out_shape=jax.ShapeDtypeStruct((b, t, ATT_WIDTH), BF16),
        scratch_shapes=[pltpu.VMEM((t, hps * hd), BF16), pltpu.VMEM((hps, hd + ones_rows, t), BF16)],
        compiler_params=_params(("arbitrary", "arbitrary", "arbitrary")),
    )(slopes, qt, k, v, lambda_qk, subln_w.reshape(hd, 1))


def _attn_decode_kernel(pt_ref, q_ref, kn_ref, vn_ref, slope_ref, lq_ref, w_ref, *rest, pps, past_len, lam_init):
    kp_refs, vp_refs = rest[:pps], rest[pps:2 * pps]
    o_ref, qt_ref, s_ref, a_ref, m_ref, snew_ref, anew_ref, acc_ref = rest[2 * pps:]
    ph = pl.program_id(1)
    p = pl.program_id(2)
    n_steps = pl.num_programs(2)
    nh = ATT_HEADS
    nrow = 2 * nh
    hd = 2 * ATT_DH
    plane = PAGE_SIZE * nh
    n_pages = past_len // PAGE_SIZE
    per_vreg = LANES // nh

    def page_lanes(page):
        return pl.ds(pl.multiple_of(page * plane, plane), plane)

    @pl.when((ph == 0) & (p == 0))
    def _():
        q8 = q_ref[0]
        lane = lax.broadcasted_iota(jnp.int32, (nh, hd), 1)
        qt = jnp.concatenate([jnp.where(lane < ATT_DH, q8, 0.0), jnp.where(lane >= ATT_DH, q8, 0.0)], axis=0)
        qt_ref[...] = qt.astype(BF16)
        kn = jnp.concatenate([kn_ref[0], kn_ref[0]], axis=0)
        s_new = jnp.sum(qt * kn, axis=-1, keepdims=True)
        snew_ref[...] = s_new
        m_ref[...] = s_new

    @pl.when(ph == 0)
    def _():
        lane = lax.broadcasted_iota(jnp.int32, (nrow, plane), 1)
        row = lax.broadcasted_iota(jnp.int32, (nrow, plane), 0)
        own_head = (lane % nh) == (row % nh)
        for i in range(pps):
            page = p * pps + i
            kflat = kp_refs[i][0].reshape(plane, hd).astype(BF16)
            s = lax.dot_general(qt_ref[...], kflat, (((1,), (1,)), ((), ())), preferred_element_type=F32)
            dist = (past_len - page * PAGE_SIZE - lane // nh).astype(F32)
            s = jnp.where(own_head, s - slope_ref[...] * dist, -jnp.inf)
            s_ref[:, page_lanes(page)] = s
            m_ref[...] = jnp.maximum(m_ref[...], jnp.max(s, axis=-1, keepdims=True))

    @pl.when((ph == 1) & (p == 0))
    def _():
        m = m_ref[...]
        e_new = jnp.exp(snew_ref[...] - m)

        def expsum(g, part):
            for u in range(pps):
                e = jnp.exp(s_ref[:, page_lanes(g * pps + u)] - m)
                s_ref[:, page_lanes(g * pps + u)] = e
                for c in range(plane // LANES):
                    part = part + e[:, c * LANES:(c + 1) * LANES]
            return part

        part = lax.fori_loop(0, n_pages // pps, expsum, jnp.zeros((nrow, LANES), F32))
        den = jnp.sum(part, axis=-1, keepdims=True) + e_new
        inv = 1.0 / den
        lam = _lam(lq_ref[...]) + lam_init

        def combine(g, _):
            for u in range(pps):
                pn = s_ref[:, page_lanes(g * pps + u)] * inv
                a_ref[:, page_lanes(g * pps + u)] = pn[0:nh] - lam * pn[nh:nrow]
            return 0

        lax.fori_loop(0, n_pages // pps, combine, 0)
        pn_new = e_new * inv
        anew_ref[...] = pn_new[0:nh] - lam * pn_new[nh:nrow]
        acc_ref[...] = jnp.zeros_like(acc_ref)

    @pl.when(ph == 1)
    def _():
        lane = lax.broadcasted_iota(jnp.int32, (nh, LANES), 1)
        n_acc = acc_ref.shape[0]
        for i in range(pps):
            page = p * pps + i

            accs = [acc_ref[k] for k in range(n_acc)]
            for g in range(plane // LANES):
                av = a_ref[:, pl.ds(pl.multiple_of(page * plane + g * LANES, LANES), LANES)]
                for jj in range(per_vreg):
                    sel = (lane >= jj * nh) & (lane < (jj + 1) * nh)
                    wcol = jnp.sum(jnp.where(sel, av, 0.0), axis=-1, keepdims=True)
                    accs[jj % n_acc] = accs[jj % n_acc] + wcol * vp_refs[i][0, g * per_vreg + jj]
            for k in range(n_acc):
                acc_ref[k] = accs[k]

    @pl.when((ph == 1) & (p == n_steps - 1))
    def _():
        o = anew_ref[...] * vn_ref[0]
        for k in range(acc_ref.shape[0]):
            o = o + acc_ref[k]
        o = o * lax.rsqrt(jnp.mean(o * o, axis=-1, keepdims=True) + RMS_EPS) * w_ref[...] * (1.0 - lam_init)
        o_ref[0] = o.astype(BF16)


def _attn_decode(q, k_new, v_new, cache_k, cache_v, page_table, lambda_qk, subln_w, slopes, lam_init):
    bd = q.shape[0]
    n_pages = page_table.shape[1]
    nh = ATT_HEADS
    nrow = 2 * nh
    hd = 2 * ATT_DH
    pps = max(c for c in (1, 2, 4, 8) if n_pages % c == 0)
    n_steps = n_pages // pps
    kp = cache_k.reshape(-1, PAGE_SIZE, nh, hd)
    vp = cache_v.reshape(-1, PAGE_SIZE, nh, hd)
    slope_rows = jnp.tile(slopes, 2).reshape(nrow, 1)
    past_len = n_pages * PAGE_SIZE
    kern = functools.partial(_attn_decode_kernel, pps=pps, past_len=past_len, lam_init=lam_init)
    head3 = lambda b, ph, p, pt: (b, 0, 0)
    const2 = lambda b, ph, p, pt: (0, 0)

    def kpage(i):
        return lambda b, ph, p, pt: (pt[b * n_pages + jnp.where(ph == 0, p, n_steps - 1) * pps + i], 0, 0, 0)

    def vpage(i):
        return lambda b, ph, p, pt: (pt[b * n_pages + jnp.where(ph == 0, 0, p) * pps + i], 0, 0, 0)

    page_block = (1, PAGE_SIZE, nh, hd)
    n_acc = 4
    grid_spec = pltpu.PrefetchScalarGridSpec(
        num_scalar_prefetch=1,
        grid=(bd, 2, n_steps),
        in_specs=[pl.BlockSpec((1, nh, hd), head3),
                  pl.BlockSpec((1, nh, hd), head3),
                  pl.BlockSpec((1, nh, hd), head3),
                  pl.BlockSpec((nrow, 1), const2),
                  pl.BlockSpec((4, ATT_DH), const2),
                  pl.BlockSpec((1, hd), const2)]
                 + [pl.BlockSpec(page_block, kpage(i)) for i in range(pps)]
                 + [pl.BlockSpec(page_block, vpage(i)) for i in range(pps)],
        out_specs=pl.BlockSpec((1, nh, hd), head3),
        scratch_shapes=[pltpu.VMEM((nrow, hd), BF16), pltpu.VMEM((nrow, past_len * nh), F32),
                        pltpu.VMEM((nh, past_len * nh), F32), pltpu.VMEM((nrow, 1), F32),
                        pltpu.VMEM((nrow, 1), F32), pltpu.VMEM((nh, 1), F32),
                        pltpu.VMEM((n_acc, nh, hd), F32)],
    )
    out = pl.pallas_call(
        kern,
        grid_spec=grid_spec,
        out_shape=jax.ShapeDtypeStruct((bd, nh, hd), BF16),
        compiler_params=_params(("arbitrary", "arbitrary", "arbitrary")),
    )(page_table.reshape(-1), q.reshape(bd, nh, hd), k_new.reshape(bd, nh, hd), v_new.reshape(bd, nh, hd),
      slope_rows, lambda_qk, subln_w.reshape(1, hd), *([kp] * pps), *([vp] * pps))
    return out.reshape(bd, ATT_WIDTH)


def _gated_group_norm(y, z, w):
    yg = y * _silu(z)
    gw = SSD_WIDTH // SSD_GROUPS
    parts = []
    for g in range(SSD_GROUPS):
        v = yg[:, g * gw:(g + 1) * gw]
        parts.append(v * lax.rsqrt(jnp.mean(v * v, axis=-1, keepdims=True) + RMS_EPS) * w[:, g * gw:(g + 1) * gw])
    return parts


def _ssd_prompt_kernel(xbc_ref, dt_ref, dtt_ref, z_ref, cw_ref, cb_ref, dtb_ref, dtbt_ref, al_ref, alt_ref,
                       dsk_ref, nw_ref, h0_ref, c0_ref, y_ref, st_ref, cv_ref, xpad_ref, ysc_ref, xdd_ref):
    c = pl.program_id(1)
    nc = pl.num_programs(1)
    L = SSD_CHUNK
    P = SSD_HEADDIM
    hpg = SSD_HEADS // SSD_GROUPS
    halo = 8

    @pl.when(c == 0)
    def _():
        st_ref[...] = h0_ref[...]
        xpad_ref[0:halo, :] = jnp.zeros((halo, CONV_DIM), F32)
        xpad_ref[halo - (CONV_W - 1):halo, :] = c0_ref[0]

    xc = xbc_ref[0]
    xpad_ref[halo:halo + L, :] = xc
    conv = cb_ref[...]
    for i in range(CONV_W - 1):
        sh = CONV_W - 1 - i
        conv = conv + xpad_ref[halo - sh:halo - sh + L, :] * cw_ref[i:i + 1, :]
    conv = conv + xc * cw_ref[CONV_W - 1:CONV_W, :]
    tail = xc[L - (CONV_W - 1):L, :]
    xpad_ref[halo - (CONV_W - 1):halo, :] = tail

    @pl.when(c == nc - 1)
    def _():
        cv_ref[0] = tail

    act = _silu(conv)
    xs = act[:, :SSD_WIDTH]
    bmat = [act[:, SSD_WIDTH + g * D_STATE:SSD_WIDTH + (g + 1) * D_STATE].astype(BF16) for g in range(SSD_GROUPS)]
    c_off = SSD_WIDTH + SSD_GROUPS * D_STATE
    cmat = [act[:, c_off + g * D_STATE:c_off + (g + 1) * D_STATE].astype(BF16) for g in range(SSD_GROUPS)]

    dtp = _softplus(dt_ref[0] + dtb_ref[...])
    dtpt = _softplus(dtt_ref[0] + dtbt_ref[...])
    da = dtp * (-jnp.exp(al_ref[...]))
    dat = dtpt * (-jnp.exp(alt_ref[...]))
    ri = lax.broadcasted_iota(jnp.int32, (L, L), 0)
    ci = lax.broadcasted_iota(jnp.int32, (L, L), 1)
    causal = ri >= ci
    tri = jnp.where(causal, 1.0, 0.0).astype(BF16)
    trit = jnp.where(ci >= ri, 1.0, 0.0).astype(BF16)
    cs = sum(jnp.dot(tri, part, preferred_element_type=F32) for part in _split3(da))
    cst = sum(jnp.dot(part, trit, preferred_element_type=F32) for part in _split3(dat))

    cb = [lax.dot_general(cmat[g], bmat[g], (((1,), (1,)), ((), ())), preferred_element_type=F32)
          for g in range(SSD_GROUPS)]
    dsk = dsk_ref[...]

    for h in range(SSD_HEADS):
        g = h // hpg
        cs_col = cs[:, h:h + 1]
        diff = cs_col - cst[h:h + 1, :]
        lmat = jnp.exp(jnp.where(causal, diff, -jnp.inf))
        mmat = (cb[g] * lmat).astype(BF16)
        xs_h = xs[:, h * P:(h + 1) * P]
        xd_h = xs_h * dtp[:, h:h + 1]
        y = jnp.dot(mmat, xd_h.astype(BF16), preferred_element_type=F32)
        st = st_ref[0, h]
        yoff = lax.dot_general(cmat[g], st.astype(BF16), (((1,), (1,)), ((), ())), preferred_element_type=F32)
        y = y + jnp.exp(cs_col) * yoff + dsk[:, h:h + 1] * xs_h
        ysc_ref[:, h * P:(h + 1) * P] = y
        cs_last = cs[L - 1:L, h:h + 1]
        xdd_ref[:, h * P:(h + 1) * P] = xd_h * jnp.exp(cs_last - cs_col)

    xddt = xdd_ref[...].T
    for h in range(SSD_HEADS):
        g = h // hpg
        new = jnp.dot(xddt[h * P:(h + 1) * P, :].astype(BF16), bmat[g], preferred_element_type=F32)
        cs_last = cs[L - 1:L, h:h + 1]
        st_ref[0, h] = jnp.exp(cs_last) * st_ref[0, h] + new

    parts = _gated_group_norm(ysc_ref[...], z_ref[0], nw_ref[...])
    gw = SSD_WIDTH // SSD_GROUPS
    for g in range(SSD_GROUPS):
        y_ref[0, :, g * gw:(g + 1) * gw] = parts[g].astype(BF16)


def _ssd_prompt(xbc, dt, z, h0, conv0, conv_w, conv_b, dt_bias, a_log, d_skip, norm_w):
    b, t, _ = xbc.shape
    L = SSD_CHUNK
    nc = t // L
    dtt = jnp.swapaxes(dt, 1, 2)
    seq = lambda bi, ci: (bi, ci, 0)
    const2 = lambda bi, ci: (0, 0)
    full2 = lambda shp: pl.BlockSpec(shp, const2)
    return pl.pallas_call(
        _ssd_prompt_kernel,
        grid=(b, nc),
        in_specs=[pl.BlockSpec((1, L, CONV_DIM), seq),
                  pl.BlockSpec((1, L, SSD_HEADS), seq),
                  pl.BlockSpec((1, SSD_HEADS, L), lambda bi, ci: (bi, 0, ci)),
                  pl.BlockSpec((1, L, SSD_WIDTH), seq),
                  full2((CONV_W, CONV_DIM)), full2((1, CONV_DIM)),
                  full2((1, SSD_HEADS)), full2((SSD_HEADS, 1)),
                  full2((1, SSD_HEADS)), full2((SSD_HEADS, 1)),
                  full2((1, SSD_HEADS)), full2((1, SSD_WIDTH)),
                  pl.BlockSpec((1, SSD_HEADS, SSD_HEADDIM, D_STATE), lambda bi, ci: (bi, 0, 0, 0)),
                  pl.BlockSpec((1, CONV_W - 1, CONV_DIM), lambda bi, ci: (bi, 0, 0))],
        out_specs=[pl.BlockSpec((1, L, SSD_WIDTH), seq),
                   pl.BlockSpec((1, SSD_HEADS, SSD_HEADDIM, D_STATE), lambda bi, ci: (bi, 0, 0, 0)),
                   pl.BlockSpec((1, CONV_W - 1, CONV_DIM), lambda bi, ci: (bi, 0, 0))],
        out_shape=[jax.ShapeDtypeStruct((b, t, SSD_WIDTH), BF16),
                   jax.ShapeDtypeStruct((b, SSD_HEADS, SSD_HEADDIM, D_STATE), F32),
                   jax.ShapeDtypeStruct((b, CONV_W - 1, CONV_DIM), F32)],
        scratch_shapes=[pltpu.VMEM((8 + L, CONV_DIM), F32), pltpu.VMEM((L, SSD_WIDTH), F32),
                        pltpu.VMEM((L, SSD_WIDTH), F32)],
        compiler_params=_params(("arbitrary", "arbitrary")),
    )(xbc, dt, dtt, z, conv_w, conv_b.reshape(1, -1), dt_bias.reshape(1, -1), dt_bias.reshape(-1, 1),
      a_log.reshape(1, -1), a_log.reshape(-1, 1), d_skip.reshape(1, -1), norm_w.reshape(1, -1), h0, conv0)


def _bf16_round(v):
    return v.astype(BF16).astype(F32)


def _ssd_sample_kernel(xbc_ref, ci_ref, dt_ref, z_ref, cw_ref, cb_ref, dtb_ref, al_ref, dsk_ref, nw_ref, h0_ref,
                       xbc8_ref, ci8_ref, dt8_ref, cw8_ref, cb8_ref, dtb8_ref, y_ref, st_ref, cv_ref, ysc_ref):
    P = SSD_HEADDIM
    hpg = SSD_HEADS // SSD_GROUPS
    xrow = xbc_ref[0]
    hist = ci_ref[0]
    conv = cb_ref[...]
    for i in range(CONV_W - 1):
        conv = conv + hist[i:i + 1, :] * cw_ref[i:i + 1, :]
    conv = conv + xrow * cw_ref[CONV_W - 1:CONV_W, :]
    cv_ref[0, 0:CONV_W - 2, :] = hist[1:CONV_W - 1, :]
    cv_ref[0, CONV_W - 2:CONV_W - 1, :] = xrow
    act = _silu(conv)
    xs = act[:, :SSD_WIDTH]
    c_off = SSD_WIDTH + SSD_GROUPS * D_STATE
    dtp = _softplus(dt_ref[0] + dtb_ref[...])
    decay = jnp.exp(dtp * (-jnp.exp(al_ref[...])))
    dsk = dsk_ref[...]

    nx = SSD_WIDTH // LANES
    conv8 = cb8_ref[0:nx, :]
    for i in range(CONV_W - 1):
        conv8 = conv8 + ci8_ref[0, i, 0:nx, :] * cw8_ref[i, 0:nx, :]
    conv8 = conv8 + xbc8_ref[0, 0:nx, :] * cw8_ref[CONV_W - 1, 0:nx, :]
    xd8 = _silu(conv8) * _softplus(dt8_ref[0] + dtb8_ref[...])
    xdt = xd8.T

    brows = [act[:, SSD_WIDTH + g * D_STATE:SSD_WIDTH + (g + 1) * D_STATE] for g in range(SSD_GROUPS)]
    crows = [act[:, c_off + g * D_STATE:c_off + (g + 1) * D_STATE] for g in range(SSD_GROUPS)]
    cbs = [jnp.sum(_bf16_round(brows[g]) * _bf16_round(crows[g]), axis=-1, keepdims=True) for g in range(SSD_GROUPS)]
    c8s = [jnp.broadcast_to(crows[g], (8, D_STATE)).astype(BF16) for g in range(SSD_GROUPS)]
    hpr = LANES // P
    for h in range(SSD_HEADS):
        g = h // hpg
        xs_h = xs[:, h * P:(h + 1) * P]
        xd_h = xs_h * dtp[:, h:h + 1]
        xcol = xdt[(h % hpr) * P:(h % hpr + 1) * P, h // hpr:h // hpr + 1]
        dec = decay[:, h:h + 1]
        h0q = h0_ref[0, h].astype(BF16)
        st_ref[0, h] = _bf16_round(dec) * h0q.astype(F32) + _bf16_round(xcol * brows[g])
        yoff = lax.dot_general(c8s[g], h0q, (((1,), (1,)), ((), ())), preferred_element_type=F32)[0:1, :]
        ysc_ref[:, h * P:(h + 1) * P] = (cbs[g] * xd_h + dec * yoff) + dsk[:, h:h + 1] * xs_h
    parts = _gated_group_norm(ysc_ref[...], z_ref[0], nw_ref[...])
    gw = SSD_WIDTH // SSD_GROUPS
    for g in range(SSD_GROUPS):
        y_ref[0, :, g * gw:(g + 1) * gw] = parts[g].astype(BF16)


def _ssd_sample(xbc, dt, z, h0, conv0, conv_w, conv_b, dt_bias, a_log, d_skip, norm_w):
    bd = xbc.shape[0]
    nr = CONV_DIM // LANES
    nx = SSD_WIDTH // LANES
    rep = SSD_HEADDIM
    row3 = lambda b: (b, 0, 0)
    const2 = lambda b: (0, 0)
    full2 = lambda shp: pl.BlockSpec(shp, const2)
    state = pl.BlockSpec((1, SSD_HEADS, SSD_HEADDIM, D_STATE), lambda b: (b, 0, 0, 0))
    y, st, cv = pl.pallas_call(
        _ssd_sample_kernel,
        grid=(bd,),
        in_specs=[pl.BlockSpec((1, 1, CONV_DIM), row3),
                  pl.BlockSpec((1, CONV_W - 1, CONV_DIM), row3),
                  pl.BlockSpec((1, 1, SSD_HEADS), row3),
                  pl.BlockSpec((1, 1, SSD_WIDTH), row3),
                  full2((CONV_W, CONV_DIM)), full2((1, CONV_DIM)), full2((1, SSD_HEADS)), full2((1, SSD_HEADS)),
                  full2((1, SSD_HEADS)), full2((1, SSD_WIDTH)), state,
                  pl.BlockSpec((1, nr, LANES), row3),
                  pl.BlockSpec((1, CONV_W - 1, nr, LANES), lambda b: (b, 0, 0, 0)),
                  pl.BlockSpec((1, nx, LANES), row3),
                  pl.BlockSpec((CONV_W, nr, LANES), lambda b: (0, 0, 0)),
                  full2((nr, LANES)), full2((nx, LANES))],
        out_specs=[pl.BlockSpec((1, 1, SSD_WIDTH), row3), state,
                   pl.BlockSpec((1, CONV_W - 1, CONV_DIM), row3)],
        out_shape=[jax.ShapeDtypeStruct((bd, 1, SSD_WIDTH), BF16),
                   jax.ShapeDtypeStruct((bd, SSD_HEADS, SSD_HEADDIM, D_STATE), F32),
                   jax.ShapeDtypeStruct((bd, CONV_W - 1, CONV_DIM), F32)],
        scratch_shapes=[pltpu.VMEM((1, SSD_WIDTH), F32)],
        compiler_params=_params(("arbitrary",)),
    )(xbc.reshape(bd, 1, CONV_DIM), conv0, dt.reshape(bd, 1, SSD_HEADS), z.reshape(bd, 1, SSD_WIDTH),
      conv_w, conv_b.reshape(1, -1), dt_bias.reshape(1, -1), a_log.reshape(1, -1), d_skip.reshape(1, -1),
      norm_w.reshape(1, -1), h0,
      xbc.reshape(bd, nr, LANES), conv0.reshape(bd, CONV_W - 1, nr, LANES),
      jnp.repeat(dt, rep, axis=-1).reshape(bd, nx, LANES), conv_w.reshape(CONV_W, nr, LANES),
      conv_b.reshape(nr, LANES), jnp.repeat(dt_bias, rep).reshape(nx, LANES))
    return y.reshape(bd, SSD_WIDTH), st, cv


def _outproj_kernel(att_ref, ssd_ref, x_ref, w_ref, g_ref, b_ref, wr_ref, br_ref, h_ref, route_ref, cnt_ref,
                    carry_ref, *, alpha):
    i = pl.program_id(0)
    tm = x_ref.shape[0]

    @pl.when(i == 0)
    def _():
        carry_ref[...] = jnp.zeros_like(carry_ref)

    mix = jnp.dot(att_ref[...], w_ref[0:ATT_WIDTH, :], preferred_element_type=F32)
    mix = mix + jnp.dot(ssd_ref[...], w_ref[ATT_WIDTH:ATT_WIDTH + SSD_WIDTH, :], preferred_element_type=F32)
    hval = _layer_norm(alpha * x_ref[...] + mix, g_ref[...], b_ref[...])
    h_ref[...] = hval

    logits = jnp.dot(hval.astype(BF16), wr_ref[...].astype(BF16), preferred_element_type=F32) + br_ref[...]

    lane = lax.broadcasted_iota(jnp.int32, (tm, N_EXPERTS), 1).astype(F32)
    work = logits
    chosen = jnp.zeros((tm, N_EXPERTS), F32)
    vals, idxs = [], []
    for _ in range(TOP_K):
        mk = jnp.max(work, axis=-1, keepdims=True)
        ik = jnp.min(jnp.where(work == mk, lane, float(N_EXPERTS)), axis=-1, keepdims=True)
        sel = lane == ik
        work = jnp.where(sel, -jnp.inf, work)
        chosen = jnp.where(sel, 1.0, chosen)
        vals.append(mk)
        idxs.append(ik)
    es = [jnp.exp(v - vals[0]) for v in vals]
    den = es[0] + es[1] + es[2] + es[3]

    ri = lax.broadcasted_iota(jnp.int32, (tm, tm), 0)
    ci = lax.broadcasted_iota(jnp.int32, (tm, tm), 1)
    before = jnp.where(ci < ri, 1.0, 0.0).astype(BF16)
    prefix = jnp.dot(before, chosen.astype(BF16), preferred_element_type=F32) + carry_ref[...]
    carry_ref[...] = carry_ref[...] + jnp.sum(chosen, axis=0, keepdims=True)
    cnt_ref[...] = carry_ref[...]

    olane = lax.broadcasted_iota(jnp.int32, (tm, LANES), 1)
    route = jnp.zeros((tm, LANES), F32)
    for k in range(TOP_K):
        rank_k = jnp.sum(jnp.where(lane == idxs[k], prefix, 0.0), axis=-1, keepdims=True)
        route = jnp.where(olane == k, idxs[k], route)
        route = jnp.where(olane == TOP_K + k, es[k] / den, route)
        route = jnp.where(olane == 2 * TOP_K + k, rank_k, route)
    route_ref[...] = route


def _outproj_router(att, ssd, x2d, w_out_bf16, ln_g, ln_b, w_router, b_router, alpha):
    m, d = x2d.shape
    tm = min(256, m)
    row = lambda i: (i, 0)
    const = lambda i: (0, 0)
    kern = functools.partial(_outproj_kernel, alpha=alpha)
    return pl.pallas_call(
        kern,
        grid=(m // tm,),
        in_specs=[pl.BlockSpec((tm, ATT_WIDTH), row), pl.BlockSpec((tm, SSD_WIDTH), row), pl.BlockSpec((tm, d), row),
                  pl.BlockSpec((ATT_WIDTH + SSD_WIDTH, d), const, pipeline_mode=pl.Buffered(1)),
                  pl.BlockSpec((1, d), const), pl.BlockSpec((1, d), const),
                  pl.BlockSpec((d, N_EXPERTS), const), pl.BlockSpec((1, N_EXPERTS), const)],
        out_specs=[pl.BlockSpec((tm, d), row), pl.BlockSpec((tm, LANES), row), pl.BlockSpec((1, N_EXPERTS), const)],
        out_shape=[jax.ShapeDtypeStruct((m, d), F32), jax.ShapeDtypeStruct((m, LANES), F32),
                   jax.ShapeDtypeStruct((1, N_EXPERTS), F32)],
        scratch_shapes=[pltpu.VMEM((1, N_EXPERTS), F32)],
        compiler_params=_params(("arbitrary",)),
    )(att, ssd, x2d, w_out_bf16, ln_g.reshape(1, d), ln_b.reshape(1, d), w_router, b_router.reshape(1, -1))


def _row_copy(src, dst, sem):
    return pltpu.make_async_copy(src, dst, sem)


def _scatter_kernel(cnt_ref, pst_ref, dest_ref, h_ref, *rest, first):
    xrows_ref, zero_ref, sem, zsem = rest[-4:]
    i = pl.program_id(0)
    tm = h_ref.shape[0]

    @pl.when((i == 0) & first)
    def _():
        zero_ref[...] = jnp.zeros_like(zero_ref)

        def per_expert(e, _):
            n = cnt_ref[e]
            base = pst_ref[e]
            end = (n + MOE_SUB - 1) // MOE_SUB * MOE_SUB

            def start(r, _):
                _row_copy(zero_ref.at[pl.ds(0, 1)], xrows_ref.at[pl.ds(base + r, 1)], zsem).start()
                return 0

            def wait(r, _):
                _row_copy(zero_ref.at[pl.ds(0, 1)], xrows_ref.at[pl.ds(base + r, 1)], zsem).wait()
                return 0

            lax.fori_loop(n, end, start, 0)
            lax.fori_loop(n, end, wait, 0)
            return 0

        lax.fori_loop(0, N_EXPERTS, per_expert, 0)

    def start(t, _):
        for k in range(TOP_K):
            d = dest_ref[0, 0, t * TOP_K + k]
            _row_copy(h_ref.at[pl.ds(t, 1)], xrows_ref.at[pl.ds(d, 1)], sem).start()
        return 0

    def wait(t, _):
        for k in range(TOP_K):
            _row_copy(h_ref.at[pl.ds(t, 1)], xrows_ref.at[pl.ds(0, 1)], sem).wait()
        return 0

    for t in range(tm):
        start(t, 0)
    for t in range(tm):
        wait(t, 0)


def _moe_scatter(h2d, dest, counts, pstart, n_rows, x_rows=None):
    m, d = h2d.shape
    tm = min(256, m)
    nt = m // tm
    dest3 = dest.reshape(nt, 1, tm * TOP_K)
    first = x_rows is None
    in_specs = [pl.BlockSpec((1, 1, tm * TOP_K), lambda i, c, p: (i, 0, 0), memory_space=pltpu.SMEM),
                pl.BlockSpec((tm, d), lambda i, c, p: (i, 0))]
    args = [counts, pstart, dest3, h2d]
    aliases = {}
    if not first:
        in_specs.append(pl.BlockSpec(memory_space=pl.ANY))
        args.append(x_rows)
        aliases = {len(args) - 1: 0}
    grid_spec = pltpu.PrefetchScalarGridSpec(
        num_scalar_prefetch=2,
        grid=(nt,),
        in_specs=in_specs,
        out_specs=pl.BlockSpec(memory_space=pl.ANY),
        scratch_shapes=[pltpu.VMEM((8, d), F32), pltpu.SemaphoreType.DMA(()), pltpu.SemaphoreType.DMA(())],
    )
    return pl.pallas_call(
        functools.partial(_scatter_kernel, first=first),
        grid_spec=grid_spec,
        out_shape=jax.ShapeDtypeStruct((n_rows, d), F32),
        input_output_aliases=aliases,
        compiler_params=_params(("arbitrary",)),
    )(*args)


def _moe_mlp_kernel(ie_ref, ib_ref, iv_ref, x_ref, w1g_ref, w1l_ref, b1g_ref, b1l_ref, w2_ref, b2_ref, o_ref,
                    xb_ref, wg_ref, wl_ref, w2b_ref):
    i = pl.program_id(0)
    j = pl.program_id(1)
    nvalid = iv_ref[i]
    nsub = (nvalid + MOE_SUB - 1) // MOE_SUB
    d = x_ref.shape[1]

    def rows(s):
        return pl.ds(pl.multiple_of(s * MOE_SUB, MOE_SUB), MOE_SUB)

    @pl.when(nvalid > 0)
    def _():
        @pl.when(j == 0)
        def _():
            def init(s, _):
                xb_ref[rows(s), :] = x_ref[rows(s), :].astype(BF16)
                o_ref[rows(s), :] = jnp.broadcast_to(b2_ref[0], (MOE_SUB, d))
                return 0

            lax.fori_loop(0, nsub, init, 0)

        def hidden(s):
            xb = xb_ref[rows(s), :]
            return (jnp.dot(xb, wg_ref[...], preferred_element_type=F32),
                    jnp.dot(xb, wl_ref[...], preferred_element_type=F32))

        def finish(s, hid):
            glu = jnp.minimum(hid[0] + b1g_ref[0], SWIGLU_LIMIT)
            lin = jnp.clip(hid[1] + b1l_ref[0], -SWIGLU_LIMIT, SWIGLU_LIMIT)
            act = glu * _sigmoid(SWIGLU_ALPHA * glu) * (lin + 1.0)
            o_ref[rows(s), :] += jnp.dot(act.astype(BF16), w2b_ref[...], preferred_element_type=F32)

        def step(s, hid):
            nxt = hidden(s + 1)
            finish(s, hid)
            return nxt

        xb0 = xb_ref[rows(0), :]
        wg_ref[...] = w1g_ref[0].astype(BF16)
        glu0 = jnp.dot(xb0, wg_ref[...], preferred_element_type=F32)
        wl_ref[...] = w1l_ref[0].astype(BF16)
        lin0 = jnp.dot(xb0, wl_ref[...], preferred_element_type=F32)
        w2b_ref[...] = w2_ref[0].astype(BF16)
        hid = (glu0, lin0)
        hid = lax.fori_loop(0, nsub - 1, step, hid)
        finish(nsub - 1, hid)


def _moe_mlp(x_rows, item_e, item_blk, item_valid, w1, b1, w2, b2):
    n_rows, d = x_rows.shape
    n_items = item_e.shape[0]
    d_ff = w2.shape[1]
    tf = MOE_F_TILE
    nf = d_ff // tf
    tmr = MOE_ROW_TILE

    def jj(i, j, iv):
        return jnp.where(iv[i] > 0, j, nf - 1)

    grid_spec = pltpu.PrefetchScalarGridSpec(
        num_scalar_prefetch=3,
        grid=(n_items, nf),
        in_specs=[pl.BlockSpec((tmr, d), lambda i, j, ie, ib, iv: (ib[i], 0)),
                  pl.BlockSpec((1, d, tf), lambda i, j, ie, ib, iv: (ie[i], 0, jj(i, j, iv))),
                  pl.BlockSpec((1, d, tf), lambda i, j, ie, ib, iv: (ie[i], 0, nf + jj(i, j, iv))),
                  pl.BlockSpec((1, 1, tf), lambda i, j, ie, ib, iv: (ie[i], 0, jj(i, j, iv))),
                  pl.BlockSpec((1, 1, tf), lambda i, j, ie, ib, iv: (ie[i], 0, nf + jj(i, j, iv))),
                  pl.BlockSpec((1, tf, d), lambda i, j, ie, ib, iv: (ie[i], jj(i, j, iv), 0)),
                  pl.BlockSpec((1, 1, d), lambda i, j, ie, ib, iv: (ie[i], 0, 0))],
        out_specs=pl.BlockSpec((tmr, d), lambda i, j, ie, ib, iv: (ib[i], 0)),
        scratch_shapes=[pltpu.VMEM((tmr, d), BF16), pltpu.VMEM((d, tf), BF16), pltpu.VMEM((d, tf), BF16),
                        pltpu.VMEM((tf, d), BF16)],
    )
    return pl.pallas_call(
        _moe_mlp_kernel,
        grid_spec=grid_spec,
        out_shape=jax.ShapeDtypeStruct((n_rows, d), F32),
        compiler_params=_params(("arbitrary", "arbitrary")),
    )(item_e, item_blk, item_valid, x_rows, w1, w1, b1.reshape(N_EXPERTS, 1, -1), b1.reshape(N_EXPERTS, 1, -1),
      w2, b2.reshape(N_EXPERTS, 1, -1))


def _combine_kernel(dest_ref, nxt_ref, h_ref, route_ref, yrows_ref, g_ref, b_ref, o_ref, buf_ref, sems, *, alpha):
    i = pl.program_id(0)
    nt = pl.num_programs(0)
    tm = h_ref.shape[0]
    slot = i % 2

    def gather(idx_ref, s):
        for t in range(tm):
            for k in range(TOP_K):
                d = idx_ref[0, 0, t * TOP_K + k]
                _row_copy(yrows_ref.at[pl.ds(d, 1)], buf_ref.at[s, k, pl.ds(t, 1)], sems.at[s]).start()

    @pl.when(i == 0)
    def _():
        gather(dest_ref, 0)

    @pl.when(i + 1 < nt)
    def _():
        gather(nxt_ref, 1 - slot)

    for t in range(tm):
        for k in range(TOP_K):
            _row_copy(yrows_ref.at[pl.ds(0, 1)], buf_ref.at[slot, k, pl.ds(t, 1)], sems.at[slot]).wait()
    route = route_ref[...]
    acc = route[:, TOP_K:TOP_K + 1] * buf_ref[slot, 0]
    for k in range(1, TOP_K):
        acc = acc + route[:, TOP_K + k:TOP_K + k + 1] * buf_ref[slot, k]
    o_ref[...] = _layer_norm(alpha * h_ref[...] + acc, g_ref[...], b_ref[...])


def _moe_combine(h2d, route, dest, y_rows, ln_g, ln_b, alpha):
    m, d = h2d.shape
    tm = min(128, m)
    nt = m // tm
    dest3 = dest.reshape(nt, 1, tm * TOP_K)
    kern = functools.partial(_combine_kernel, alpha=alpha)
    idx_block = (1, 1, tm * TOP_K)
    return pl.pallas_call(
        kern,
        grid=(nt,),
        in_specs=[pl.BlockSpec(idx_block, lambda i: (i, 0, 0), memory_space=pltpu.SMEM),
                  pl.BlockSpec(idx_block, lambda i: (jnp.minimum(i + 1, nt - 1), 0, 0), memory_space=pltpu.SMEM),
                  pl.BlockSpec((tm, d), lambda i: (i, 0)),
                  pl.BlockSpec((tm, LANES), lambda i: (i, 0)),
                  pl.BlockSpec(memory_space=pl.ANY),
                  pl.BlockSpec((1, d), lambda i: (0, 0)), pl.BlockSpec((1, d), lambda i: (0, 0))],
        out_specs=pl.BlockSpec((tm, d), lambda i: (i, 0)),
        out_shape=jax.ShapeDtypeStruct((m, d), F32),
        scratch_shapes=[pltpu.VMEM((2, TOP_K, tm, d), F32), pltpu.SemaphoreType.DMA((2,))],
        compiler_params=_params(("arbitrary",)),
    )(dest3, dest3, h2d, route, y_rows, ln_g.reshape(1, d), ln_b.reshape(1, d))


def _moe_ln2(groups, w1, b1, w2, b2, ln_g, ln_b, alpha):
    tmr = MOE_ROW_TILE
    group_counts = [c.reshape(-1).astype(jnp.int32) for _, _, c in groups]
    counts = sum(group_counts)
    tiles = (counts + tmr - 1) // tmr
    tile_end = jnp.cumsum(tiles)
    tile_start = tile_end - tiles
    pstart = (tile_start * tmr).astype(jnp.int32)
    n_tok = sum(h.shape[0] for h, _, _ in groups)
    n_items = -(-(n_tok * TOP_K) // tmr) + N_EXPERTS
    n_rows = n_items * tmr
    it = jnp.arange(n_items, dtype=jnp.int32)
    total = tile_end[-1]
    it_c = jnp.minimum(it, total - 1)
    item_e = jnp.minimum(jnp.searchsorted(tile_end, it_c, side='right'), N_EXPERTS - 1).astype(jnp.int32)
    item_r = it_c - tile_start[item_e]
    item_blk = (tile_start[item_e] + item_r).astype(jnp.int32)
    item_valid = jnp.where(it < total, jnp.clip(counts[item_e] - item_r * tmr, 0, tmr), 0).astype(jnp.int32)

    dests = []
    earlier = jnp.zeros_like(counts)
    x_rows = None
    for (h2d, route, _), gc in zip(groups, group_counts):
        idx = route[:, 0:TOP_K].astype(jnp.int32)
        rank = route[:, 2 * TOP_K:3 * TOP_K].astype(jnp.int32)
        dest = (pstart[idx] + earlier[idx] + rank).reshape(-1)
        dests.append(dest)
        earlier = earlier + gc
        x_rows = _moe_scatter(h2d, dest, counts, pstart, n_rows, x_rows)
    y_rows = _moe_mlp(x_rows, item_e, item_blk, item_valid, w1, b1, w2, b2)
    return [_moe_combine(h2d, route, dest, y_rows, ln_g, ln_b, alpha)
            for (h2d, route, _), dest in zip(groups, dests)]


def kernel(x_prompt, x_sample, cache_k, cache_v, page_table, state_ssm, state_conv, w_in, w_out, lambda_qk,
           attn_subln_w, conv_w, conv_b, dt_bias, a_log, d_skip, ssd_norm_w, ln1_g, ln1_b, w_router, b_router,
           w_mlp1, b_mlp1, w_mlp2, b_mlp2, ln2_g, ln2_b):
    bp, sp, d = x_prompt.shape
    bd, sd, _ = x_sample.shape
    depth = w_in.shape[0]
    assert depth == 1 and sd == 1, "kernel supports the single-layer, single-token-decode configuration"
    alpha = (2.0 * depth) ** 0.25
    slopes = 2.0 ** (-8.0 * jnp.arange(1, ATT_HEADS + 1, dtype=F32) / ATT_HEADS)
    l = 0
    lam_init = 0.8 - 0.6 * math.exp(-0.3 * l)

    w_in_b = _cast_bf16(w_in[l])
    w_out_b = _cast_bf16(w_out[l])
    moe_w = (w_mlp1[l], b_mlp1[l], w_mlp2[l], b_mlp2[l], ln2_g[l], ln2_b[l])
    ssd_w = (conv_w[l], conv_b[l], dt_bias[l], a_log[l], d_skip[l], ssd_norm_w[l])

    xp = x_prompt.reshape(bp * sp, d)
    q, k, v, z, xbc, dt = _inproj(xp, w_in_b, BF16)
    att = _attn_prompt(q.reshape(bp, sp, -1), k.reshape(bp, sp, -1), v.reshape(bp, sp, -1), lambda_qk[l],
                       attn_subln_w[l], slopes, lam_init)
    ssm_zero = jnp.zeros((bp, SSD_HEADS, SSD_HEADDIM, D_STATE), F32)
    conv_zero = jnp.zeros((bp, CONV_W - 1, CONV_DIM), F32)
    ssd, ssm_p, conv_p = _ssd_prompt(xbc.reshape(bp, sp, -1), dt.reshape(bp, sp, -1), z.reshape(bp, sp, -1),
                                     ssm_zero, conv_zero, *ssd_w)
    h, route, counts = _outproj_router(att.reshape(bp * sp, -1), ssd.reshape(bp * sp, -1), xp, w_out_b,
                                       ln1_g[l], ln1_b[l], w_router[l], b_router[l], alpha)

    xs = x_sample.reshape(bd, d)
    qs, ks, vs, zs, xbcs, dts = _inproj(xs, w_in_b, F32)
    att_s = _attn_decode(qs, ks, vs, cache_k, cache_v, page_table, lambda_qk[l], attn_subln_w[l], slopes,
                         lam_init)
    ssd_s, ssm_s, conv_s = _ssd_sample(xbcs, dts, zs, state_ssm[l], state_conv[l], *ssd_w)
    hs, route_s, counts_s = _outproj_router(att_s, ssd_s, xs, w_out_b, ln1_g[l], ln1_b[l], w_router[l],
                                            b_router[l], alpha)
    y_prompt, y_sample = _moe_ln2([(h, route, counts), (hs, route_s, counts_s)], *moe_w, alpha)
    y_prompt = y_prompt.reshape(bp, sp, d)
    y_sample = y_sample.reshape(bd, sd, d)

    hshape = (ATT_HEADS, 2 * ATT_DH)
    return (y_prompt, y_sample,
            k.reshape(1, bp, sp, *hshape), v.reshape(1, bp, sp, *hshape), ssm_p[None], conv_p[None],
            ks.reshape(1, bd, sd, *hshape), vs.reshape(1, bd, sd, *hshape), ssm_s[None], conv_s[None])
```

```python
import functools
import math

import jax
import jax.numpy as jnp
from jax import lax
from jax.experimental import pallas as pl
from jax.experimental.pallas import tpu as pltpu

F32 = jnp.float32
BF16 = jnp.bfloat16

ATT_DH = 64
ATT_HEADS = 8
ATT_WIDTH = ATT_HEADS * 2 * ATT_DH
ATT_SCALE = ATT_DH ** -0.5
SSD_HEADDIM = 64
SSD_HEADS = 16
SSD_WIDTH = SSD_HEADS * SSD_HEADDIM
SSD_GROUPS = 2
D_STATE = 128
CONV_W = 4
CONV_DIM = SSD_WIDTH + 2 * SSD_GROUPS * D_STATE
SSD_CHUNK = 128
N_EXPERTS = 32
TOP_K = 4
SWIGLU_ALPHA = 1.702
SWIGLU_LIMIT = 7.0
LN_EPS = 1e-5
RMS_EPS = 1e-5
PAGE_SIZE = 128

V7X_VMEM_LIMIT_BYTES = 56 * 1024 * 1024
LANES = 128

MOE_ROW_TILE = 1024
MOE_SUB = 256
MOE_F_TILE = 256


def _params(sem, vmem=V7X_VMEM_LIMIT_BYTES):
    return pltpu.CompilerParams(dimension_semantics=sem, vmem_limit_bytes=vmem)


def _sigmoid(x):
    return 1.0 / (1.0 + jnp.exp(-x))


def _silu(x):
    return x * _sigmoid(x)


def _softplus(x):
    return jnp.maximum(x, 0.0) + jnp.log1p(jnp.exp(-jnp.abs(x)))


def _layer_norm(x, g, b):
    mu = jnp.mean(x, axis=-1, keepdims=True)
    xc = x - mu
    var = jnp.mean(xc * xc, axis=-1, keepdims=True)
    return xc * lax.rsqrt(var + LN_EPS) * g + b


def _split3(x):
    a = x.astype(BF16)
    r = x - a.astype(F32)
    b = r.astype(BF16)
    c = (r - b.astype(F32)).astype(BF16)
    return a, b, c


def _lam(lq):
    s01 = jnp.sum(lq[0:1, :] * lq[1:2, :], axis=-1, keepdims=True)
    s23 = jnp.sum(lq[2:3, :] * lq[3:4, :], axis=-1, keepdims=True)
    return jnp.exp(s01) - jnp.exp(s23)


def _cast_kernel(w_ref, o_ref):
    o_ref[...] = w_ref[...].astype(BF16)


def _cast_bf16(w, row_tile=256):
    r, c = w.shape
    rt = min(row_tile, r)
    return pl.pallas_call(
        _cast_kernel,
        grid=(r // rt,),
        in_specs=[pl.BlockSpec((rt, c), lambda i: (i, 0))],
        out_specs=pl.BlockSpec((rt, c), lambda i: (i, 0)),
        out_shape=jax.ShapeDtypeStruct((r, c), BF16),
        compiler_params=_params(("arbitrary",)),
    )(w)


def _inproj_kernel(x_ref, w_ref, q_ref, k_ref, v_ref, z_ref, xbc_ref, dt_ref):
    xb = x_ref[...].astype(BF16)

    def mm(c0, c1):
        return jnp.dot(xb, w_ref[:, c0:c1], preferred_element_type=F32)

    a = ATT_WIDTH
    q_ref[...] = (mm(0, a) * ATT_SCALE).astype(q_ref.dtype)
    k_ref[...] = mm(a, 2 * a)
    v_ref[...] = mm(2 * a, 3 * a)
    z_ref[...] = mm(3 * a, 3 * a + SSD_WIDTH)
    c0 = 3 * a + SSD_WIDTH
    xbc_ref[...] = mm(c0, c0 + CONV_DIM)
    dt_ref[...] = mm(c0 + CONV_DIM, c0 + CONV_DIM + SSD_HEADS)


def _inproj(x2d, w_in_bf16, q_dtype):
    m, d = x2d.shape
    ncol = w_in_bf16.shape[1]
    tm = min(256, m)
    row = lambda i: (i, 0)
    widths = (ATT_WIDTH, ATT_WIDTH, ATT_WIDTH, SSD_WIDTH, CONV_DIM, SSD_HEADS)
    dtypes = (q_dtype, F32, F32, F32, F32, F32)
    return pl.pallas_call(
        _inproj_kernel,
        grid=(m // tm,),
        in_specs=[pl.BlockSpec((tm, d), row),
                  pl.BlockSpec((d, ncol), lambda i: (0, 0), pipeline_mode=pl.Buffered(1))],
        out_specs=[pl.BlockSpec((tm, w), row) for w in widths],
        out_shape=[jax.ShapeDtypeStruct((m, w), dt) for w, dt in zip(widths, dtypes)],
        compiler_params=_params(("arbitrary",)),
    )(x2d, w_in_bf16)


def _attn_prompt_kernel(slopes_ref, qt_ref, k_ref, v_ref, lq_ref, w_ref, o_ref, kb_ref, vt_ref, *, tq, hps,
                        lam_init):
    hp = pl.program_id(1)
    qi = pl.program_id(2)
    hd = 2 * ATT_DH
    vrows = vt_ref.shape[1]

    @pl.when(qi == 0)
    def _():
        kb_ref[...] = k_ref[0].astype(BF16)
        for hh in range(hps):
            vt_ref[hh, 0:hd, :] = v_ref[0, :, hh * hd:(hh + 1) * hd].T.astype(BF16)
            vt_ref[hh, hd:vrows, :] = jnp.ones((vrows - hd, vt_ref.shape[2]), BF16)

    drow = lax.broadcasted_iota(jnp.int32, (hd, tq), 0)
    kr = lax.broadcasted_iota(jnp.int32, (tq, tq), 0)
    qc = lax.broadcasted_iota(jnp.int32, (tq, tq), 1)
    rel = (qc - kr).astype(F32)
    future = kr > qc
    slopes = [slopes_ref[hp * hps + hh] for hh in range(hps)]
    qts = []
    for hh in range(hps):
        qt = qt_ref[0, hh * hd:(hh + 1) * hd, :]
        zero = jnp.zeros_like(qt)
        qts += [jnp.where(drow < ATT_DH, qt, zero), jnp.where(drow >= ATT_DH, qt, zero)]

    def scores(j):
        kb = kb_ref[pl.ds(pl.multiple_of(j * tq, tq), tq), :]
        return tuple(jnp.dot(kb[:, (c // 2) * hd:(c // 2 + 1) * hd], qts[c], preferred_element_type=F32)
                     for c in range(2 * hps))

    def block(j, raw, stats, diag):
        cols = pl.ds(pl.multiple_of(j * tq, tq), tq)
        dist = rel + jnp.full((1, 1), (qi - j) * tq, jnp.int32).astype(F32)
        out = []
        for c in range(2 * hps):
            hh = c // 2
            m, l, a = stats[c]
            s = raw[c] - dist * slopes[hh]
            if diag:
                s = jnp.where(future, -jnp.inf, s)
            mn = jnp.maximum(m, jnp.max(s, axis=0, keepdims=True))
            p = jnp.exp(s - mn)
            al = jnp.exp(m - mn)
            pv = jnp.dot(vt_ref[hh, :, cols], p.astype(BF16), preferred_element_type=F32)
            l = al * l + pv[hd:hd + 1, :]
            a = al * a + pv[0:hd, :]
            out.append((mn, l, a))
        return tuple(out)

    def step(j, stats):
        return block(j, scores(j), stats, False)

    init1 = (jnp.full((1, tq), -1e30, F32), jnp.zeros((1, tq), F32), jnp.zeros((hd, tq), F32))
    stats = lax.fori_loop(0, qi, step, (init1,) * (2 * hps))
    stats = block(qi, scores(qi), stats, True)
    lam = _lam(lq_ref[...]) + lam_init
    for hh in range(hps):
        (_, l0, a0), (_, l1, a1) = stats[2 * hh], stats[2 * hh + 1]
        o = a0 / l0 - lam * (a1 / l1)
        o = o * lax.rsqrt(jnp.mean(o * o, axis=0, keepdims=True) + RMS_EPS) * w_ref[...] * (1.0 - lam_init)
        o_ref[0, :, hh * hd:(hh + 1) * hd] = o.T.astype(BF16)


def _attn_prompt(q, k, v, lambda_qk, subln_w, slopes, lam_init):
    b, t, _ = k.shape
    tq = min(256, t)
    hd = 2 * ATT_DH
    hps = 8
    ones_rows = 16
    qt = jnp.swapaxes(q, 1, 2)
    kern = functools.partial(_attn_prompt_kernel, tq=tq, hps=hps, lam_init=lam_init)
    return pl.pallas_call(
        kern,
        grid=(b, ATT_HEADS // hps, t // tq),
        in_specs=[pl.BlockSpec(memory_space=pltpu.SMEM),
                  pl.BlockSpec((1, hps * hd, tq), lambda bi, h, qi: (bi, h, qi)),
                  pl.BlockSpec((1, t, hps * hd), lambda bi, h, qi: (bi, 0, h)),
                  pl.BlockSpec((1, t, hps * hd), lambda bi, h, qi: (bi, 0, h)),
                  pl.BlockSpec((4, ATT_DH), lambda bi, h, qi: (0, 0)),
                  pl.BlockSpec((hd, 1), lambda bi, h, qi: (0, 0))],
        out_specs=pl.BlockSpec((1, tq, hps * hd), lambda bi, h, qi: (bi, qi, h)),
        out_shape=jax.ShapeDtypeStruct((b, t, ATT_WIDTH), BF16),
        scratch_shapes=[pltpu.VMEM((t, hps * hd), BF16), pltpu.VMEM((hps, hd + ones_rows, t), BF16)],
        compiler_params=_params(("arbitrary", "arbitrary", "arbitrary")),
    )(slopes, qt, k, v, lambda_qk, subln_w.reshape(hd, 1))


def _attn_decode_kernel(pt_ref, q_ref, kn_ref, vn_ref, slope_ref, lq_ref, w_ref, *rest, pps, past_len, lam_init):
    kp_refs, vp_refs = rest[:pps], rest[pps:2 * pps]
    o_ref, qt_ref, s_ref, a_ref, m_ref, snew_ref, anew_ref, acc_ref = rest[2 * pps:]
    ph = pl.program_id(1)
    p = pl.program_id(2)
    n_steps = pl.num_programs(2)
    nh = ATT_HEADS
    nrow = 2 * nh
    hd = 2 * ATT_DH
    plane = PAGE_SIZE * nh
    n_pages = past_len // PAGE_SIZE
    per_vreg = LANES // nh

    def page_lanes(page):
        return pl.ds(pl.multiple_of(page * plane, plane), plane)

    @pl.when((ph == 0) & (p == 0))
    def _():
        q8 = q_ref[0]
        lane = lax.broadcasted_iota(jnp.int32, (nh, hd), 1)
        qt = jnp.concatenate([jnp.where(lane < ATT_DH, q8, 0.0), jnp.where(lane >= ATT_DH, q8, 0.0)], axis=0)
        qt_ref[...] = qt.astype(BF16)
        kn = jnp.concatenate([kn_ref[0], kn_ref[0]], axis=0)
        s_new = jnp.sum(qt * kn, axis=-1, keepdims=True)
        snew_ref[...] = s_new
        m_ref[...] = s_new

    @pl.when(ph == 0)
    def _():
        lane = lax.broadcasted_iota(jnp.int32, (nrow, plane), 1)
        row = lax.broadcasted_iota(jnp.int32, (nrow, plane), 0)
        own_head = (lane % nh) == (row % nh)
        for i in range(pps):
            page = p * pps + i
            kflat = kp_refs[i][0].reshape(plane, hd).astype(BF16)
            s = lax.dot_general(qt_ref[...], kflat, (((1,), (1,)), ((), ())), preferred_element_type=F32)
            dist = (past_len - page * PAGE_SIZE - lane // nh).astype(F32)
            s = jnp.where(own_head, s - slope_ref[...] * dist, -jnp.inf)
            s_ref[:, page_lanes(page)] = s
            m_ref[...] = jnp.maximum(m_ref[...], jnp.max(s, axis=-1, keepdims=True))

    @pl.when((ph == 1) & (p == 0))
    def _():
        m = m_ref[...]
        e_new = jnp.exp(snew_ref[...] - m)

        def expsum(g, part):
            for u in range(pps):
                e = jnp.exp(s_ref[:, page_lanes(g * pps + u)] - m)
                s_ref[:, page_lanes(g * pps + u)] = e
                for c in range(plane // LANES):
                    part = part + e[:, c * LANES:(c + 1) * LANES]
            return part

        part = lax.fori_loop(0, n_pages // pps, expsum, jnp.zeros((nrow, LANES), F32))
        den = jnp.sum(part, axis=-1, keepdims=True) + e_new
        inv = 1.0 / den
        lam = _lam(lq_ref[...]) + lam_init

        def combine(g, _):
            for u in range(pps):
                pn = s_ref[:, page_lanes(g * pps + u)] * inv
                a_ref[:, page_lanes(g * pps + u)] = pn[0:nh] - lam * pn[nh:nrow]
            return 0

        lax.fori_loop(0, n_pages // pps, combine, 0)
        pn_new = e_new * inv
        anew_ref[...] = pn_new[0:nh] - lam * pn_new[nh:nrow]
        acc_ref[...] = jnp.zeros_like(acc_ref)

    @pl.when(ph == 1)
    def _():
        lane = lax.broadcasted_iota(jnp.int32, (nh, LANES), 1)
        n_acc = acc_ref.shape[0]
        for i in range(pps):
            page = p * pps + i

            accs = [acc_ref[k] for k in range(n_acc)]
            for g in range(plane // LANES):
                av = a_ref[:, pl.ds(pl.multiple_of(page * plane + g * LANES, LANES), LANES)]
                for jj in range(per_vreg):
                    sel = (lane >= jj * nh) & (lane < (jj + 1) * nh)
                    wcol = jnp.sum(jnp.where(sel, av, 0.0), axis=-1, keepdims=True)
                    accs[jj % n_acc] = accs[jj % n_acc] + wcol * vp_refs[i][0, g * per_vreg + jj]
            for k in range(n_acc):
                acc_ref[k] = accs[k]

    @pl.when((ph == 1) & (p == n_steps - 1))
    def _():
        o = anew_ref[...] * vn_ref[0]
        for k in range(acc_ref.shape[0]):
            o = o + acc_ref[k]
        o = o * lax.rsqrt(jnp.mean(o * o, axis=-1, keepdims=True) + RMS_EPS) * w_ref[...] * (1.0 - lam_init)
        o_ref[0] = o.astype(BF16)


def _attn_decode(q, k_new, v_new, cache_k, cache_v, page_table, lambda_qk, subln_w, slopes, lam_init):
    bd = q.shape[0]
    n_pages = page_table.shape[1]
    nh = ATT_HEADS
    nrow = 2 * nh
    hd = 2 * ATT_DH
    pps = max(c for c in (1, 2, 4, 8) if n_pages % c == 0)
    n_steps = n_pages // pps
    kp = cache_k.reshape(-1, PAGE_SIZE, nh, hd)
    vp = cache_v.reshape(-1, PAGE_SIZE, nh, hd)
    slope_rows = jnp.tile(slopes, 2).reshape(nrow, 1)
    past_len = n_pages * PAGE_SIZE
    kern = functools.partial(_attn_decode_kernel, pps=pps, past_len=past_len, lam_init=lam_init)
    head3 = lambda b, ph, p, pt: (b, 0, 0)
    const2 = lambda b, ph, p, pt: (0, 0)

    def kpage(i):
        return lambda b, ph, p, pt: (pt[b * n_pages + jnp.where(ph == 0, p, n_steps - 1) * pps + i], 0, 0, 0)

    def vpage(i):
        return lambda b, ph, p, pt: (pt[b * n_pages + jnp.where(ph == 0, 0, p) * pps + i], 0, 0, 0)

    page_block = (1, PAGE_SIZE, nh, hd)
    n_acc = 4
    grid_spec = pltpu.PrefetchScalarGridSpec(
        num_scalar_prefetch=1,
        grid=(bd, 2, n_steps),
        in_specs=[pl.BlockSpec((1, nh, hd), head3),
                  pl.BlockSpec((1, nh, hd), head3),
                  pl.BlockSpec((1, nh, hd), head3),
                  pl.BlockSpec((nrow, 1), const2),
                  pl.BlockSpec((4, ATT_DH), const2),
                  pl.BlockSpec((1, hd), const2)]
                 + [pl.BlockSpec(page_block, kpage(i)) for i in range(pps)]
                 + [pl.BlockSpec(page_block, vpage(i)) for i in range(pps)],
        out_specs=pl.BlockSpec((1, nh, hd), head3),
        scratch_shapes=[pltpu.VMEM((nrow, hd), BF16), pltpu.VMEM((nrow, past_len * nh), F32),
                        pltpu.VMEM((nh, past_len * nh), F32), pltpu.VMEM((nrow, 1), F32),
                        pltpu.VMEM((nrow, 1), F32), pltpu.VMEM((nh, 1), F32),
                        pltpu.VMEM((n_acc, nh, hd), F32)],
    )
    out = pl.pallas_call(
        kern,
        grid_spec=grid_spec,
        out_shape=jax.ShapeDtypeStruct((bd, nh, hd), BF16),
        compiler_params=_params(("arbitrary", "arbitrary", "arbitrary")),
    )(page_table.reshape(-1), q.reshape(bd, nh, hd), k_new.reshape(bd, nh, hd), v_new.reshape(bd, nh, hd),
      slope_rows, lambda_qk, subln_w.reshape(1, hd), *([kp] * pps), *([vp] * pps))
    return out.reshape(bd, ATT_WIDTH)


def _gated_group_norm(y, z, w):
    yg = y * _silu(z)
    gw = SSD_WIDTH // SSD_GROUPS
    parts = []
    for g in range(SSD_GROUPS):
        v = yg[:, g * gw:(g + 1) * gw]
        parts.append(v * lax.rsqrt(jnp.mean(v * v, axis=-1, keepdims=True) + RMS_EPS) * w[:, g * gw:(g + 1) * gw])
    return parts


def _ssd_prompt_kernel(xbc_ref, dt_ref, dtt_ref, z_ref, cw_ref, cb_ref, dtb_ref, dtbt_ref, al_ref, alt_ref,
                       dsk_ref, nw_ref, h0_ref, c0_ref, y_ref, st_ref, cv_ref, xpad_ref, ysc_ref, xdd_ref):
    c = pl.program_id(1)
    nc = pl.num_programs(1)
    L = SSD_CHUNK
    P = SSD_HEADDIM
    hpg = SSD_HEADS // SSD_GROUPS
    halo = 8

    @pl.when(c == 0)
    def _():
        st_ref[...] = h0_ref[...]
        xpad_ref[0:halo, :] = jnp.zeros((halo, CONV_DIM), F32)
        xpad_ref[halo - (CONV_W - 1):halo, :] = c0_ref[0]

    xc = xbc_ref[0]
    xpad_ref[halo:halo + L, :] = xc
    conv = cb_ref[...]
    for i in range(CONV_W - 1):
        sh = CONV_W - 1 - i
        conv = conv + xpad_ref[halo - sh:halo - sh + L, :] * cw_ref[i:i + 1, :]
    conv = conv + xc * cw_ref[CONV_W - 1:CONV_W, :]
    tail = xc[L - (CONV_W - 1):L, :]
    xpad_ref[halo - (CONV_W - 1):halo, :] = tail

    @pl.when(c == nc - 1)
    def _():
        cv_ref[0] = tail

    act = _silu(conv)
    xs = act[:, :SSD_WIDTH]
    bmat = [act[:, SSD_WIDTH + g * D_STATE:SSD_WIDTH + (g + 1) * D_STATE].astype(BF16) for g in range(SSD_GROUPS)]
    c_off = SSD_WIDTH + SSD_GROUPS * D_STATE
    cmat = [act[:, c_off + g * D_STATE:c_off + (g + 1) * D_STATE].astype(BF16) for g in range(SSD_GROUPS)]

    dtp = _softplus(dt_ref[0] + dtb_ref[...])
    dtpt = _softplus(dtt_ref[0] + dtbt_ref[...])
    da = dtp * (-jnp.exp(al_ref[...]))
    dat = dtpt * (-jnp.exp(alt_ref[...]))
    ri = lax.broadcasted_iota(jnp.int32, (L, L), 0)
    ci = lax.broadcasted_iota(jnp.int32, (L, L), 1)
    causal = ri >= ci
    tri = jnp.where(causal, 1.0, 0.0).astype(BF16)
    trit = jnp.where(ci >= ri, 1.0, 0.0).astype(BF16)
    cs = sum(jnp.dot(tri, part, preferred_element_type=F32) for part in _split3(da))
    cst = sum(jnp.dot(part, trit, preferred_element_type=F32) for part in _split3(dat))

    cb = [lax.dot_general(cmat[g], bmat[g], (((1,), (1,)), ((), ())), preferred_element_type=F32)
          for g in range(SSD_GROUPS)]
    dsk = dsk_ref[...]

    for h in range(SSD_HEADS):
        g = h // hpg
        cs_col = cs[:, h:h + 1]
        diff = cs_col - cst[h:h + 1, :]
        lmat = jnp.exp(jnp.where(causal, diff, -jnp.inf))
        mmat = (cb[g] * lmat).astype(BF16)
        xs_h = xs[:, h * P:(h + 1) * P]
        xd_h = xs_h * dtp[:, h:h + 1]
        y = jnp.dot(mmat, xd_h.astype(BF16), preferred_element_type=F32)
        st = st_ref[0, h]
        yoff = lax.dot_general(cmat[g], st.astype(BF16), (((1,), (1,)), ((), ())), preferred_element_type=F32)
        y = y + jnp.exp(cs_col) * yoff + dsk[:, h:h + 1] * xs_h
        ysc_ref[:, h * P:(h + 1) * P] = y
        cs_last = cs[L - 1:L, h:h + 1]
        xdd_ref[:, h * P:(h + 1) * P] = xd_h * jnp.exp(cs_last - cs_col)

    xddt = xdd_ref[...].T
    for h in range(SSD_HEADS):
        g = h // hpg
        new = jnp.dot(xddt[h * P:(h + 1) * P, :].astype(BF16), bmat[g], preferred_element_type=F32)
        cs_last = cs[L - 1:L, h:h + 1]
        st_ref[0, h] = jnp.exp(cs_last) * st_ref[0, h] + new

    parts = _gated_group_norm(ysc_ref[...], z_ref[0], nw_ref[...])
    gw = SSD_WIDTH // SSD_GROUPS
    for g in range(SSD_GROUPS):
        y_ref[0, :, g * gw:(g + 1) * gw] = parts[g].astype(BF16)


def _ssd_prompt(xbc, dt, z, h0, conv0, conv_w, conv_b, dt_bias, a_log, d_skip, norm_w):
    b, t, _ = xbc.shape
    L = SSD_CHUNK
    nc = t // L
    dtt = jnp.swapaxes(dt, 1, 2)
    seq = lambda bi, ci: (bi, ci, 0)
    const2 = lambda bi, ci: (0, 0)
    full2 = lambda shp: pl.BlockSpec(shp, const2)
    return pl.pallas_call(
        _ssd_prompt_kernel,
        grid=(b, nc),
        in_specs=[pl.BlockSpec((1, L, CONV_DIM), seq),
                  pl.BlockSpec((1, L, SSD_HEADS), seq),
                  pl.BlockSpec((1, SSD_HEADS, L), lambda bi, ci: (bi, 0, ci)),
                  pl.BlockSpec((1, L, SSD_WIDTH), seq),
                  full2((CONV_W, CONV_DIM)), full2((1, CONV_DIM)),
                  full2((1, SSD_HEADS)), full2((SSD_HEADS, 1)),
                  full2((1, SSD_HEADS)), full2((SSD_HEADS, 1)),
                  full2((1, SSD_HEADS)), full2((1, SSD_WIDTH)),
                  pl.BlockSpec((1, SSD_HEADS, SSD_HEADDIM, D_STATE), lambda bi, ci: (bi, 0, 0, 0)),
                  pl.BlockSpec((1, CONV_W - 1, CONV_DIM), lambda bi, ci: (bi, 0, 0))],
        out_specs=[pl.BlockSpec((1, L, SSD_WIDTH), seq),
                   pl.BlockSpec((1, SSD_HEADS, SSD_HEADDIM, D_STATE), lambda bi, ci: (bi, 0, 0, 0)),
                   pl.BlockSpec((1, CONV_W - 1, CONV_DIM), lambda bi, ci: (bi, 0, 0))],
        out_shape=[jax.ShapeDtypeStruct((b, t, SSD_WIDTH), BF16),
                   jax.ShapeDtypeStruct((b, SSD_HEADS, SSD_HEADDIM, D_STATE), F32),
                   jax.ShapeDtypeStruct((b, CONV_W - 1, CONV_DIM), F32)],
        scratch_shapes=[pltpu.VMEM((8 + L, CONV_DIM), F32), pltpu.VMEM((L, SSD_WIDTH), F32),
                        pltpu.VMEM((L, SSD_WIDTH), F32)],
        compiler_params=_params(("arbitrary", "arbitrary")),
    )(xbc, dt, dtt, z, conv_w, conv_b.reshape(1, -1), dt_bias.reshape(1, -1), dt_bias.reshape(-1, 1),
      a_log.reshape(1, -1), a_log.reshape(-1, 1), d_skip.reshape(1, -1), norm_w.reshape(1, -1), h0, conv0)


def _bf16_round(v):
    return v.astype(BF16).astype(F32)


def _ssd_sample_kernel(xbc_ref, ci_ref, dt_ref, z_ref, cw_ref, cb_ref, dtb_ref, al_ref, dsk_ref, nw_ref, h0_ref,
                       xbc8_ref, ci8_ref, dt8_ref, cw8_ref, cb8_ref, dtb8_ref, y_ref, st_ref, cv_ref, ysc_ref):
    P = SSD_HEADDIM
    hpg = SSD_HEADS // SSD_GROUPS
    xrow = xbc_ref[0]
    hist = ci_ref[0]
    conv = cb_ref[...]
    for i in range(CONV_W - 1):
        conv = conv + hist[i:i + 1, :] * cw_ref[i:i + 1, :]
    conv = conv + xrow * cw_ref[CONV_W - 1:CONV_W, :]
    cv_ref[0, 0:CONV_W - 2, :] = hist[1:CONV_W - 1, :]
    cv_ref[0, CONV_W - 2:CONV_W - 1, :] = xrow
    act = _silu(conv)
    xs = act[:, :SSD_WIDTH]
    c_off = SSD_WIDTH + SSD_GROUPS * D_STATE
    dtp = _softplus(dt_ref[0] + dtb_ref[...])
    decay = jnp.exp(dtp * (-jnp.exp(al_ref[...])))
    dsk = dsk_ref[...]

    nx = SSD_WIDTH // LANES
    conv8 = cb8_ref[0:nx, :]
    for i in range(CONV_W - 1):
        conv8 = conv8 + ci8_ref[0, i, 0:nx, :] * cw8_ref[i, 0:nx, :]
    conv8 = conv8 + xbc8_ref[0, 0:nx, :] * cw8_ref[CONV_W - 1, 0:nx, :]
    xd8 = _silu(conv8) * _softplus(dt8_ref[0] + dtb8_ref[...])
    xdt = xd8.T

    brows = [act[:, SSD_WIDTH + g * D_STATE:SSD_WIDTH + (g + 1) * D_STATE] for g in range(SSD_GROUPS)]
    crows = [act[:, c_off + g * D_STATE:c_off + (g + 1) * D_STATE] for g in range(SSD_GROUPS)]
    cbs = [jnp.sum(_bf16_round(brows[g]) * _bf16_round(crows[g]), axis=-1, keepdims=True) for g in range(SSD_GROUPS)]
    c8s = [jnp.broadcast_to(crows[g], (8, D_STATE)).astype(BF16) for g in range(SSD_GROUPS)]
    hpr = LANES // P
    for h in range(SSD_HEADS):
        g = h // hpg
        xs_h = xs[:, h * P:(h + 1) * P]
        xd_h = xs_h * dtp[:, h:h + 1]
        xcol = xdt[(h % hpr) * P:(h % hpr + 1) * P, h // hpr:h // hpr + 1]
        dec = decay[:, h:h + 1]
        h0q = h0_ref[0, h].astype(BF16)
        st_ref[0, h] = _bf16_round(dec) * h0q.astype(F32) + _bf16_round(xcol * brows[g])
        yoff = lax.dot_general(c8s[g], h0q, (((1,), (1,)), ((), ())), preferred_element_type=F32)[0:1, :]
        ysc_ref[:, h * P:(h + 1) * P] = (cbs[g] * xd_h + dec * yoff) + dsk[:, h:h + 1] * xs_h
    parts = _gated_group_norm(ysc_ref[...], z_ref[0], nw_ref[...])
    gw = SSD_WIDTH // SSD_GROUPS
    for g in range(SSD_GROUPS):
        y_ref[0, :, g * gw:(g + 1) * gw] = parts[g].astype(BF16)


def _ssd_sample(xbc, dt, z, h0, conv0, conv_w, conv_b, dt_bias, a_log, d_skip, norm_w):
    bd = xbc.shape[0]
    nr = CONV_DIM // LANES
    nx = SSD_WIDTH // LANES
    rep = SSD_HEADDIM
    row3 = lambda b: (b, 0, 0)
    const2 = lambda b: (0, 0)
    full2 = lambda shp: pl.BlockSpec(shp, const2)
    state = pl.BlockSpec((1, SSD_HEADS, SSD_HEADDIM, D_STATE), lambda b: (b, 0, 0, 0))
    y, st, cv = pl.pallas_call(
        _ssd_sample_kernel,
        grid=(bd,),
        in_specs=[pl.BlockSpec((1, 1, CONV_DIM), row3),
                  pl.BlockSpec((1, CONV_W - 1, CONV_DIM), row3),
                  pl.BlockSpec((1, 1, SSD_HEADS), row3),
                  pl.BlockSpec((1, 1, SSD_WIDTH), row3),
                  full2((CONV_W, CONV_DIM)), full2((1, CONV_DIM)), full2((1, SSD_HEADS)), full2((1, SSD_HEADS)),
                  full2((1, SSD_HEADS)), full2((1, SSD_WIDTH)), state,
                  pl.BlockSpec((1, nr, LANES), row3),
                  pl.BlockSpec((1, CONV_W - 1, nr, LANES), lambda b: (b, 0, 0, 0)),
                  pl.BlockSpec((1, nx, LANES), row3),
                  pl.BlockSpec((CONV_W, nr, LANES), lambda b: (0, 0, 0)),
                  full2((nr, LANES)), full2((nx, LANES))],
        out_specs=[pl.BlockSpec((1, 1, SSD_WIDTH), row3), state,
                   pl.BlockSpec((1, CONV_W - 1, CONV_DIM), row3)],
        out_shape=[jax.ShapeDtypeStruct((bd, 1, SSD_WIDTH), BF16),
                   jax.ShapeDtypeStruct((bd, SSD_HEADS, SSD_HEADDIM, D_STATE), F32),
                   jax.ShapeDtypeStruct((bd, CONV_W - 1, CONV_DIM), F32)],
        scratch_shapes=[pltpu.VMEM((1, SSD_WIDTH), F32)],
        compiler_params=_params(("arbitrary",)),
    )(xbc.reshape(bd, 1, CONV_DIM), conv0, dt.reshape(bd, 1, SSD_HEADS), z.reshape(bd, 1, SSD_WIDTH),
      conv_w, conv_b.reshape(1, -1), dt_bias.reshape(1, -1), a_log.reshape(1, -1), d_skip.reshape(1, -1),
      norm_w.reshape(1, -1), h0,
      xbc.reshape(bd, nr, LANES), conv0.reshape(bd, CONV_W - 1, nr, LANES),
      jnp.repeat(dt, rep, axis=-1).reshape(bd, nx, LANES), conv_w.reshape(CONV_W, nr, LANES),
      conv_b.reshape(nr, LANES), jnp.repeat(dt_bias, rep).reshape(nx, LANES))
    return y.reshape(bd, SSD_WIDTH), st, cv


def _outproj_kernel(att_ref, ssd_ref, x_ref, w_ref, g_ref, b_ref, wr_ref, br_ref, h_ref, route_ref, cnt_ref,
                    carry_ref, *, alpha):
    i = pl.program_id(0)
    tm = x_ref.shape[0]

    @pl.when(i == 0)
    def _():
        carry_ref[...] = jnp.zeros_like(carry_ref)

    mix = jnp.dot(att_ref[...], w_ref[0:ATT_WIDTH, :], preferred_element_type=F32)
    mix = mix + jnp.dot(ssd_ref[...], w_ref[ATT_WIDTH:ATT_WIDTH + SSD_WIDTH, :], preferred_element_type=F32)
    hval = _layer_norm(alpha * x_ref[...] + mix, g_ref[...], b_ref[...])
    h_ref[...] = hval

    logits = jnp.dot(hval.astype(BF16), wr_ref[...].astype(BF16), preferred_element_type=F32) + br_ref[...]

    lane = lax.broadcasted_iota(jnp.int32, (tm, N_EXPERTS), 1).astype(F32)
    work = logits
    chosen = jnp.zeros((tm, N_EXPERTS), F32)
    vals, idxs = [], []
    for _ in range(TOP_K):
        mk = jnp.max(work, axis=-1, keepdims=True)
        ik = jnp.min(jnp.where(work == mk, lane, float(N_EXPERTS)), axis=-1, keepdims=True)
        sel = lane == ik
        work = jnp.where(sel, -jnp.inf, work)
        chosen = jnp.where(sel, 1.0, chosen)
        vals.append(mk)
        idxs.append(ik)
    es = [jnp.exp(v - vals[0]) for v in vals]
    den = es[0] + es[1] + es[2] + es[3]

    ri = lax.broadcasted_iota(jnp.int32, (tm, tm), 0)
    ci = lax.broadcasted_iota(jnp.int32, (tm, tm), 1)
    before = jnp.where(ci < ri, 1.0, 0.0).astype(BF16)
    prefix = jnp.dot(before, chosen.astype(BF16), preferred_element_type=F32) + carry_ref[...]
    carry_ref[...] = carry_ref[...] + jnp.sum(chosen, axis=0, keepdims=True)
    cnt_ref[...] = carry_ref[...]

    olane = lax.broadcasted_iota(jnp.int32, (tm, LANES), 1)
    route = jnp.zeros((tm, LANES), F32)
    for k in range(TOP_K):
        rank_k = jnp.sum(jnp.where(lane == idxs[k], prefix, 0.0), axis=-1, keepdims=True)
        route = jnp.where(olane == k, idxs[k], route)
        route = jnp.where(olane == TOP_K + k, es[k] / den, route)
        route = jnp.where(olane == 2 * TOP_K + k, rank_k, route)
    route_ref[...] = route


def _outproj_router(att, ssd, x2d, w_out_bf16, ln_g, ln_b, w_router, b_router, alpha):
    m, d = x2d.shape
    tm = min(256, m)
    row = lambda i: (i, 0)
    const = lambda i: (0, 0)
    kern = functools.partial(_outproj_kernel, alpha=alpha)
    return pl.pallas_call(
        kern,
        grid=(m // tm,),
        in_specs=[pl.BlockSpec((tm, ATT_WIDTH), row), pl.BlockSpec((tm, SSD_WIDTH), row), pl.BlockSpec((tm, d), row),
                  pl.BlockSpec((ATT_WIDTH + SSD_WIDTH, d), const, pipeline_mode=pl.Buffered(1)),
                  pl.BlockSpec((1, d), const), pl.BlockSpec((1, d), const),
                  pl.BlockSpec((d, N_EXPERTS), const), pl.BlockSpec((1, N_EXPERTS), const)],
        out_specs=[pl.BlockSpec((tm, d), row), pl.BlockSpec((tm, LANES), row), pl.BlockSpec((1, N_EXPERTS), const)],
        out_shape=[jax.ShapeDtypeStruct((m, d), F32), jax.ShapeDtypeStruct((m, LANES), F32),
                   jax.ShapeDtypeStruct((1, N_EXPERTS), F32)],
        scratch_shapes=[pltpu.VMEM((1, N_EXPERTS), F32)],
        compiler_params=_params(("arbitrary",)),
    )(att, ssd, x2d, w_out_bf16, ln_g.reshape(1, d), ln_b.reshape(1, d), w_router, b_router.reshape(1, -1))


def _row_copy(src, dst, sem):
    return pltpu.make_async_copy(src, dst, sem)


def _scatter_kernel(cnt_ref, pst_ref, dest_ref, h_ref, *rest, first):
    xrows_ref, zero_ref, sem, zsem = rest[-4:]
    i = pl.program_id(0)
    tm = h_ref.shape[0]

    @pl.when((i == 0) & first)
    def _():
        zero_ref[...] = jnp.zeros_like(zero_ref)

        def per_expert(e, _):
            n = cnt_ref[e]
            base = pst_ref[e]
            end = (n + MOE_SUB - 1) // MOE_SUB * MOE_SUB

            def start(r, _):
                _row_copy(zero_ref.at[pl.ds(0, 1)], xrows_ref.at[pl.ds(base + r, 1)], zsem).start()
                return 0

            def wait(r, _):
                _row_copy(zero_ref.at[pl.ds(0, 1)], xrows_ref.at[pl.ds(base + r, 1)], zsem).wait()
                return 0

            lax.fori_loop(n, end, start, 0)
            lax.fori_loop(n, end, wait, 0)
            return 0

        lax.fori_loop(0, N_EXPERTS, per_expert, 0)

    def start(t, _):
        for k in range(TOP_K):
            d = dest_ref[0, 0, t * TOP_K + k]
            _row_copy(h_ref.at[pl.ds(t, 1)], xrows_ref.at[pl.ds(d, 1)], sem).start()
        return 0

    def wait(t, _):
        for k in range(TOP_K):
            _row_copy(h_ref.at[pl.ds(t, 1)], xrows_ref.at[pl.ds(0, 1)], sem).wait()
        return 0

    for t in range(tm):
        start(t, 0)
    for t in range(tm):
        wait(t, 0)


def _moe_scatter(h2d, dest, counts, pstart, n_rows, x_rows=None):
    m, d = h2d.shape
    tm = min(256, m)
    nt = m // tm
    dest3 = dest.reshape(nt, 1, tm * TOP_K)
    first = x_rows is None
    in_specs = [pl.BlockSpec((1, 1, tm * TOP_K), lambda i, c, p: (i, 0, 0), memory_space=pltpu.SMEM),
                pl.BlockSpec((tm, d), lambda i, c, p: (i, 0))]
    args = [counts, pstart, dest3, h2d]
    aliases = {}
    if not first:
        in_specs.append(pl.BlockSpec(memory_space=pl.ANY))
        args.append(x_rows)
        aliases = {len(args) - 1: 0}
    grid_spec = pltpu.PrefetchScalarGridSpec(
        num_scalar_prefetch=2,
        grid=(nt,),
        in_specs=in_specs,
        out_specs=pl.BlockSpec(memory_space=pl.ANY),
        scratch_shapes=[pltpu.VMEM((8, d), F32), pltpu.SemaphoreType.DMA(()), pltpu.SemaphoreType.DMA(())],
    )
    return pl.pallas_call(
        functools.partial(_scatter_kernel, first=first),
        grid_spec=grid_spec,
        out_shape=jax.ShapeDtypeStruct((n_rows, d), F32),
        input_output_aliases=aliases,
        compiler_params=_params(("arbitrary",)),
    )(*args)


def _moe_mlp_kernel(ie_ref, ib_ref, iv_ref, x_ref, w1g_ref, w1l_ref, b1g_ref, b1l_ref, w2_ref, b2_ref, o_ref,
                    xb_ref, wg_ref, wl_ref, w2b_ref):
    i = pl.program_id(0)
    j = pl.program_id(1)
    nvalid = iv_ref[i]
    nsub = (nvalid + MOE_SUB - 1) // MOE_SUB
    d = x_ref.shape[1]

    def rows(s):
        return pl.ds(pl.multiple_of(s * MOE_SUB, MOE_SUB), MOE_SUB)

    @pl.when(nvalid > 0)
    def _():
        @pl.when(j == 0)
        def _():
            def init(s, _):
                xb_ref[rows(s), :] = x_ref[rows(s), :].astype(BF16)
                o_ref[rows(s), :] = jnp.broadcast_to(b2_ref[0], (MOE_SUB, d))
                return 0

            lax.fori_loop(0, nsub, init, 0)

        def hidden(s):
            xb = xb_ref[rows(s), :]
            return (jnp.dot(xb, wg_ref[...], preferred_element_type=F32),
                    jnp.dot(xb, wl_ref[...], preferred_element_type=F32))

        def finish(s, hid):
            glu = jnp.minimum(hid[0] + b1g_ref[0], SWIGLU_LIMIT)
            lin = jnp.clip(hid[1] + b1l_ref[0], -SWIGLU_LIMIT, SWIGLU_LIMIT)
            act = glu * _sigmoid(SWIGLU_ALPHA * glu) * (lin + 1.0)
            o_ref[rows(s), :] += jnp.dot(act.astype(BF16), w2b_ref[...], preferred_element_type=F32)

        def step(s, hid):
            nxt = hidden(s + 1)
            finish(s, hid)
            return nxt

        xb0 = xb_ref[rows(0), :]
        wg_ref[...] = w1g_ref[0].astype(BF16)
        glu0 = jnp.dot(xb0, wg_ref[...], preferred_element_type=F32)
        wl_ref[...] = w1l_ref[0].astype(BF16)
        lin0 = jnp.dot(xb0, wl_ref[...], preferred_element_type=F32)
        w2b_ref[...] = w2_ref[0].astype(BF16)
        hid = (glu0, lin0)
        hid = lax.fori_loop(0, nsub - 1, step, hid)
        finish(nsub - 1, hid)


def _moe_mlp(x_rows, item_e, item_blk, item_valid, w1, b1, w2, b2):
    n_rows, d = x_rows.shape
    n_items = item_e.shape[0]
    d_ff = w2.shape[1]
    tf = MOE_F_TILE
    nf = d_ff // tf
    tmr = MOE_ROW_TILE

    def jj(i, j, iv):
        return jnp.where(iv[i] > 0, j, nf - 1)

    grid_spec = pltpu.PrefetchScalarGridSpec(
        num_scalar_prefetch=3,
        grid=(n_items, nf),
        in_specs=[pl.BlockSpec((tmr, d), lambda i, j, ie, ib, iv: (ib[i], 0)),
                  pl.BlockSpec((1, d, tf), lambda i, j, ie, ib, iv: (ie[i], 0, jj(i, j, iv))),
                  pl.BlockSpec((1, d, tf), lambda i, j, ie, ib, iv: (ie[i], 0, nf + jj(i, j, iv))),
                  pl.BlockSpec((1, 1, tf), lambda i, j, ie, ib, iv: (ie[i], 0, jj(i, j, iv))),
                  pl.BlockSpec((1, 1, tf), lambda i, j, ie, ib, iv: (ie[i], 0, nf + jj(i, j, iv))),
                  pl.BlockSpec((1, tf, d), lambda i, j, ie, ib, iv: (ie[i], jj(i, j, iv), 0)),
                  pl.BlockSpec((1, 1, d), lambda i, j, ie, ib, iv: (ie[i], 0, 0))],
        out_specs=pl.BlockSpec((tmr, d), lambda i, j, ie, ib, iv: (ib[i], 0)),
        scratch_shapes=[pltpu.VMEM((tmr, d), BF16), pltpu.VMEM((d, tf), BF16), pltpu.VMEM((d, tf), BF16),
                        pltpu.VMEM((tf, d), BF16)],
    )
    return pl.pallas_call(
        _moe_mlp_kernel,
        grid_spec=grid_spec,
        out_shape=jax.ShapeDtypeStruct((n_rows, d), F32),
        compiler_params=_params(("arbitrary", "arbitrary")),
    )(item_e, item_blk, item_valid, x_rows, w1, w1, b1.reshape(N_EXPERTS, 1, -1), b1.reshape(N_EXPERTS, 1, -1),
      w2, b2.reshape(N_EXPERTS, 1, -1))


def _combine_kernel(dest_ref, nxt_ref, h_ref, route_ref, yrows_ref, g_ref, b_ref, o_ref, buf_ref, sems, *, alpha):
    i = pl.program_id(0)
    nt = pl.num_programs(0)
    tm = h_ref.shape[0]
    slot = i % 2

    def gather(idx_ref, s):
        for t in range(tm):
            for k in range(TOP_K):
                d = idx_ref[0, 0, t * TOP_K + k]
                _row_copy(yrows_ref.at[pl.ds(d, 1)], buf_ref.at[s, k, pl.ds(t, 1)], sems.at[s]).start()

    @pl.when(i == 0)
    def _():
        gather(dest_ref, 0)

    @pl.when(i + 1 < nt)
    def _():
        gather(nxt_ref, 1 - slot)

    for t in range(tm):
        for k in range(TOP_K):
            _row_copy(yrows_ref.at[pl.ds(0, 1)], buf_ref.at[slot, k, pl.ds(t, 1)], sems.at[slot]).wait()
    route = route_ref[...]
    acc = route[:, TOP_K:TOP_K + 1] * buf_ref[slot, 0]
    for k in range(1, TOP_K):
        acc = acc + route[:, TOP_K + k:TOP_K + k + 1] * buf_ref[slot, k]
    o_ref[...] = _layer_norm(alpha * h_ref[...] + acc, g_ref[...], b_ref[...])


def _moe_combine(h2d, route, dest, y_rows, ln_g, ln_b, alpha):
    m, d = h2d.shape
    tm = min(128, m)
    nt = m // tm
    dest3 = dest.reshape(nt, 1, tm * TOP_K)
    kern = functools.partial(_combine_kernel, alpha=alpha)
    idx_block = (1, 1, tm * TOP_K)
    return pl.pallas_call(
        kern,
        grid=(nt,),
        in_specs=[pl.BlockSpec(idx_block, lambda i: (i, 0, 0), memory_space=pltpu.SMEM),
                  pl.BlockSpec(idx_block, lambda i: (jnp.minimum(i + 1, nt - 1), 0, 0), memory_space=pltpu.SMEM),
                  pl.BlockSpec((tm, d), lambda i: (i, 0)),
                  pl.BlockSpec((tm, LANES), lambda i: (i, 0)),
                  pl.BlockSpec(memory_space=pl.ANY),
                  pl.BlockSpec((1, d), lambda i: (0, 0)), pl.BlockSpec((1, d), lambda i: (0, 0))],
        out_specs=pl.BlockSpec((tm, d), lambda i: (i, 0)),
        out_shape=jax.ShapeDtypeStruct((m, d), F32),
        scratch_shapes=[pltpu.VMEM((2, TOP_K, tm, d), F32), pltpu.SemaphoreType.DMA((2,))],
        compiler_params=_params(("arbitrary",)),
    )(dest3, dest3, h2d, route, y_rows, ln_g.reshape(1, d), ln_b.reshape(1, d))


def _moe_ln2(groups, w1, b1, w2, b2, ln_g, ln_b, alpha):
    tmr = MOE_ROW_TILE
    group_counts = [c.reshape(-1).astype(jnp.int32) for _, _, c in groups]
    counts = sum(group_counts)
    tiles = (counts + tmr - 1) // tmr
    tile_end = jnp.cumsum(tiles)
    tile_start = tile_end - tiles
    pstart = (tile_start * tmr).astype(jnp.int32)
    n_tok = sum(h.shape[0] for h, _, _ in groups)
    n_items = -(-(n_tok * TOP_K) // tmr) + N_EXPERTS
    n_rows = n_items * tmr
    it = jnp.arange(n_items, dtype=jnp.int32)
    total = tile_end[-1]
    it_c = jnp.minimum(it, total - 1)
    item_e = jnp.minimum(jnp.searchsorted(tile_end, it_c, side='right'), N_EXPERTS - 1).astype(jnp.int32)
    item_r = it_c - tile_start[item_e]
    item_blk = (tile_start[item_e] + item_r).astype(jnp.int32)
    item_valid = jnp.where(it < total, jnp.clip(counts[item_e] - item_r * tmr, 0, tmr), 0).astype(jnp.int32)

    dests = []
    earlier = jnp.zeros_like(counts)
    x_rows = None
    for (h2d, route, _), gc in zip(groups, group_counts):
        idx = route[:, 0:TOP_K].astype(jnp.int32)
        rank = route[:, 2 * TOP_K:3 * TOP_K].astype(jnp.int32)
        dest = (pstart[idx] + earlier[idx] + rank).reshape(-1)
        dests.append(dest)
        earlier = earlier + gc
        x_rows = _moe_scatter(h2d, dest, counts, pstart, n_rows, x_rows)
    y_rows = _moe_mlp(x_rows, item_e, item_blk, item_valid, w1, b1, w2, b2)
    return [_moe_combine(h2d, route, dest, y_rows, ln_g, ln_b, alpha)
            for (h2d, route, _), dest in zip(groups, dests)]


def kernel(x_prompt, x_sample, cache_k, cache_v, page_table, state_ssm, state_conv, w_in, w_out, lambda_qk,
           attn_subln_w, conv_w, conv_b, dt_bias, a_log, d_skip, ssd_norm_w, ln1_g, ln1_b, w_router, b_router,
           w_mlp1, b_mlp1, w_mlp2, b_mlp2, ln2_g, ln2_b):
    bp, sp, d = x_prompt.shape
    bd, sd, _ = x_sample.shape
    depth = w_in.shape[0]
    assert depth == 1 and sd == 1, "kernel supports the single-layer, single-token-decode configuration"
    alpha = (2.0 * depth) ** 0.25
    slopes = 2.0 ** (-8.0 * jnp.arange(1, ATT_HEADS + 1, dtype=F32) / ATT_HEADS)
    l = 0
    lam_init = 0.8 - 0.6 * math.exp(-0.3 * l)

    w_in_b = _cast_bf16(w_in[l])
    w_out_b = _cast_bf16(w_out[l])
    moe_w = (w_mlp1[l], b_mlp1[l], w_mlp2[l], b_mlp2[l], ln2_g[l], ln2_b[l])
    ssd_w = (conv_w[l], conv_b[l], dt_bias[l], a_log[l], d_skip[l], ssd_norm_w[l])

    xp = x_prompt.reshape(bp * sp, d)
    q, k, v, z, xbc, dt = _inproj(xp, w_in_b, BF16)
    att = _attn_prompt(q.reshape(bp, sp, -1), k.reshape(bp, sp, -1), v.reshape(bp, sp, -1), lambda_qk[l],
                       attn_subln_w[l], slopes, lam_init)
    ssm_zero = jnp.zeros((bp, SSD_HEADS, SSD_HEADDIM, D_STATE), F32)
    conv_zero = jnp.zeros((bp, CONV_W - 1, CONV_DIM), F32)
    ssd, ssm_p, conv_p = _ssd_prompt(xbc.reshape(bp, sp, -1), dt.reshape(bp, sp, -1), z.reshape(bp, sp, -1),
                                     ssm_zero, conv_zero, *ssd_w)
    h, route, counts = _outproj_router(att.reshape(bp * sp, -1), ssd.reshape(bp * sp, -1), xp, w_out_b,
                                       ln1_g[l], ln1_b[l], w_router[l], b_router[l], alpha)

    xs = x_sample.reshape(bd, d)
    qs, ks, vs, zs, xbcs, dts = _inproj(xs, w_in_b, F32)
    att_s = _attn_decode(qs, ks, vs, cache_k, cache_v, page_table, lambda_qk[l], attn_subln_w[l], slopes,
                         lam_init)
    ssd_s, ssm_s, conv_s = _ssd_sample(xbcs, dts, zs, state_ssm[l], state_conv[l], *ssd_w)
    hs, route_s, counts_s = _outproj_router(att_s, ssd_s, xs, w_out_b, ln1_g[l], ln1_b[l], w_router[l],
                                            b_router[l], alpha)
    y_prompt, y_sample = _moe_ln2([(h, route, counts), (hs, route_s, counts_s)], *moe_w, alpha)
    y_prompt = y_prompt.reshape(bp, sp, d)
    y_sample = y_sample.reshape(bd, sd, d)

    hshape = (ATT_HEADS, 2 * ATT_DH)
    return (y_prompt, y_sample,
            k.reshape(1, bp, sp, *hshape), v.reshape(1, bp, sp, *hshape), ssm_p[None], conv_p[None],
            ks.reshape(1, bd, sd, *hshape), vs.reshape(1, bd, sd, *hshape), ssm_s[None], conv_s[None])
```

```python
import functools
import math

import jax
import jax.numpy as jnp
from jax import lax
from jax.experimental import pallas as pl
from jax.experimental.pallas import tpu as pltpu

F32 = jnp.float32
BF16 = jnp.bfloat16

ATT_DH = 64
ATT_HEADS = 8
ATT_WIDTH = ATT_HEADS * 2 * ATT_DH
ATT_SCALE = ATT_DH ** -0.5
SSD_HEADDIM = 64
SSD_HEADS = 16
SSD_WIDTH = SSD_HEADS * SSD_HEADDIM
SSD_GROUPS = 2
D_STATE = 128
CONV_W = 4
CONV_DIM = SSD_WIDTH + 2 * SSD_GROUPS * D_STATE
SSD_CHUNK = 128
N_EXPERTS = 32
TOP_K = 4
SWIGLU_ALPHA = 1.702
SWIGLU_LIMIT = 7.0
LN_EPS = 1e-5
RMS_EPS = 1e-5
PAGE_SIZE = 128

V7X_VMEM_LIMIT_BYTES = 56 * 1024 * 1024
LANES = 128

MOE_ROW_TILE = 1024
MOE_SUB = 256
MOE_F_TILE = 256


def _params(sem, vmem=V7X_VMEM_LIMIT_BYTES):
    return pltpu.CompilerParams(dimension_semantics=sem, vmem_limit_bytes=vmem)


def _sigmoid(x):
    return 1.0 / (1.0 + jnp.exp(-x))


def _silu(x):
    return x * _sigmoid(x)


def _softplus(x):
    return jnp.maximum(x, 0.0) + jnp.log1p(jnp.exp(-jnp.abs(x)))


def _layer_norm(x, g, b):
    mu = jnp.mean(x, axis=-1, keepdims=True)
    xc = x - mu
    var = jnp.mean(xc * xc, axis=-1, keepdims=True)
    return xc * lax.rsqrt(var + LN_EPS) * g + b


def _split3(x):
    a = x.astype(BF16)
    r = x - a.astype(F32)
    b = r.astype(BF16)
    c = (r - b.astype(F32)).astype(BF16)
    return a, b, c


def _lam(lq):
    s01 = jnp.sum(lq[0:1, :] * lq[1:2, :], axis=-1, keepdims=True)
    s23 = jnp.sum(lq[2:3, :] * lq[3:4, :], axis=-1, keepdims=True)
    return jnp.exp(s01) - jnp.exp(s23)


def _cast_kernel(w_ref, o_ref):
    o_ref[...] = w_ref[...].astype(BF16)


def _cast_bf16(w, row_tile=256):
    r, c = w.shape
    rt = min(row_tile, r)
    return pl.pallas_call(
        _cast_kernel,
        grid=(r // rt,),
        in_specs=[pl.BlockSpec((rt, c), lambda i: (i, 0))],
        out_specs=pl.BlockSpec((rt, c), lambda i: (i, 0)),
        out_shape=jax.ShapeDtypeStruct((r, c), BF16),
        compiler_params=_params(("arbitrary",)),
    )(w)


def _inproj_kernel(x_ref, w_ref, q_ref, k_ref, v_ref, z_ref, xbc_ref, dt_ref):
    xb = x_ref[...].astype(BF16)

    def mm(c0, c1):
        return jnp.dot(xb, w_ref[:, c0:c1], preferred_element_type=F32)

    a = ATT_WIDTH
    q_ref[...] = (mm(0, a) * ATT_SCALE).astype(q_ref.dtype)
    k_ref[...] = mm(a, 2 * a)
    v_ref[...] = mm(2 * a, 3 * a)
    z_ref[...] = mm(3 * a, 3 * a + SSD_WIDTH)
    c0 = 3 * a + SSD_WIDTH
    xbc_ref[...] = mm(c0, c0 + CONV_DIM)
    dt_ref[...] = mm(c0 + CONV_DIM, c0 + CONV_DIM + SSD_HEADS)


def _inproj(x2d, w_in_bf16, q_dtype):
    m, d = x2d.shape
    ncol = w_in_bf16.shape[1]
    tm = min(256, m)
    row = lambda i: (i, 0)
    widths = (ATT_WIDTH, ATT_WIDTH, ATT_WIDTH, SSD_WIDTH, CONV_DIM, SSD_HEADS)
    dtypes = (q_dtype, F32, F32, F32, F32, F32)
    return pl.pallas_call(
        _inproj_kernel,
        grid=(m // tm,),
        in_specs=[pl.BlockSpec((tm, d), row),
                  pl.BlockSpec((d, ncol), lambda i: (0, 0), pipeline_mode=pl.Buffered(1))],
        out_specs=[pl.BlockSpec((tm, w), row) for w in widths],
        out_shape=[jax.ShapeDtypeStruct((m, w), dt) for w, dt in zip(widths, dtypes)],
        compiler_params=_params(("arbitrary",)),
    )(x2d, w_in_bf16)


def _attn_prompt_kernel(slopes_ref, qt_ref, k_ref, v_ref, lq_ref, w_ref, o_ref, kb_ref, vt_ref, *, tq, hps,
                        lam_init):
    hp = pl.program_id(1)
    qi = pl.program_id(2)
    hd = 2 * ATT_DH
    vrows = vt_ref.shape[1]

    @pl.when(qi == 0)
    def _():
        kb_ref[...] = k_ref[0].astype(BF16)
        for hh in range(hps):
            vt_ref[hh, 0:hd, :] = v_ref[0, :, hh * hd:(hh + 1) * hd].T.astype(BF16)
            vt_ref[hh, hd:vrows, :] = jnp.ones((vrows - hd, vt_ref.shape[2]), BF16)

    drow = lax.broadcasted_iota(jnp.int32, (hd, tq), 0)
    kr = lax.broadcasted_iota(jnp.int32, (tq, tq), 0)
    qc = lax.broadcasted_iota(jnp.int32, (tq, tq), 1)
    rel = (qc - kr).astype(F32)
    future = kr > qc
    slopes = [slopes_ref[hp * hps + hh] for hh in range(hps)]
    qts = []
    for hh in range(hps):
        qt = qt_ref[0, hh * hd:(hh + 1) * hd, :]
        zero = jnp.zeros_like(qt)
        qts += [jnp.where(drow < ATT_DH, qt, zero), jnp.where(drow >= ATT_DH, qt, zero)]

    def scores(j):
        kb = kb_ref[pl.ds(pl.multiple_of(j * tq, tq), tq), :]
        return tuple(jnp.dot(kb[:, (c // 2) * hd:(c // 2 + 1) * hd], qts[c], preferred_element_type=F32)
                     for c in range(2 * hps))

    def block(j, raw, stats, diag):
        cols = pl.ds(pl.multiple_of(j * tq, tq), tq)
        dist = rel + jnp.full((1, 1), (qi - j) * tq, jnp.int32).astype(F32)
        out = []
        for c in range(2 * hps):
            hh = c // 2
            m, l, a = stats[c]
            s = raw[c] - dist * slopes[hh]
            if diag:
                s = jnp.where(future, -jnp.inf, s)
            mn = jnp.maximum(m, jnp.max(s, axis=0, keepdims=True))
            p = jnp.exp(s - mn)
            al = jnp.exp(m - mn)
            pv = jnp.dot(vt_ref[hh, :, cols], p.astype(BF16), preferred_element_type=F32)
            l = al * l + pv[hd:hd + 1, :]
            a = al * a + pv[0:hd, :]
            out.append((mn, l, a))
        return tuple(out)

    def step(j, stats):
        return block(j, scores(j), stats, False)

    init1 = (jnp.full((1, tq), -1e30, F32), jnp.zeros((1, tq), F32), jnp.zeros((hd, tq), F32))
    stats = lax.fori_loop(0, qi, step, (init1,) * (2 * hps))
    stats = block(qi, scores(qi), stats, True)
    lam = _lam(lq_ref[...]) + lam_init
    for hh in range(hps):
        (_, l0, a0), (_, l1, a1) = stats[2 * hh], stats[2 * hh + 1]
        o = a0 / l0 - lam * (a1 / l1)
        o = o * lax.rsqrt(jnp.mean(o * o, axis=0, keepdims=True) + RMS_EPS) * w_ref[...] * (1.0 - lam_init)
        o_ref[0, :, hh * hd:(hh + 1) * hd] = o.T.astype(BF16)


def _attn_prompt(q, k, v, lambda_qk, subln_w, slopes, lam_init):
    b, t, _ = k.shape
    tq = min(256, t)
    hd = 2 * ATT_DH
    hps = 8
    ones_rows = 16
    qt = jnp.swapaxes(q, 1, 2)
    kern = functools.partial(_attn_prompt_kernel, tq=tq, hps=hps, lam_init=lam_init)
    return pl.pallas_call(
        kern,
        grid=(b, ATT_HEADS // hps, t // tq),
        in_specs=[pl.BlockSpec(memory_space=pltpu.SMEM),
                  pl.BlockSpec((1, hps * hd, tq), lambda bi, h, qi: (bi, h, qi)),
                  pl.BlockSpec((1, t, hps * hd), lambda bi, h, qi: (bi, 0, h)),
                  pl.BlockSpec((1, t, hps * hd), lambda bi, h, qi: (bi, 0, h)),
                  pl.BlockSpec((4, ATT_DH), lambda bi, h, qi: (0, 0)),
                  pl.BlockSpec((hd, 1), lambda bi, h, qi: (0, 0))],
        out_specs=pl.BlockSpec((1, tq, hps * hd), lambda bi, h, qi: (bi, qi, h)),
        out_shape=jax.ShapeDtypeStruct((b, t, ATT_WIDTH), BF16),
        scratch_shapes=[pltpu.VMEM((t, hps * hd), BF16), pltpu.VMEM((hps, hd + ones_rows, t), BF16)],
        compiler_params=_params(("arbitrary", "arbitrary", "arbitrary")),
    )(slopes, qt, k, v, lambda_qk, subln_w.reshape(hd, 1))


def _attn_decode_kernel(pt_ref, q_ref, kn_ref, vn_ref, slope_ref, lq_ref, w_ref, *rest, pps, past_len, lam_init):
    kp_refs, vp_refs = rest[:pps], rest[pps:2 * pps]
    o_ref, qt_ref, s_ref, a_ref, m_ref, snew_ref, anew_ref, acc_ref = rest[2 * pps:]
    ph = pl.program_id(1)
    p = pl.program_id(2)
    n_steps = pl.num_programs(2)
    nh = ATT_HEADS
    nrow = 2 * nh
    hd = 2 * ATT_DH
    plane = PAGE_SIZE * nh
    n_pages = past_len // PAGE_SIZE
    per_vreg = LANES // nh

    def page_lanes(page):
        return pl.ds(pl.multiple_of(page * plane, plane), plane)

    @pl.when((ph == 0) & (p == 0))
    def _():
        q8 = q_ref[0]
        lane = lax.broadcasted_iota(jnp.int32, (nh, hd), 1)
        qt = jnp.concatenate([jnp.where(lane < ATT_DH, q8, 0.0), jnp.where(lane >= ATT_DH, q8, 0.0)], axis=0)
        qt_ref[...] = qt.astype(BF16)
        kn = jnp.concatenate([kn_ref[0], kn_ref[0]], axis=0)
        s_new = jnp.sum(qt * kn, axis=-1, keepdims=True)
        snew_ref[...] = s_new
        m_ref[...] = s_new

    @pl.when(ph == 0)
    def _():
        lane = lax.broadcasted_iota(jnp.int32, (nrow, plane), 1)
        row = lax.broadcasted_iota(jnp.int32, (nrow, plane), 0)
        own_head = (lane % nh) == (row % nh)
        for i in range(pps):
            page = p * pps + i
            kflat = kp_refs[i][0].reshape(plane, hd).astype(BF16)
            s = lax.dot_general(qt_ref[...], kflat, (((1,), (1,)), ((), ())), preferred_element_type=F32)
            dist = (past_len - page * PAGE_SIZE - lane // nh).astype(F32)
            s = jnp.where(own_head, s - slope_ref[...] * dist, -jnp.inf)
            s_ref[:, page_lanes(page)] = s
            m_ref[...] = jnp.maximum(m_ref[...], jnp.max(s, axis=-1, keepdims=True))

    @pl.when((ph == 1) & (p == 0))
    def _():
        m = m_ref[...]
        e_new = jnp.exp(snew_ref[...] - m)

        def expsum(g, part):
            for u in range(pps):
                e = jnp.exp(s_ref[:, page_lanes(g * pps + u)] - m)
                s_ref[:, page_lanes(g * pps + u)] = e
                for c in range(plane // LANES):
                    part = part + e[:, c * LANES:(c + 1) * LANES]
            return part

        part = lax.fori_loop(0, n_pages // pps, expsum, jnp.zeros((nrow, LANES), F32))
        den = jnp.sum(part, axis=-1, keepdims=True) + e_new
        inv = 1.0 / den
        lam = _lam(lq_ref[...]) + lam_init

        def combine(g, _):
            for u in range(pps):
                pn = s_ref[:, page_lanes(g * pps + u)] * inv
                a_ref[:, page_lanes(g * pps + u)] = pn[0:nh] - lam * pn[nh:nrow]
            return 0

        lax.fori_loop(0, n_pages // pps, combine, 0)
        pn_new = e_new * inv
        anew_ref[...] = pn_new[0:nh] - lam * pn_new[nh:nrow]
        acc_ref[...] = jnp.zeros_like(acc_ref)

    @pl.when(ph == 1)
    def _():
        lane = lax.broadcasted_iota(jnp.int32, (nh, LANES), 1)
        n_acc = acc_ref.shape[0]
        for i in range(pps):
            page = p * pps + i

            accs = [acc_ref[k] for k in range(n_acc)]
            for g in range(plane // LANES):
                av = a_ref[:, pl.ds(pl.multiple_of(page * plane + g * LANES, LANES), LANES)]
                for jj in range(per_vreg):
                    sel = (lane >= jj * nh) & (lane < (jj + 1) * nh)
                    wcol = jnp.sum(jnp.where(sel, av, 0.0), axis=-1, keepdims=True)
                    accs[jj % n_acc] = accs[jj % n_acc] + wcol * vp_refs[i][0, g * per_vreg + jj]
            for k in range(n_acc):
                acc_ref[k] = accs[k]

    @pl.when((ph == 1) & (p == n_steps - 1))
    def _():
        o = anew_ref[...] * vn_ref[0]
        for k in range(acc_ref.shape[0]):
            o = o + acc_ref[k]
        o = o * lax.rsqrt(jnp.mean(o * o, axis=-1, keepdims=True) + RMS_EPS) * w_ref[...] * (1.0 - lam_init)
        o_ref[0] = o.astype(BF16)


def _attn_decode(q, k_new, v_new, cache_k, cache_v, page_table, lambda_qk, subln_w, slopes, lam_init):
    bd = q.shape[0]
    n_pages = page_table.shape[1]
    nh = ATT_HEADS
    nrow = 2 * nh
    hd = 2 * ATT_DH
    pps = max(c for c in (1, 2, 4, 8) if n_pages % c == 0)
    n_steps = n_pages // pps
    kp = cache_k.reshape(-1, PAGE_SIZE, nh, hd)
    vp = cache_v.reshape(-1, PAGE_SIZE, nh, hd)
    slope_rows = jnp.tile(slopes, 2).reshape(nrow, 1)
    past_len = n_pages * PAGE_SIZE
    kern = functools.partial(_attn_decode_kernel, pps=pps, past_len=past_len, lam_init=lam_init)
    head3 = lambda b, ph, p, pt: (b, 0, 0)
    const2 = lambda b, ph, p, pt: (0, 0)

    def kpage(i):
        return lambda b, ph, p, pt: (pt[b * n_pages + jnp.where(ph == 0, p, n_steps - 1) * pps + i], 0, 0, 0)

    def vpage(i):
        return lambda b, ph, p, pt: (pt[b * n_pages + jnp.where(ph == 0, 0, p) * pps + i], 0, 0, 0)

    page_block = (1, PAGE_SIZE, nh, hd)
    n_acc = 4
    grid_spec = pltpu.PrefetchScalarGridSpec(
        num_scalar_prefetch=1,
        grid=(bd, 2, n_steps),
        in_specs=[pl.BlockSpec((1, nh, hd), head3),
                  pl.BlockSpec((1, nh, hd), head3),
                  pl.BlockSpec((1, nh, hd), head3),
                  pl.BlockSpec((nrow, 1), const2),
                  pl.BlockSpec((4, ATT_DH), const2),
                  pl.BlockSpec((1, hd), const2)]
                 + [pl.BlockSpec(page_block, kpage(i)) for i in range(pps)]
                 + [pl.BlockSpec(page_block, vpage(i)) for i in range(pps)],
        out_specs=pl.BlockSpec((1, nh, hd), head3),
        scratch_shapes=[pltpu.VMEM((nrow, hd), BF16), pltpu.VMEM((nrow, past_len * nh), F32),
                        pltpu.VMEM((nh, past_len * nh), F32), pltpu.VMEM((nrow, 1), F32),
                        pltpu.VMEM((nrow, 1), F32), pltpu.VMEM((nh, 1), F32),
                        pltpu.VMEM((n_acc, nh, hd), F32)],
    )
    out = pl.pallas_call(
        kern,
        grid_spec=grid_spec,
        out_shape=jax.ShapeDtypeStruct((bd, nh, hd), BF16),
        compiler_params=_params(("arbitrary", "arbitrary", "arbitrary")),
    )(page_table.reshape(-1), q.reshape(bd, nh, hd), k_new.reshape(bd, nh, hd), v_new.reshape(bd, nh, hd),
      slope_rows, lambda_qk, subln_w.reshape(1, hd), *([kp] * pps), *([vp] * pps))
    return out.reshape(bd, ATT_WIDTH)


def _gated_group_norm(y, z, w):
    yg = y * _silu(z)
    gw = SSD_WIDTH // SSD_GROUPS
    parts = []
    for g in range(SSD_GROUPS):
        v = yg[:, g * gw:(g + 1) * gw]
        parts.append(v * lax.rsqrt(jnp.mean(v * v, axis=-1, keepdims=True) + RMS_EPS) * w[:, g * gw:(g + 1) * gw])
    return parts


def _ssd_prompt_kernel(xbc_ref, dt_ref, dtt_ref, z_ref, cw_ref, cb_ref, dtb_ref, dtbt_ref, al_ref, alt_ref,
                       dsk_ref, nw_ref, h0_ref, c0_ref, y_ref, st_ref, cv_ref, xpad_ref, ysc_ref, xdd_ref):
    c = pl.program_id(1)
    nc = pl.num_programs(1)
    L = SSD_CHUNK
    P = SSD_HEADDIM
    hpg = SSD_HEADS // SSD_GROUPS
    halo = 8

    @pl.when(c == 0)
    def _():
        st_ref[...] = h0_ref[...]
        xpad_ref[0:halo, :] = jnp.zeros((halo, CONV_DIM), F32)
        xpad_ref[halo - (CONV_W - 1):halo, :] = c0_ref[0]

    xc = xbc_ref[0]
    xpad_ref[halo:halo + L, :] = xc
    conv = cb_ref[...]
    for i in range(CONV_W - 1):
        sh = CONV_W - 1 - i
        conv = conv + xpad_ref[halo - sh:halo - sh + L, :] * cw_ref[i:i + 1, :]
    conv = conv + xc * cw_ref[CONV_W - 1:CONV_W, :]
    tail = xc[L - (CONV_W - 1):L, :]
    xpad_ref[halo - (CONV_W - 1):halo, :] = tail

    @pl.when(c == nc - 1)
    def _():
        cv_ref[0] = tail

    act = _silu(conv)
    xs = act[:, :SSD_WIDTH]
    bmat = [act[:, SSD_WIDTH + g * D_STATE:SSD_WIDTH + (g + 1) * D_STATE].astype(BF16) for g in range(SSD_GROUPS)]
    c_off = SSD_WIDTH + SSD_GROUPS * D_STATE
    cmat = [act[:, c_off + g * D_STATE:c_off + (g + 1) * D_STATE].astype(BF16) for g in range(SSD_GROUPS)]

    dtp = _softplus(dt_ref[0] + dtb_ref[...])
    dtpt = _softplus(dtt_ref[0] + dtbt_ref[...])
    da = dtp * (-jnp.exp(al_ref[...]))
    dat = dtpt * (-jnp.exp(alt_ref[...]))
    ri = lax.broadcasted_iota(jnp.int32, (L, L), 0)
    ci = lax.broadcasted_iota(jnp.int32, (L, L), 1)
    causal = ri >= ci
    tri = jnp.where(causal, 1.0, 0.0).astype(BF16)
    trit = jnp.where(ci >= ri, 1.0, 0.0).astype(BF16)
    cs = sum(jnp.dot(tri, part, preferred_element_type=F32) for part in _split3(da))
    cst = sum(jnp.dot(part, trit, preferred_element_type=F32) for part in _split3(dat))

    cb = [lax.dot_general(cmat[g], bmat[g], (((1,), (1,)), ((), ())), preferred_element_type=F32)
          for g in range(SSD_GROUPS)]
    dsk = dsk_ref[...]

    for h in range(SSD_HEADS):
        g = h // hpg
        cs_col = cs[:, h:h + 1]
        diff = cs_col - cst[h:h + 1, :]
        lmat = jnp.exp(jnp.where(causal, diff, -jnp.inf))
        mmat = (cb[g] * lmat).astype(BF16)
        xs_h = xs[:, h * P:(h + 1) * P]
        xd_h = xs_h * dtp[:, h:h + 1]
        y = jnp.dot(mmat, xd_h.astype(BF16), preferred_element_type=F32)
        st = st_ref[0, h]
        yoff = lax.dot_general(cmat[g], st.astype(BF16), (((1,), (1,)), ((), ())), preferred_element_type=F32)
        y = y + jnp.exp(cs_col) * yoff + dsk[:, h:h + 1] * xs_h
        ysc_ref[:, h * P:(h + 1) * P] = y
        cs_last = cs[L - 1:L, h:h + 1]
        xdd_ref[:, h * P:(h + 1) * P] = xd_h * jnp.exp(cs_last - cs_col)

    xddt = xdd_ref[...].T
    for h in range(SSD_HEADS):
        g = h // hpg
        new = jnp.dot(xddt[h * P:(h + 1) * P, :].astype(BF16), bmat[g], preferred_element_type=F32)
        cs_last = cs[L - 1:L, h:h + 1]
        st_ref[0, h] = jnp.exp(cs_last) * st_ref[0, h] + new

    parts = _gated_group_norm(ysc_ref[...], z_ref[0], nw_ref[...])
    gw = SSD_WIDTH // SSD_GROUPS
    for g in range(SSD_GROUPS):
        y_ref[0, :, g * gw:(g + 1) * gw] = parts[g].astype(BF16)


def _ssd_prompt(xbc, dt, z, h0, conv0, conv_w, conv_b, dt_bias, a_log, d_skip, norm_w):
    b, t, _ = xbc.shape
    L = SSD_CHUNK
    nc = t // L
    dtt = jnp.swapaxes(dt, 1, 2)
    seq = lambda bi, ci: (bi, ci, 0)
    const2 = lambda bi, ci: (0, 0)
    full2 = lambda shp: pl.BlockSpec(shp, const2)
    return pl.pallas_call(
        _ssd_prompt_kernel,
        grid=(b, nc),
        in_specs=[pl.BlockSpec((1, L, CONV_DIM), seq),
                  pl.BlockSpec((1, L, SSD_HEADS), seq),
                  pl.BlockSpec((1, SSD_HEADS, L), lambda bi, ci: (bi, 0, ci)),
                  pl.BlockSpec((1, L, SSD_WIDTH), seq),
                  full2((CONV_W, CONV_DIM)), full2((1, CONV_DIM)),
                  full2((1, SSD_HEADS)), full2((SSD_HEADS, 1)),
                  full2((1, SSD_HEADS)), full2((SSD_HEADS, 1)),
                  full2((1, SSD_HEADS)), full2((1, SSD_WIDTH)),
                  pl.BlockSpec((1, SSD_HEADS, SSD_HEADDIM, D_STATE), lambda bi, ci: (bi, 0, 0, 0)),
                  pl.BlockSpec((1, CONV_W - 1, CONV_DIM), lambda bi, ci: (bi, 0, 0))],
        out_specs=[pl.BlockSpec((1, L, SSD_WIDTH), seq),
                   pl.BlockSpec((1, SSD_HEADS, SSD_HEADDIM, D_STATE), lambda bi, ci: (bi, 0, 0, 0)),
                   pl.BlockSpec((1, CONV_W - 1, CONV_DIM), lambda bi, ci: (bi, 0, 0))],
        out_shape=[jax.ShapeDtypeStruct((b, t, SSD_WIDTH), BF16),
                   jax.ShapeDtypeStruct((b, SSD_HEADS, SSD_HEADDIM, D_STATE), F32),
                   jax.ShapeDtypeStruct((b, CONV_W - 1, CONV_DIM), F32)],
        scratch_shapes=[pltpu.VMEM((8 + L, CONV_DIM), F32), pltpu.VMEM((L, SSD_WIDTH), F32),
                        pltpu.VMEM((L, SSD_WIDTH), F32)],
        compiler_params=_params(("arbitrary", "arbitrary")),
    )(xbc, dt, dtt, z, conv_w, conv_b.reshape(1, -1), dt_bias.reshape(1, -1), dt_bias.reshape(-1, 1),
      a_log.reshape(1, -1), a_log.reshape(-1, 1), d_skip.reshape(1, -1), norm_w.reshape(1, -1), h0, conv0)


def _bf16_round(v):
    return v.astype(BF16).astype(F32)


def _ssd_sample_kernel(xbc_ref, ci_ref, dt_ref, z_ref, cw_ref, cb_ref, dtb_ref, al_ref, dsk_ref, nw_ref, h0_ref,
                       xbc8_ref, ci8_ref, dt8_ref, cw8_ref, cb8_ref, dtb8_ref, y_ref, st_ref, cv_ref, ysc_ref):
    P = SSD_HEADDIM
    hpg = SSD_HEADS // SSD_GROUPS
    xrow = xbc_ref[0]
    hist = ci_ref[0]
    conv = cb_ref[...]
    for i in range(CONV_W - 1):
        conv = conv + hist[i:i + 1, :] * cw_ref[i:i + 1, :]
    conv = conv + xrow * cw_ref[CONV_W - 1:CONV_W, :]
    cv_ref[0, 0:CONV_W - 2, :] = hist[1:CONV_W - 1, :]
    cv_ref[0, CONV_W - 2:CONV_W - 1, :] = xrow
    act = _silu(conv)
    xs = act[:, :SSD_WIDTH]
    c_off = SSD_WIDTH + SSD_GROUPS * D_STATE
    dtp = _softplus(dt_ref[0] + dtb_ref[...])
    decay = jnp.exp(dtp * (-jnp.exp(al_ref[...])))
    dsk = dsk_ref[...]

    nx = SSD_WIDTH // LANES
    conv8 = cb8_ref[0:nx, :]
    for i in range(CONV_W - 1):
        conv8 = conv8 + ci8_ref[0, i, 0:nx, :] * cw8_ref[i, 0:nx, :]
    conv8 = conv8 + xbc8_ref[0, 0:nx, :] * cw8_ref[CONV_W - 1, 0:nx, :]
    xd8 = _silu(conv8) * _softplus(dt8_ref[0] + dtb8_ref[...])
    xdt = xd8.T

    brows = [act[:, SSD_WIDTH + g * D_STATE:SSD_WIDTH + (g + 1) * D_STATE] for g in range(SSD_GROUPS)]
    crows = [act[:, c_off + g * D_STATE:c_off + (g + 1) * D_STATE] for g in range(SSD_GROUPS)]
    cbs = [jnp.sum(_bf16_round(brows[g]) * _bf16_round(crows[g]), axis=-1, keepdims=True) for g in range(SSD_GROUPS)]
    c8s = [jnp.broadcast_to(crows[g], (8, D_STATE)).astype(BF16) for g in range(SSD_GROUPS)]
    hpr = LANES // P
    for h in range(SSD_HEADS):
        g = h // hpg
        xs_h = xs[:, h * P:(h + 1) * P]
        xd_h = xs_h * dtp[:, h:h + 1]
        xcol = xdt[(h % hpr) * P:(h % hpr + 1) * P, h // hpr:h // hpr + 1]
        dec = decay[:, h:h + 1]
        h0q = h0_ref[0, h].astype(BF16)
        st_ref[0, h] = _bf16_round(dec) * h0q.astype(F32) + _bf16_round(xcol * brows[g])
        yoff = lax.dot_general(c8s[g], h0q, (((1,), (1,)), ((), ())), preferred_element_type=F32)[0:1, :]
        ysc_ref[:, h * P:(h + 1) * P] = (cbs[g] * xd_h + dec * yoff) + dsk[:, h:h + 1] * xs_h
    parts = _gated_group_norm(ysc_ref[...], z_ref[0], nw_ref[...])
    gw = SSD_WIDTH // SSD_GROUPS
    for g in range(SSD_GROUPS):
        y_ref[0, :, g * gw:(g + 1) * gw] = parts[g].astype(BF16)


def _ssd_sample(xbc, dt, z, h0, conv0, conv_w, conv_b, dt_bias, a_log, d_skip, norm_w):
    bd = xbc.shape[0]
    nr = CONV_DIM // LANES
    nx = SSD_WIDTH // LANES
    rep = SSD_HEADDIM
    row3 = lambda b: (b, 0, 0)
    const2 = lambda b: (0, 0)
    full2 = lambda shp: pl.BlockSpec(shp, const2)
    state = pl.BlockSpec((1, SSD_HEADS, SSD_HEADDIM, D_STATE), lambda b: (b, 0, 0, 0))
    y, st, cv = pl.pallas_call(
        _ssd_sample_kernel,
        grid=(bd,),
        in_specs=[pl.BlockSpec((1, 1, CONV_DIM), row3),
                  pl.BlockSpec((1, CONV_W - 1, CONV_DIM), row3),
                  pl.BlockSpec((1, 1, SSD_HEADS), row3),
                  pl.BlockSpec((1, 1, SSD_WIDTH), row3),
                  full2((CONV_W, CONV_DIM)), full2((1, CONV_DIM)), full2((1, SSD_HEADS)), full2((1, SSD_HEADS)),
                  full2((1, SSD_HEADS)), full2((1, SSD_WIDTH)), state,
                  pl.BlockSpec((1, nr, LANES), row3),
                  pl.BlockSpec((1, CONV_W - 1, nr, LANES), lambda b: (b, 0, 0, 0)),
                  pl.BlockSpec((1, nx, LANES), row3),
                  pl.BlockSpec((CONV_W, nr, LANES), lambda b: (0, 0, 0)),
                  full2((nr, LANES)), full2((nx, LANES))],
        out_specs=[pl.BlockSpec((1, 1, SSD_WIDTH), row3), state,
                   pl.BlockSpec((1, CONV_W - 1, CONV_DIM), row3)],
        out_shape=[jax.ShapeDtypeStruct((bd, 1, SSD_WIDTH), BF16),
                   jax.ShapeDtypeStruct((bd, SSD_HEADS, SSD_HEADDIM, D_STATE), F32),
                   jax.ShapeDtypeStruct((bd, CONV_W - 1, CONV_DIM), F32)],
        scratch_shapes=[pltpu.VMEM((1, SSD_WIDTH), F32)],
        compiler_params=_params(("arbitrary",)),
    )(xbc.reshape(bd, 1, CONV_DIM), conv0, dt.reshape(bd, 1, SSD_HEADS), z.reshape(bd, 1, SSD_WIDTH),
      conv_w, conv_b.reshape(1, -1), dt_bias.reshape(1, -1), a_log.reshape(1, -1), d_skip.reshape(1, -1),
      norm_w.reshape(1, -1), h0,
      xbc.reshape(bd, nr, LANES), conv0.reshape(bd, CONV_W - 1, nr, LANES),
      jnp.repeat(dt, rep, axis=-1).reshape(bd, nx, LANES), conv_w.reshape(CONV_W, nr, LANES),
      conv_b.reshape(nr, LANES), jnp.repeat(dt_bias, rep).reshape(nx, LANES))
    return y.reshape(bd, SSD_WIDTH), st, cv


def _outproj_kernel(att_ref, ssd_ref, x_ref, w_ref, g_ref, b_ref, wr_ref, br_ref, h_ref, route_ref, cnt_ref,
                    carry_ref, *, alpha, sub):
    i = pl.program_id(0)
    tm = x_ref.shape[0]

    @pl.when(i == 0)
    def _():
        carry_ref[...] = jnp.zeros_like(carry_ref)

    lane = lax.broadcasted_iota(jnp.int32, (sub, N_EXPERTS), 1).astype(F32)
    ri = lax.broadcasted_iota(jnp.int32, (sub, sub), 0)
    ci = lax.broadcasted_iota(jnp.int32, (sub, sub), 1)
    before = jnp.where(ci < ri, 1.0, 0.0).astype(BF16)
    olane = lax.broadcasted_iota(jnp.int32, (sub, LANES), 1)
    wrb = wr_ref[...].astype(BF16)
    carry = carry_ref[...]
    for r0 in range(0, tm, sub):
        rows = pl.ds(r0, sub)
        mix = jnp.dot(att_ref[rows, :], w_ref[0:ATT_WIDTH, :], preferred_element_type=F32)
        mix = mix + jnp.dot(ssd_ref[rows, :], w_ref[ATT_WIDTH:ATT_WIDTH + SSD_WIDTH, :], preferred_element_type=F32)
        hval = _layer_norm(alpha * x_ref[rows, :] + mix, g_ref[...], b_ref[...])
        h_ref[rows, :] = hval

        logits = jnp.dot(hval.astype(BF16), wrb, preferred_element_type=F32) + br_ref[...]
        work = logits
        chosen = jnp.zeros((sub, N_EXPERTS), F32)
        vals, idxs = [], []
        for _ in range(TOP_K):
            mk = jnp.max(work, axis=-1, keepdims=True)
            ik = jnp.min(jnp.where(work == mk, lane, float(N_EXPERTS)), axis=-1, keepdims=True)
            sel = lane == ik
            work = jnp.where(sel, -jnp.inf, work)
            chosen = jnp.where(sel, 1.0, chosen)
            vals.append(mk)
            idxs.append(ik)
        es = [jnp.exp(v - vals[0]) for v in vals]
        den = es[0] + es[1] + es[2] + es[3]

        prefix = jnp.dot(before, chosen.astype(BF16), preferred_element_type=F32) + carry
        carry = carry + jnp.sum(chosen, axis=0, keepdims=True)

        route = jnp.zeros((sub, LANES), F32)
        for k in range(TOP_K):
            rank_k = jnp.sum(jnp.where(lane == idxs[k], prefix, 0.0), axis=-1, keepdims=True)
            route = jnp.where(olane == k, idxs[k], route)
            route = jnp.where(olane == TOP_K + k, es[k] / den, route)
            route = jnp.where(olane == 2 * TOP_K + k, rank_k, route)
        route_ref[rows, :] = route
    carry_ref[...] = carry
    cnt_ref[...] = carry


def _outproj_router(att, ssd, x2d, w_out_bf16, ln_g, ln_b, w_router, b_router, alpha):
    m, d = x2d.shape
    sub = min(256, m)
    tm = min(2 * sub, m)
    row = lambda i: (i, 0)
    const = lambda i: (0, 0)
    kern = functools.partial(_outproj_kernel, alpha=alpha, sub=sub)
    return pl.pallas_call(
        kern,
        grid=(m // tm,),
        in_specs=[pl.BlockSpec((tm, ATT_WIDTH), row), pl.BlockSpec((tm, SSD_WIDTH), row), pl.BlockSpec((tm, d), row),
                  pl.BlockSpec((ATT_WIDTH + SSD_WIDTH, d), const, pipeline_mode=pl.Buffered(1)),
                  pl.BlockSpec((1, d), const), pl.BlockSpec((1, d), const),
                  pl.BlockSpec((d, N_EXPERTS), const), pl.BlockSpec((1, N_EXPERTS), const)],
        out_specs=[pl.BlockSpec((tm, d), row), pl.BlockSpec((tm, LANES), row), pl.BlockSpec((1, N_EXPERTS), const)],
        out_shape=[jax.ShapeDtypeStruct((m, d), F32), jax.ShapeDtypeStruct((m, LANES), F32),
                   jax.ShapeDtypeStruct((1, N_EXPERTS), F32)],
        scratch_shapes=[pltpu.VMEM((1, N_EXPERTS), F32)],
        compiler_params=_params(("arbitrary",)),
    )(att, ssd, x2d, w_out_bf16, ln_g.reshape(1, d), ln_b.reshape(1, d), w_router, b_router.reshape(1, -1))


def _row_copy(src, dst, sem):
    return pltpu.make_async_copy(src, dst, sem)


def _scatter_kernel(cnt_ref, pst_ref, dest_ref, h_ref, *rest, first):
    xrows_ref, zero_ref, stage_ref, sems, zsem = rest[-5:]
    i = pl.program_id(0)
    tm = h_ref.shape[0]

    @pl.when((i == 0) & first)
    def _():
        zero_ref[...] = jnp.zeros_like(zero_ref)

        def per_expert(e, _):
            n = cnt_ref[e]
            base = pst_ref[e]
            end = (n + MOE_SUB - 1) // MOE_SUB * MOE_SUB

            def start(r, _):
                _row_copy(zero_ref.at[pl.ds(0, 1)], xrows_ref.at[pl.ds(base + r, 1)], zsem).start()
                return 0

            def wait(r, _):
                _row_copy(zero_ref.at[pl.ds(0, 1)], xrows_ref.at[pl.ds(base + r, 1)], zsem).wait()
                return 0

            lax.fori_loop(n, end, start, 0)
            lax.fori_loop(n, end, wait, 0)
            return 0

        lax.fori_loop(0, N_EXPERTS, per_expert, 0)

    slot = i % 2
    stage_ref[slot] = h_ref[...]
    for t in range(tm):
        for k in range(TOP_K):
            d = dest_ref[0, 0, t * TOP_K + k]
            _row_copy(stage_ref.at[slot, pl.ds(t, 1)], xrows_ref.at[pl.ds(d, 1)], sems.at[slot]).start()

    def drain(s):
        for t in range(tm):
            for k in range(TOP_K):
                _row_copy(stage_ref.at[s, pl.ds(t, 1)], xrows_ref.at[pl.ds(0, 1)], sems.at[s]).wait()

    @pl.when(i > 0)
    def _():
        drain(1 - slot)

    @pl.when(i == pl.num_programs(0) - 1)
    def _():
        drain(slot)


def _moe_scatter(h2d, dest, counts, pstart, n_rows, x_rows=None):
    m, d = h2d.shape
    tm = min(256, m)
    nt = m // tm
    dest3 = dest.reshape(nt, 1, tm * TOP_K)
    first = x_rows is None
    in_specs = [pl.BlockSpec((1, 1, tm * TOP_K), lambda i, c, p: (i, 0, 0), memory_space=pltpu.SMEM),
                pl.BlockSpec((tm, d), lambda i, c, p: (i, 0))]
    args = [counts, pstart, dest3, h2d]
    aliases = {}
    if not first:
        in_specs.append(pl.BlockSpec(memory_space=pl.ANY))
        args.append(x_rows)
        aliases = {len(args) - 1: 0}
    grid_spec = pltpu.PrefetchScalarGridSpec(
        num_scalar_prefetch=2,
        grid=(nt,),
        in_specs=in_specs,
        out_specs=pl.BlockSpec(memory_space=pl.ANY),
        scratch_shapes=[pltpu.VMEM((8, d), F32), pltpu.VMEM((2, tm, d), F32), pltpu.SemaphoreType.DMA((2,)),
                        pltpu.SemaphoreType.DMA(())],
    )
    return pl.pallas_call(
        functools.partial(_scatter_kernel, first=first),
        grid_spec=grid_spec,
        out_shape=jax.ShapeDtypeStruct((n_rows, d), F32),
        input_output_aliases=aliases,
        compiler_params=_params(("arbitrary",)),
    )(*args)


def _moe_mlp_kernel(ie_ref, ib_ref, iv_ref, x_ref, w1g_ref, w1l_ref, b1g_ref, b1l_ref, w2_ref, b2_ref, o_ref,
                    xb_ref, wg_ref, wl_ref, w2b_ref):
    i = pl.program_id(0)
    j = pl.program_id(1)
    nvalid = iv_ref[i]
    nsub = (nvalid + MOE_SUB - 1) // MOE_SUB
    d = x_ref.shape[1]

    def rows(s):
        return pl.ds(pl.multiple_of(s * MOE_SUB, MOE_SUB), MOE_SUB)

    @pl.when(nvalid > 0)
    def _():
        @pl.when(j == 0)
        def _():
            def init(s, _):
                xb_ref[rows(s), :] = x_ref[rows(s), :].astype(BF16)
                o_ref[rows(s), :] = jnp.broadcast_to(b2_ref[0], (MOE_SUB, d))
                return 0

            lax.fori_loop(0, nsub, init, 0)

        def hidden(s):
            xb = xb_ref[rows(s), :]
            return (jnp.dot(xb, wg_ref[...], preferred_element_type=F32),
                    jnp.dot(xb, wl_ref[...], preferred_element_type=F32))

        def finish(s, hid):
            glu = jnp.minimum(hid[0] + b1g_ref[0], SWIGLU_LIMIT)
            lin = jnp.clip(hid[1] + b1l_ref[0], -SWIGLU_LIMIT, SWIGLU_LIMIT)
            act = glu * _sigmoid(SWIGLU_ALPHA * glu) * (lin + 1.0)
            o_ref[rows(s), :] += jnp.dot(act.astype(BF16), w2b_ref[...], preferred_element_type=F32)

        def step(s, hid):
            nxt = hidden(s + 1)
            finish(s, hid)
            return nxt

        xb0 = xb_ref[rows(0), :]
        wg_ref[...] = w1g_ref[0].astype(BF16)
        glu0 = jnp.dot(xb0, wg_ref[...], preferred_element_type=F32)
        wl_ref[...] = w1l_ref[0].astype(BF16)
        lin0 = jnp.dot(xb0, wl_ref[...], preferred_element_type=F32)
        w2b_ref[...] = w2_ref[0].astype(BF16)
        hid = (glu0, lin0)
        hid = lax.fori_loop(0, nsub - 1, step, hid)
        finish(nsub - 1, hid)


def _moe_mlp(x_rows, item_e, item_blk, item_valid, w1, b1, w2, b2):
    n_rows, d = x_rows.shape
    n_items = item_e.shape[0]
    d_ff = w2.shape[1]
    tf = MOE_F_TILE
    nf = d_ff // tf
    tmr = MOE_ROW_TILE

    def jj(i, j, iv):
        return jnp.where(iv[i] > 0, j, nf - 1)

    grid_spec = pltpu.PrefetchScalarGridSpec(
        num_scalar_prefetch=3,
        grid=(n_items, nf),
        in_specs=[pl.BlockSpec((tmr, d), lambda i, j, ie, ib, iv: (ib[i], 0)),
                  pl.BlockSpec((1, d, tf), lambda i, j, ie, ib, iv: (ie[i], 0, jj(i, j, iv))),
                  pl.BlockSpec((1, d, tf), lambda i, j, ie, ib, iv: (ie[i], 0, nf + jj(i, j, iv))),
                  pl.BlockSpec((1, 1, tf), lambda i, j, ie, ib, iv: (ie[i], 0, jj(i, j, iv))),
                  pl.BlockSpec((1, 1, tf), lambda i, j, ie, ib, iv: (ie[i], 0, nf + jj(i, j, iv))),
                  pl.BlockSpec((1, tf, d), lambda i, j, ie, ib, iv: (ie[i], jj(i, j, iv), 0)),
                  pl.BlockSpec((1, 1, d), lambda i, j, ie, ib, iv: (ie[i], 0, 0))],
        out_specs=pl.BlockSpec((tmr, d), lambda i, j, ie, ib, iv: (ib[i], 0)),
        scratch_shapes=[pltpu.VMEM((tmr, d), BF16), pltpu.VMEM((d, tf), BF16), pltpu.VMEM((d, tf), BF16),
                        pltpu.VMEM((tf, d), BF16)],
    )
    return pl.pallas_call(
        _moe_mlp_kernel,
        grid_spec=grid_spec,
        out_shape=jax.ShapeDtypeStruct((n_rows, d), F32),
        compiler_params=_params(("arbitrary", "arbitrary")),
    )(item_e, item_blk, item_valid, x_rows, w1, w1, b1.reshape(N_EXPERTS, 1, -1), b1.reshape(N_EXPERTS, 1, -1),
      w2, b2.reshape(N_EXPERTS, 1, -1))


def _combine_kernel(dest_ref, nxt_ref, h_ref, route_ref, yrows_ref, g_ref, b_ref, o_ref, buf_ref, sems, *, alpha):
    i = pl.program_id(0)
    nt = pl.num_programs(0)
    tm = h_ref.shape[0]
    slot = i % 2

    def gather(idx_ref, s):
        for t in range(tm):
            for k in range(TOP_K):
                d = idx_ref[0, 0, t * TOP_K + k]
                _row_copy(yrows_ref.at[pl.ds(d, 1)], buf_ref.at[s, k, pl.ds(t, 1)], sems.at[s]).start()

    @pl.when(i == 0)
    def _():
        gather(dest_ref, 0)

    @pl.when(i + 1 < nt)
    def _():
        gather(nxt_ref, 1 - slot)

    for t in range(tm):
        for k in range(TOP_K):
            _row_copy(yrows_ref.at[pl.ds(0, 1)], buf_ref.at[slot, k, pl.ds(t, 1)], sems.at[slot]).wait()
    route = route_ref[...]
    acc = route[:, TOP_K:TOP_K + 1] * buf_ref[slot, 0]
    for k in range(1, TOP_K):
        acc = acc + route[:, TOP_K + k:TOP_K + k + 1] * buf_ref[slot, k]
    o_ref[...] = _layer_norm(alpha * h_ref[...] + acc, g_ref[...], b_ref[...])


def _moe_combine(h2d, route, dest, y_rows, ln_g, ln_b, alpha):
    m, d = h2d.shape
    tm = min(128, m)
    nt = m // tm
    dest3 = dest.reshape(nt, 1, tm * TOP_K)
    kern = functools.partial(_combine_kernel, alpha=alpha)
    idx_block = (1, 1, tm * TOP_K)
    return pl.pallas_call(
        kern,
        grid=(nt,),
        in_specs=[pl.BlockSpec(idx_block, lambda i: (i, 0, 0), memory_space=pltpu.SMEM),
                  pl.BlockSpec(idx_block, lambda i: (jnp.minimum(i + 1, nt - 1), 0, 0), memory_space=pltpu.SMEM),
                  pl.BlockSpec((tm, d), lambda i: (i, 0)),
                  pl.BlockSpec((tm, LANES), lambda i: (i, 0)),
                  pl.BlockSpec(memory_space=pl.ANY),
                  pl.BlockSpec((1, d), lambda i: (0, 0)), pl.BlockSpec((1, d), lambda i: (0, 0))],
        out_specs=pl.BlockSpec((tm, d), lambda i: (i, 0)),
        out_shape=jax.ShapeDtypeStruct((m, d), F32),
        scratch_shapes=[pltpu.VMEM((2, TOP_K, tm, d), F32), pltpu.SemaphoreType.DMA((2,))],
        compiler_params=_params(("arbitrary",)),
    )(dest3, dest3, h2d, route, y_rows, ln_g.reshape(1, d), ln_b.reshape(1, d))


def _moe_ln2(groups, w1, b1, w2, b2, ln_g, ln_b, alpha):
    tmr = MOE_ROW_TILE
    group_counts = [c.reshape(-1).astype(jnp.int32) for _, _, c in groups]
    counts = sum(group_counts)
    tiles = (counts + tmr - 1) // tmr
    tile_end = jnp.cumsum(tiles)
    tile_start = tile_end - tiles
    pstart = (tile_start * tmr).astype(jnp.int32)
    n_tok = sum(h.shape[0] for h, _, _ in groups)
    n_items = -(-(n_tok * TOP_K) // tmr) + N_EXPERTS
    n_rows = n_items * tmr
    it = jnp.arange(n_items, dtype=jnp.int32)
    total = tile_end[-1]
    it_c = jnp.minimum(it, total - 1)
    item_e = jnp.minimum(jnp.searchsorted(tile_end, it_c, side='right'), N_EXPERTS - 1).astype(jnp.int32)
    item_r = it_c - tile_start[item_e]
    item_blk = (tile_start[item_e] + item_r).astype(jnp.int32)
    item_valid = jnp.where(it < total, jnp.clip(counts[item_e] - item_r * tmr, 0, tmr), 0).astype(jnp.int32)

    dests = []
    earlier = jnp.zeros_like(counts)
    x_rows = None
    for (h2d, route, _), gc in zip(groups, group_counts):
        idx = route[:, 0:TOP_K].astype(jnp.int32)
        rank = route[:, 2 * TOP_K:3 * TOP_K].astype(jnp.int32)
        dest = (pstart[idx] + earlier[idx] + rank).reshape(-1)
        dests.append(dest)
        earlier = earlier + gc
        x_rows = _moe_scatter(h2d, dest, counts, pstart, n_rows, x_rows)
    y_rows = _moe_mlp(x_rows, item_e, item_blk, item_valid, w1, b1, w2, b2)
    return [_moe_combine(h2d, route, dest, y_rows, ln_g, ln_b, alpha)
            for (h2d, route, _), dest in zip(groups, dests)]


def kernel(x_prompt, x_sample, cache_k, cache_v, page_table, state_ssm, state_conv, w_in, w_out, lambda_qk,
           attn_subln_w, conv_w, conv_b, dt_bias, a_log, d_skip, ssd_norm_w, ln1_g, ln1_b, w_router, b_router,
           w_mlp1, b_mlp1, w_mlp2, b_mlp2, ln2_g, ln2_b):
    bp, sp, d = x_prompt.shape
    bd, sd, _ = x_sample.shape
    depth = w_in.shape[0]
    assert depth == 1 and sd == 1, "kernel supports the single-layer, single-token-decode configuration"
    alpha = (2.0 * depth) ** 0.25
    slopes = 2.0 ** (-8.0 * jnp.arange(1, ATT_HEADS + 1, dtype=F32) / ATT_HEADS)
    l = 0
    lam_init = 0.8 - 0.6 * math.exp(-0.3 * l)

    w_in_b = _cast_bf16(w_in[l])
    w_out_b = _cast_bf16(w_out[l])
    moe_w = (w_mlp1[l], b_mlp1[l], w_mlp2[l], b_mlp2[l], ln2_g[l], ln2_b[l])
    ssd_w = (conv_w[l], conv_b[l], dt_bias[l], a_log[l], d_skip[l], ssd_norm_w[l])

    xp = x_prompt.reshape(bp * sp, d)
    q, k, v, z, xbc, dt = _inproj(xp, w_in_b, BF16)
    att = _attn_prompt(q.reshape(bp, sp, -1), k.reshape(bp, sp, -1), v.reshape(bp, sp, -1), lambda_qk[l],
                       attn_subln_w[l], slopes, lam_init)
    ssm_zero = jnp.zeros((bp, SSD_HEADS, SSD_HEADDIM, D_STATE), F32)
    conv_zero = jnp.zeros((bp, CONV_W - 1, CONV_DIM), F32)
    ssd, ssm_p, conv_p = _ssd_prompt(xbc.reshape(bp, sp, -1), dt.reshape(bp, sp, -1), z.reshape(bp, sp, -1),
                                     ssm_zero, conv_zero, *ssd_w)
    h, route, counts = _outproj_router(att.reshape(bp * sp, -1), ssd.reshape(bp * sp, -1), xp, w_out_b,
                                       ln1_g[l], ln1_b[l], w_router[l], b_router[l], alpha)

    xs = x_sample.reshape(bd, d)
    qs, ks, vs, zs, xbcs, dts = _inproj(xs, w_in_b, F32)
    att_s = _attn_decode(qs, ks, vs, cache_k, cache_v, page_table, lambda_qk[l], attn_subln_w[l], slopes,
                         lam_init)
    ssd_s, ssm_s, conv_s = _ssd_sample(xbcs, dts, zs, state_ssm[l], state_conv[l], *ssd_w)
    hs, route_s, counts_s = _outproj_router(att_s, ssd_s, xs, w_out_b, ln1_g[l], ln1_b[l], w_router[l],
                                            b_router[l], alpha)
    y_prompt, y_sample = _moe_ln2([(h, route, counts), (hs, route_s, counts_s)], *moe_w, alpha)
    y_prompt = y_prompt.reshape(bp, sp, d)
    y_sample = y_sample.reshape(bd, sd, d)

    hshape = (ATT_HEADS, 2 * ATT_DH)
    return (y_prompt, y_sample,
            k.reshape(1, bp, sp, *hshape), v.reshape(1, bp, sp, *hshape), ssm_p[None], conv_p[None],
            ks.reshape(1, bd, sd, *hshape), vs.reshape(1, bd, sd, *hshape), ssm_s[None], conv_s[None])
```

```python
import functools
import math

import jax
import jax.numpy as jnp
from jax import lax
from jax.experimental import pallas as pl
from jax.experimental.pallas import tpu as pltpu

F32 = jnp.float32
BF16 = jnp.bfloat16

ATT_DH = 64
ATT_HEADS = 8
ATT_WIDTH = ATT_HEADS * 2 * ATT_DH
ATT_SCALE = ATT_DH ** -0.5
SSD_HEADDIM = 64
SSD_HEADS = 16
SSD_WIDTH = SSD_HEADS * SSD_HEADDIM
SSD_GROUPS = 2
D_STATE = 128
CONV_W = 4
CONV_DIM = SSD_WIDTH + 2 * SSD_GROUPS * D_STATE
SSD_CHUNK = 128
N_EXPERTS = 32
TOP_K = 4
SWIGLU_ALPHA = 1.702
SWIGLU_LIMIT = 7.0
LN_EPS = 1e-5
RMS_EPS = 1e-5
PAGE_SIZE = 128

V7X_VMEM_LIMIT_BYTES = 56 * 1024 * 1024
LANES = 128

MOE_ROW_TILE = 1024
MOE_SUB = 256
MOE_F_TILE = 256


def _params(sem, vmem=V7X_VMEM_LIMIT_BYTES):
    return pltpu.CompilerParams(dimension_semantics=sem, vmem_limit_bytes=vmem)


def _sigmoid(x):
    return 1.0 / (1.0 + jnp.exp(-x))


def _silu(x):
    return x * _sigmoid(x)


def _softplus(x):
    return jnp.maximum(x, 0.0) + jnp.log1p(jnp.exp(-jnp.abs(x)))


def _layer_norm(x, g, b):
    mu = jnp.mean(x, axis=-1, keepdims=True)
    xc = x - mu
    var = jnp.mean(xc * xc, axis=-1, keepdims=True)
    return xc * lax.rsqrt(var + LN_EPS) * g + b


def _split3(x):
    a = x.astype(BF16)
    r = x - a.astype(F32)
    b = r.astype(BF16)
    c = (r - b.astype(F32)).astype(BF16)
    return a, b, c


def _lam(lq):
    s01 = jnp.sum(lq[0:1, :] * lq[1:2, :], axis=-1, keepdims=True)
    s23 = jnp.sum(lq[2:3, :] * lq[3:4, :], axis=-1, keepdims=True)
    return jnp.exp(s01) - jnp.exp(s23)


def _cast_kernel(w_ref, o_ref):
    o_ref[...] = w_ref[...].astype(BF16)


def _cast_bf16(w, row_tile=256):
    r, c = w.shape
    rt = min(row_tile, r)
    return pl.pallas_call(
        _cast_kernel,
        grid=(r // rt,),
        in_specs=[pl.BlockSpec((rt, c), lambda i: (i, 0))],
        out_specs=pl.BlockSpec((rt, c), lambda i: (i, 0)),
        out_shape=jax.ShapeDtypeStruct((r, c), BF16),
        compiler_params=_params(("arbitrary",)),
    )(w)


def _inproj_kernel(x_ref, w_ref, q_ref, k_ref, v_ref, z_ref, xbc_ref, dt_ref):
    xb = x_ref[...].astype(BF16)

    def mm(c0, c1):
        return jnp.dot(xb, w_ref[:, c0:c1], preferred_element_type=F32)

    a = ATT_WIDTH
    q_ref[...] = (mm(0, a) * ATT_SCALE).astype(q_ref.dtype)
    k_ref[...] = mm(a, 2 * a)
    v_ref[...] = mm(2 * a, 3 * a)
    z_ref[...] = mm(3 * a, 3 * a + SSD_WIDTH)
    c0 = 3 * a + SSD_WIDTH
    xbc_ref[...] = mm(c0, c0 + CONV_DIM)
    dt_ref[...] = mm(c0 + CONV_DIM, c0 + CONV_DIM + SSD_HEADS)


def _inproj(x2d, w_in_bf16, q_dtype):
    m, d = x2d.shape
    ncol = w_in_bf16.shape[1]
    tm = min(256, m)
    row = lambda i: (i, 0)
    widths = (ATT_WIDTH, ATT_WIDTH, ATT_WIDTH, SSD_WIDTH, CONV_DIM, SSD_HEADS)
    dtypes = (q_dtype, F32, F32, F32, F32, F32)
    return pl.pallas_call(
        _inproj_kernel,
        grid=(m // tm,),
        in_specs=[pl.BlockSpec((tm, d), row),
                  pl.BlockSpec((d, ncol), lambda i: (0, 0), pipeline_mode=pl.Buffered(1))],
        out_specs=[pl.BlockSpec((tm, w), row) for w in widths],
        out_shape=[jax.ShapeDtypeStruct((m, w), dt) for w, dt in zip(widths, dtypes)],
        compiler_params=_params(("arbitrary",)),
    )(x2d, w_in_bf16)


def _attn_prompt_kernel(slopes_ref, qt_ref, k_ref, v_ref, lq_ref, w_ref, o_ref, kb_ref, vt_ref, *, tq, hps,
                        lam_init):
    hp = pl.program_id(1)
    qi = pl.program_id(2)
    hd = 2 * ATT_DH
    vrows = vt_ref.shape[1]

    @pl.when(qi == 0)
    def _():
        kb_ref[...] = k_ref[0].astype(BF16)
        for hh in range(hps):
            vt_ref[hh, 0:hd, :] = v_ref[0, :, hh * hd:(hh + 1) * hd].T.astype(BF16)
            vt_ref[hh, hd:vrows, :] = jnp.ones((vrows - hd, vt_ref.shape[2]), BF16)

    drow = lax.broadcasted_iota(jnp.int32, (hd, tq), 0)
    kr = lax.broadcasted_iota(jnp.int32, (tq, tq), 0)
    qc = lax.broadcasted_iota(jnp.int32, (tq, tq), 1)
    rel = (qc - kr).astype(F32)
    future = kr > qc
    slopes = [slopes_ref[hp * hps + hh] for hh in range(hps)]
    qts = []
    for hh in range(hps):
        qt = qt_ref[0, hh * hd:(hh + 1) * hd, :]
        zero = jnp.zeros_like(qt)
        qts += [jnp.where(drow < ATT_DH, qt, zero), jnp.where(drow >= ATT_DH, qt, zero)]

    def scores(j):
        kb = kb_ref[pl.ds(pl.multiple_of(j * tq, tq), tq), :]
        return tuple(jnp.dot(kb[:, (c // 2) * hd:(c // 2 + 1) * hd], qts[c], preferred_element_type=F32)
                     for c in range(2 * hps))

    def block(j, raw, stats, diag):
        cols = pl.ds(pl.multiple_of(j * tq, tq), tq)
        dist = rel + jnp.full((1, 1), (qi - j) * tq, jnp.int32).astype(F32)
        out = []
        for c in range(2 * hps):
            hh = c // 2
            m, l, a = stats[c]
            s = raw[c] - dist * slopes[hh]
            if diag:
                s = jnp.where(future, -jnp.inf, s)
            mn = jnp.maximum(m, jnp.max(s, axis=0, keepdims=True))
            p = jnp.exp(s - mn)
            al = jnp.exp(m - mn)
            pv = jnp.dot(vt_ref[hh, :, cols], p.astype(BF16), preferred_element_type=F32)
            l = al * l + pv[hd:hd + 1, :]
            a = al * a + pv[0:hd, :]
            out.append((mn, l, a))
        return tuple(out)

    def step(j, stats):
        return block(j, scores(j), stats, False)

    init1 = (jnp.full((1, tq), -1e30, F32), jnp.zeros((1, tq), F32), jnp.zeros((hd, tq), F32))
    stats = lax.fori_loop(0, qi, step, (init1,) * (2 * hps))
    stats = block(qi, scores(qi), stats, True)
    lam = _lam(lq_ref[...]) + lam_init
    for hh in range(hps):
        (_, l0, a0), (_, l1, a1) = stats[2 * hh], stats[2 * hh + 1]
        o = a0 / l0 - lam * (a1 / l1)
        o = o * lax.rsqrt(jnp.mean(o * o, axis=0, keepdims=True) + RMS_EPS) * w_ref[...] * (1.0 - lam_init)
        o_ref[0, :, hh * hd:(hh + 1) * hd] = o.T.astype(BF16)


def _attn_prompt(q, k, v, lambda_qk, subln_w, slopes, lam_init):
    b, t, _ = k.shape
    tq = min(256, t)
    hd = 2 * ATT_DH
    hps = 8
    ones_rows = 16
    qt = jnp.swapaxes(q, 1, 2)
    kern = functools.partial(_attn_prompt_kernel, tq=tq, hps=hps, lam_init=lam_init)
    return pl.pallas_call(
        kern,
        grid=(b, ATT_HEADS // hps, t // tq),
        in_specs=[pl.BlockSpec(memory_space=pltpu.SMEM),
                  pl.BlockSpec((1, hps * hd, tq), lambda bi, h, qi: (bi, h, qi)),
                  pl.BlockSpec((1, t, hps * hd), lambda bi, h, qi: (bi, 0, h)),
                  pl.BlockSpec((1, t, hps * hd), lambda bi, h, qi: (bi, 0, h)),
                  pl.BlockSpec((4, ATT_DH), lambda bi, h, qi: (0, 0)),
                  pl.BlockSpec((hd, 1), lambda bi, h, qi: (0, 0))],
        out_specs=pl.BlockSpec((1, tq, hps * hd), lambda bi, h, qi: (bi, qi, h)),
        out_shape=jax.ShapeDtypeStruct((b, t, ATT_WIDTH), BF16),
        scratch_shapes=[pltpu.VMEM((t, hps * hd), BF16), pltpu.VMEM((hps, hd + ones_rows, t), BF16)],
        compiler_params=_params(("arbitrary", "arbitrary", "arbitrary")),
    )(slopes, qt, k, v, lambda_qk, subln_w.reshape(hd, 1))


def _attn_decode_kernel(pt_ref, q_ref, kn_ref, vn_ref, slope_ref, lq_ref, w_ref, *rest, pps, past_len, lam_init):
    kp_refs, vp_refs = rest[:pps], rest[pps:2 * pps]
    o_ref, qt_ref, s_ref, a_ref, m_ref, snew_ref, anew_ref, acc_ref = rest[2 * pps:]
    ph = pl.program_id(1)
    p = pl.program_id(2)
    n_steps = pl.num_programs(2)
    nh = ATT_HEADS
    nrow = 2 * nh
    hd = 2 * ATT_DH
    plane = PAGE_SIZE * nh
    n_pages = past_len // PAGE_SIZE
    per_vreg = LANES // nh

    def page_lanes(page):
        return pl.ds(pl.multiple_of(page * plane, plane), plane)

    @pl.when((ph == 0) & (p == 0))
    def _():
        q8 = q_ref[0]
        lane = lax.broadcasted_iota(jnp.int32, (nh, hd), 1)
        qt = jnp.concatenate([jnp.where(lane < ATT_DH, q8, 0.0), jnp.where(lane >= ATT_DH, q8, 0.0)], axis=0)
        qt_ref[...] = qt.astype(BF16)
        kn = jnp.concatenate([kn_ref[0], kn_ref[0]], axis=0)
        s_new = jnp.sum(qt * kn, axis=-1, keepdims=True)
        snew_ref[...] = s_new
        m_ref[...] = s_new

    @pl.when(ph == 0)
    def _():
        lane = lax.broadcasted_iota(jnp.int32, (nrow, plane), 1)
        row = lax.broadcasted_iota(jnp.int32, (nrow, plane), 0)
        own_head = (lane % nh) == (row % nh)
        for i in range(pps):
            page = p * pps + i
            kflat = kp_refs[i][0].reshape(plane, hd).astype(BF16)
            s = lax.dot_general(qt_ref[...], kflat, (((1,), (1,)), ((), ())), preferred_element_type=F32)
            dist = (past_len - page * PAGE_SIZE - lane // nh).astype(F32)
            s = jnp.where(own_head, s - slope_ref[...] * dist, -jnp.inf)
            s_ref[:, page_lanes(page)] = s
            m_ref[...] = jnp.maximum(m_ref[...], jnp.max(s, axis=-1, keepdims=True))

    @pl.when((ph == 1) & (p == 0))
    def _():
        m = m_ref[...]
        e_new = jnp.exp(snew_ref[...] - m)

        def expsum(g, part):
            for u in range(pps):
                e = jnp.exp(s_ref[:, page_lanes(g * pps + u)] - m)
                s_ref[:, page_lanes(g * pps + u)] = e
                for c in range(plane // LANES):
                    part = part + e[:, c * LANES:(c + 1) * LANES]
            return part

        part = lax.fori_loop(0, n_pages // pps, expsum, jnp.zeros((nrow, LANES), F32))
        den = jnp.sum(part, axis=-1, keepdims=True) + e_new
        inv = 1.0 / den
        lam = _lam(lq_ref[...]) + lam_init

        def combine(g, _):
            for u in range(pps):
                pn = s_ref[:, page_lanes(g * pps + u)] * inv
                a_ref[:, page_lanes(g * pps + u)] = pn[0:nh] - lam * pn[nh:nrow]
            return 0

        lax.fori_loop(0, n_pages // pps, combine, 0)
        pn_new = e_new * inv
        anew_ref[...] = pn_new[0:nh] - lam * pn_new[nh:nrow]
        acc_ref[...] = jnp.zeros_like(acc_ref)

    @pl.when(ph == 1)
    def _():
        lane = lax.broadcasted_iota(jnp.int32, (nh, LANES), 1)
        n_acc = acc_ref.shape[0]
        for i in range(pps):
            page = p * pps + i

            accs = [acc_ref[k] for k in range(n_acc)]
            for g in range(plane // LANES):
                av = a_ref[:, pl.ds(pl.multiple_of(page * plane + g * LANES, LANES), LANES)]
                for jj in range(per_vreg):
                    sel = (lane >= jj * nh) & (lane < (jj + 1) * nh)
                    wcol = jnp.sum(jnp.where(sel, av, 0.0), axis=-1, keepdims=True)
                    accs[jj % n_acc] = accs[jj % n_acc] + wcol * vp_refs[i][0, g * per_vreg + jj]
            for k in range(n_acc):
                acc_ref[k] = accs[k]

    @pl.when((ph == 1) & (p == n_steps - 1))
    def _():
        o = anew_ref[...] * vn_ref[0]
        for k in range(acc_ref.shape[0]):
            o = o + acc_ref[k]
        o = o * lax.rsqrt(jnp.mean(o * o, axis=-1, keepdims=True) + RMS_EPS) * w_ref[...] * (1.0 - lam_init)
        o_ref[0] = o.astype(BF16)


def _attn_decode(q, k_new, v_new, cache_k, cache_v, page_table, lambda_qk, subln_w, slopes, lam_init):
    bd = q.shape[0]
    n_pages = page_table.shape[1]
    nh = ATT_HEADS
    nrow = 2 * nh
    hd = 2 * ATT_DH
    pps = max(c for c in (1, 2, 4, 8, 16) if n_pages % c == 0)
    n_steps = n_pages // pps
    kp = cache_k.reshape(-1, PAGE_SIZE, nh, hd)
    vp = cache_v.reshape(-1, PAGE_SIZE, nh, hd)
    slope_rows = jnp.tile(slopes, 2).reshape(nrow, 1)
    past_len = n_pages * PAGE_SIZE
    kern = functools.partial(_attn_decode_kernel, pps=pps, past_len=past_len, lam_init=lam_init)
    head3 = lambda b, ph, p, pt: (b, 0, 0)
    const2 = lambda b, ph, p, pt: (0, 0)

    def kpage(i):
        return lambda b, ph, p, pt: (pt[b * n_pages + jnp.where(ph == 0, p, n_steps - 1) * pps + i], 0, 0, 0)

    def vpage(i):
        return lambda b, ph, p, pt: (pt[b * n_pages + jnp.where(ph == 0, 0, p) * pps + i], 0, 0, 0)

    page_block = (1, PAGE_SIZE, nh, hd)
    n_acc = 4
    grid_spec = pltpu.PrefetchScalarGridSpec(
        num_scalar_prefetch=1,
        grid=(bd, 2, n_steps),
        in_specs=[pl.BlockSpec((1, nh, hd), head3),
                  pl.BlockSpec((1, nh, hd), head3),
                  pl.BlockSpec((1, nh, hd), head3),
                  pl.BlockSpec((nrow, 1), const2),
                  pl.BlockSpec((4, ATT_DH), const2),
                  pl.BlockSpec((1, hd), const2)]
                 + [pl.BlockSpec(page_block, kpage(i)) for i in range(pps)]
                 + [pl.BlockSpec(page_block, vpage(i)) for i in range(pps)],
        out_specs=pl.BlockSpec((1, nh, hd), head3),
        scratch_shapes=[pltpu.VMEM((nrow, hd), BF16), pltpu.VMEM((nrow, past_len * nh), F32),
                        pltpu.VMEM((nh, past_len * nh), F32), pltpu.VMEM((nrow, 1), F32),
                        pltpu.VMEM((nrow, 1), F32), pltpu.VMEM((nh, 1), F32),
                        pltpu.VMEM((n_acc, nh, hd), F32)],
    )
    out = pl.pallas_call(
        kern,
        grid_spec=grid_spec,
        out_shape=jax.ShapeDtypeStruct((bd, nh, hd), BF16),
        compiler_params=_params(("arbitrary", "arbitrary", "arbitrary")),
    )(page_table.reshape(-1), q.reshape(bd, nh, hd), k_new.reshape(bd, nh, hd), v_new.reshape(bd, nh, hd),
      slope_rows, lambda_qk, subln_w.reshape(1, hd), *([kp] * pps), *([vp] * pps))
    return out.reshape(bd, ATT_WIDTH)


def _gated_group_norm(y, z, w):
    yg = y * _silu(z)
    gw = SSD_WIDTH // SSD_GROUPS
    parts = []
    for g in range(SSD_GROUPS):
        v = yg[:, g * gw:(g + 1) * gw]
        parts.append(v * lax.rsqrt(jnp.mean(v * v, axis=-1, keepdims=True) + RMS_EPS) * w[:, g * gw:(g + 1) * gw])
    return parts


def _ssd_prompt_kernel(xbc_ref, dt_ref, dtt_ref, z_ref, cw_ref, cb_ref, dtb_ref, dtbt_ref, al_ref, alt_ref,
                       dsk_ref, nw_ref, h0_ref, c0_ref, y_ref, st_ref, cv_ref, xpad_ref, ysc_ref, xdd_ref):
    c = pl.program_id(1)
    nc = pl.num_programs(1)
    L = SSD_CHUNK
    P = SSD_HEADDIM
    hpg = SSD_HEADS // SSD_GROUPS
    halo = 8

    @pl.when(c == 0)
    def _():
        st_ref[...] = h0_ref[...]
        xpad_ref[0:halo, :] = jnp.zeros((halo, CONV_DIM), F32)
        xpad_ref[halo - (CONV_W - 1):halo, :] = c0_ref[0]

    xc = xbc_ref[0]
    xpad_ref[halo:halo + L, :] = xc
    conv = cb_ref[...]
    for i in range(CONV_W - 1):
        sh = CONV_W - 1 - i
        conv = conv + xpad_ref[halo - sh:halo - sh + L, :] * cw_ref[i:i + 1, :]
    conv = conv + xc * cw_ref[CONV_W - 1:CONV_W, :]
    tail = xc[L - (CONV_W - 1):L, :]
    xpad_ref[halo - (CONV_W - 1):halo, :] = tail

    @pl.when(c == nc - 1)
    def _():
        cv_ref[0] = tail

    act = _silu(conv)
    xs = act[:, :SSD_WIDTH]
    bmat = [act[:, SSD_WIDTH + g * D_STATE:SSD_WIDTH + (g + 1) * D_STATE].astype(BF16) for g in range(SSD_GROUPS)]
    c_off = SSD_WIDTH + SSD_GROUPS * D_STATE
    cmat = [act[:, c_off + g * D_STATE:c_off + (g + 1) * D_STATE].astype(BF16) for g in range(SSD_GROUPS)]

    dtp = _softplus(dt_ref[0] + dtb_ref[...])
    dtpt = _softplus(dtt_ref[0] + dtbt_ref[...])
    da = dtp * (-jnp.exp(al_ref[...]))
    dat = dtpt * (-jnp.exp(alt_ref[...]))
    ri = lax.broadcasted_iota(jnp.int32, (L, L), 0)
    ci = lax.broadcasted_iota(jnp.int32, (L, L), 1)
    causal = ri >= ci
    tri = jnp.where(causal, 1.0, 0.0).astype(BF16)
    trit = jnp.where(ci >= ri, 1.0, 0.0).astype(BF16)
    cs = sum(jnp.dot(tri, part, preferred_element_type=F32) for part in _split3(da))
    cst = sum(jnp.dot(part, trit, preferred_element_type=F32) for part in _split3(dat))

    cb = [lax.dot_general(cmat[g], bmat[g], (((1,), (1,)), ((), ())), preferred_element_type=F32)
          for g in range(SSD_GROUPS)]
    dsk = dsk_ref[...]

    for h in range(SSD_HEADS):
        g = h // hpg
        cs_col = cs[:, h:h + 1]
        diff = cs_col - cst[h:h + 1, :]
        lmat = jnp.exp(jnp.where(causal, diff, -jnp.inf))
        mmat = (cb[g] * lmat).astype(BF16)
        xs_h = xs[:, h * P:(h + 1) * P]
        xd_h = xs_h * dtp[:, h:h + 1]
        y = jnp.dot(mmat, xd_h.astype(BF16), preferred_element_type=F32)
        st = st_ref[0, h]
        yoff = lax.dot_general(cmat[g], st.astype(BF16), (((1,), (1,)), ((), ())), preferred_element_type=F32)
        y = y + jnp.exp(cs_col) * yoff + dsk[:, h:h + 1] * xs_h
        ysc_ref[:, h * P:(h + 1) * P] = y
        cs_last = cs[L - 1:L, h:h + 1]
        xdd_ref[:, h * P:(h + 1) * P] = xd_h * jnp.exp(cs_last - cs_col)

    xddt = xdd_ref[...].T
    for h in range(SSD_HEADS):
        g = h // hpg
        new = jnp.dot(xddt[h * P:(h + 1) * P, :].astype(BF16), bmat[g], preferred_element_type=F32)
        cs_last = cs[L - 1:L, h:h + 1]
        st_ref[0, h] = jnp.exp(cs_last) * st_ref[0, h] + new

    parts = _gated_group_norm(ysc_ref[...], z_ref[0], nw_ref[...])
    gw = SSD_WIDTH // SSD_GROUPS
    for g in range(SSD_GROUPS):
        y_ref[0, :, g * gw:(g + 1) * gw] = parts[g].astype(BF16)


def _ssd_prompt(xbc, dt, z, h0, conv0, conv_w, conv_b, dt_bias, a_log, d_skip, norm_w):
    b, t, _ = xbc.shape
    L = SSD_CHUNK
    nc = t // L
    dtt = jnp.swapaxes(dt, 1, 2)
    seq = lambda bi, ci: (bi, ci, 0)
    const2 = lambda bi, ci: (0, 0)
    full2 = lambda shp: pl.BlockSpec(shp, const2)
    return pl.pallas_call(
        _ssd_prompt_kernel,
        grid=(b, nc),
        in_specs=[pl.BlockSpec((1, L, CONV_DIM), seq),
                  pl.BlockSpec((1, L, SSD_HEADS), seq),
                  pl.BlockSpec((1, SSD_HEADS, L), lambda bi, ci: (bi, 0, ci)),
                  pl.BlockSpec((1, L, SSD_WIDTH), seq),
                  full2((CONV_W, CONV_DIM)), full2((1, CONV_DIM)),
                  full2((1, SSD_HEADS)), full2((SSD_HEADS, 1)),
                  full2((1, SSD_HEADS)), full2((SSD_HEADS, 1)),
                  full2((1, SSD_HEADS)), full2((1, SSD_WIDTH)),
                  pl.BlockSpec((1, SSD_HEADS, SSD_HEADDIM, D_STATE), lambda bi, ci: (bi, 0, 0, 0)),
                  pl.BlockSpec((1, CONV_W - 1, CONV_DIM), lambda bi, ci: (bi, 0, 0))],
        out_specs=[pl.BlockSpec((1, L, SSD_WIDTH), seq),
                   pl.BlockSpec((1, SSD_HEADS, SSD_HEADDIM, D_STATE), lambda bi, ci: (bi, 0, 0, 0)),
                   pl.BlockSpec((1, CONV_W - 1, CONV_DIM), lambda bi, ci: (bi, 0, 0))],
        out_shape=[jax.ShapeDtypeStruct((b, t, SSD_WIDTH), BF16),
                   jax.ShapeDtypeStruct((b, SSD_HEADS, SSD_HEADDIM, D_STATE), F32),
                   jax.ShapeDtypeStruct((b, CONV_W - 1, CONV_DIM), F32)],
        scratch_shapes=[pltpu.VMEM((8 + L, CONV_DIM), F32), pltpu.VMEM((L, SSD_WIDTH), F32),
                        pltpu.VMEM((L, SSD_WIDTH), F32)],
        compiler_params=_params(("arbitrary", "arbitrary")),
    )(xbc, dt, dtt, z, conv_w, conv_b.reshape(1, -1), dt_bias.reshape(1, -1), dt_bias.reshape(-1, 1),
      a_log.reshape(1, -1), a_log.reshape(-1, 1), d_skip.reshape(1, -1), norm_w.reshape(1, -1), h0, conv0)


def _bf16_round(v):
    return v.astype(BF16).astype(F32)


def _ssd_sample_kernel(xbc_ref, ci_ref, dt_ref, z_ref, cw_ref, cb_ref, dtb_ref, al_ref, dsk_ref, nw_ref, h0_ref,
                       xbc8_ref, ci8_ref, dt8_ref, cw8_ref, cb8_ref, dtb8_ref, y_ref, st_ref, cv_ref, ysc_ref):
    P = SSD_HEADDIM
    hpg = SSD_HEADS // SSD_GROUPS
    xrow = xbc_ref[0]
    hist = ci_ref[0]
    conv = cb_ref[...]
    for i in range(CONV_W - 1):
        conv = conv + hist[i:i + 1, :] * cw_ref[i:i + 1, :]
    conv = conv + xrow * cw_ref[CONV_W - 1:CONV_W, :]
    cv_ref[0, 0:CONV_W - 2, :] = hist[1:CONV_W - 1, :]
    cv_ref[0, CONV_W - 2:CONV_W - 1, :] = xrow
    act = _silu(conv)
    xs = act[:, :SSD_WIDTH]
    c_off = SSD_WIDTH + SSD_GROUPS * D_STATE
    dtp = _softplus(dt_ref[0] + dtb_ref[...])
    decay = jnp.exp(dtp * (-jnp.exp(al_ref[...])))
    dsk = dsk_ref[...]

    nx = SSD_WIDTH // LANES
    conv8 = cb8_ref[0:nx, :]
    for i in range(CONV_W - 1):
        conv8 = conv8 + ci8_ref[0, i, 0:nx, :] * cw8_ref[i, 0:nx, :]
    conv8 = conv8 + xbc8_ref[0, 0:nx, :] * cw8_ref[CONV_W - 1, 0:nx, :]
    xd8 = _silu(conv8) * _softplus(dt8_ref[0] + dtb8_ref[...])
    xdt = xd8.T

    brows = [act[:, SSD_WIDTH + g * D_STATE:SSD_WIDTH + (g + 1) * D_STATE] for g in range(SSD_GROUPS)]
    crows = [act[:, c_off + g * D_STATE:c_off + (g + 1) * D_STATE] for g in range(SSD_GROUPS)]
    cbs = [jnp.sum(_bf16_round(brows[g]) * _bf16_round(crows[g]), axis=-1, keepdims=True) for g in range(SSD_GROUPS)]
    c8s = [jnp.broadcast_to(crows[g], (8, D_STATE)).astype(BF16) for g in range(SSD_GROUPS)]
    hpr = LANES // P
    for h in range(SSD_HEADS):
        g = h // hpg
        xs_h = xs[:, h * P:(h + 1) * P]
        xd_h = xs_h * dtp[:, h:h + 1]
        xcol = xdt[(h % hpr) * P:(h % hpr + 1) * P, h // hpr:h // hpr + 1]
        dec = decay[:, h:h + 1]
        h0q = h0_ref[0, h].astype(BF16)
        st_ref[0, h] = _bf16_round(dec) * h0q.astype(F32) + _bf16_round(xcol * brows[g])
        yoff = lax.dot_general(c8s[g], h0q, (((1,), (1,)), ((), ())), preferred_element_type=F32)[0:1, :]
        ysc_ref[:, h * P:(h + 1) * P] = (cbs[g] * xd_h + dec * yoff) + dsk[:, h:h + 1] * xs_h
    parts = _gated_group_norm(ysc_ref[...], z_ref[0], nw_ref[...])
    gw = SSD_WIDTH // SSD_GROUPS
    for g in range(SSD_GROUPS):
        y_ref[0, :, g * gw:(g + 1) * gw] = parts[g].astype(BF16)


def _ssd_sample(xbc, dt, z, h0, conv0, conv_w, conv_b, dt_bias, a_log, d_skip, norm_w):
    bd = xbc.shape[0]
    nr = CONV_DIM // LANES
    nx = SSD_WIDTH // LANES
    rep = SSD_HEADDIM
    row3 = lambda b: (b, 0, 0)
    const2 = lambda b: (0, 0)
    full2 = lambda shp: pl.BlockSpec(shp, const2)
    state = pl.BlockSpec((1, SSD_HEADS, SSD_HEADDIM, D_STATE), lambda b: (b, 0, 0, 0))
    y, st, cv = pl.pallas_call(
        _ssd_sample_kernel,
        grid=(bd,),
        in_specs=[pl.BlockSpec((1, 1, CONV_DIM), row3),
                  pl.BlockSpec((1, CONV_W - 1, CONV_DIM), row3),
                  pl.BlockSpec((1, 1, SSD_HEADS), row3),
                  pl.BlockSpec((1, 1, SSD_WIDTH), row3),
                  full2((CONV_W, CONV_DIM)), full2((1, CONV_DIM)), full2((1, SSD_HEADS)), full2((1, SSD_HEADS)),
                  full2((1, SSD_HEADS)), full2((1, SSD_WIDTH)), state,
                  pl.BlockSpec((1, nr, LANES), row3),
                  pl.BlockSpec((1, CONV_W - 1, nr, LANES), lambda b: (b, 0, 0, 0)),
                  pl.BlockSpec((1, nx, LANES), row3),
                  pl.BlockSpec((CONV_W, nr, LANES), lambda b: (0, 0, 0)),
                  full2((nr, LANES)), full2((nx, LANES))],
        out_specs=[pl.BlockSpec((1, 1, SSD_WIDTH), row3), state,
                   pl.BlockSpec((1, CONV_W - 1, CONV_DIM), row3)],
        out_shape=[jax.ShapeDtypeStruct((bd, 1, SSD_WIDTH), BF16),
                   jax.ShapeDtypeStruct((bd, SSD_HEADS, SSD_HEADDIM, D_STATE), F32),
                   jax.ShapeDtypeStruct((bd, CONV_W - 1, CONV_DIM), F32)],
        scratch_shapes=[pltpu.VMEM((1, SSD_WIDTH), F32)],
        compiler_params=_params(("arbitrary",)),
    )(xbc.reshape(bd, 1, CONV_DIM), conv0, dt.reshape(bd, 1, SSD_HEADS), z.reshape(bd, 1, SSD_WIDTH),
      conv_w, conv_b.reshape(1, -1), dt_bias.reshape(1, -1), a_log.reshape(1, -1), d_skip.reshape(1, -1),
      norm_w.reshape(1, -1), h0,
      xbc.reshape(bd, nr, LANES), conv0.reshape(bd, CONV_W - 1, nr, LANES),
      jnp.repeat(dt, rep, axis=-1).reshape(bd, nx, LANES), conv_w.reshape(CONV_W, nr, LANES),
      conv_b.reshape(nr, LANES), jnp.repeat(dt_bias, rep).reshape(nx, LANES))
    return y.reshape(bd, SSD_WIDTH), st, cv


def _outproj_kernel(att_ref, ssd_ref, x_ref, w_ref, g_ref, b_ref, wr_ref, br_ref, h_ref, route_ref, cnt_ref,
                    carry_ref, *, alpha, sub):
    i = pl.program_id(0)
    tm = x_ref.shape[0]

    @pl.when(i == 0)
    def _():
        carry_ref[...] = jnp.zeros_like(carry_ref)

    lane = lax.broadcasted_iota(jnp.int32, (sub, N_EXPERTS), 1).astype(F32)
    ri = lax.broadcasted_iota(jnp.int32, (sub, sub), 0)
    ci = lax.broadcasted_iota(jnp.int32, (sub, sub), 1)
    before = jnp.where(ci < ri, 1.0, 0.0).astype(BF16)
    olane = lax.broadcasted_iota(jnp.int32, (sub, LANES), 1)
    wrb = wr_ref[...].astype(BF16)
    carry = carry_ref[...]
    for r0 in range(0, tm, sub):
        rows = pl.ds(r0, sub)
        mix = jnp.dot(att_ref[rows, :], w_ref[0:ATT_WIDTH, :], preferred_element_type=F32)
        mix = mix + jnp.dot(ssd_ref[rows, :], w_ref[ATT_WIDTH:ATT_WIDTH + SSD_WIDTH, :], preferred_element_type=F32)
        hval = _layer_norm(alpha * x_ref[rows, :] + mix, g_ref[...], b_ref[...])
        h_ref[rows, :] = hval

        logits = jnp.dot(hval.astype(BF16), wrb, preferred_element_type=F32) + br_ref[...]
        work = logits
        chosen = jnp.zeros((sub, N_EXPERTS), F32)
        vals, idxs = [], []
        for _ in range(TOP_K):
            mk = jnp.max(work, axis=-1, keepdims=True)
            ik = jnp.min(jnp.where(work == mk, lane, float(N_EXPERTS)), axis=-1, keepdims=True)
            sel = lane == ik
            work = jnp.where(sel, -jnp.inf, work)
            chosen = jnp.where(sel, 1.0, chosen)
            vals.append(mk)
            idxs.append(ik)
        es = [jnp.exp(v - vals[0]) for v in vals]
        den = es[0] + es[1] + es[2] + es[3]

        prefix = jnp.dot(before, chosen.astype(BF16), preferred_element_type=F32) + carry
        carry = carry + jnp.sum(chosen, axis=0, keepdims=True)

        route = jnp.zeros((sub, LANES), F32)
        for k in range(TOP_K):
            rank_k = jnp.sum(jnp.where(lane == idxs[k], prefix, 0.0), axis=-1, keepdims=True)
            route = jnp.where(olane == k, idxs[k], route)
            route = jnp.where(olane == TOP_K + k, es[k] / den, route)
            route = jnp.where(olane == 2 * TOP_K + k, rank_k, route)
        route_ref[rows, :] = route
    carry_ref[...] = carry
    cnt_ref[...] = carry


def _outproj_router(att, ssd, x2d, w_out_bf16, ln_g, ln_b, w_router, b_router, alpha):
    m, d = x2d.shape
    sub = min(256, m)
    tm = min(2 * sub, m)
    row = lambda i: (i, 0)
    const = lambda i: (0, 0)
    kern = functools.partial(_outproj_kernel, alpha=alpha, sub=sub)
    return pl.pallas_call(
        kern,
        grid=(m // tm,),
        in_specs=[pl.BlockSpec((tm, ATT_WIDTH), row), pl.BlockSpec((tm, SSD_WIDTH), row), pl.BlockSpec((tm, d), row),
                  pl.BlockSpec((ATT_WIDTH + SSD_WIDTH, d), const, pipeline_mode=pl.Buffered(1)),
                  pl.BlockSpec((1, d), const), pl.BlockSpec((1, d), const),
                  pl.BlockSpec((d, N_EXPERTS), const), pl.BlockSpec((1, N_EXPERTS), const)],
        out_specs=[pl.BlockSpec((tm, d), row), pl.BlockSpec((tm, LANES), row), pl.BlockSpec((1, N_EXPERTS), const)],
        out_shape=[jax.ShapeDtypeStruct((m, d), F32), jax.ShapeDtypeStruct((m, LANES), F32),
                   jax.ShapeDtypeStruct((1, N_EXPERTS), F32)],
        scratch_shapes=[pltpu.VMEM((1, N_EXPERTS), F32)],
        compiler_params=_params(("arbitrary",)),
    )(att, ssd, x2d, w_out_bf16, ln_g.reshape(1, d), ln_b.reshape(1, d), w_router, b_router.reshape(1, -1))


def _row_copy(src, dst, sem):
    return pltpu.make_async_copy(src, dst, sem)


def _scatter_kernel(cnt_ref, pst_ref, dest_ref, h_ref, *rest, first):
    xrows_ref, zero_ref, stage_ref, sems, zsem = rest[-5:]
    i = pl.program_id(0)
    tm = h_ref.shape[0]

    @pl.when((i == 0) & first)
    def _():
        zero_ref[...] = jnp.zeros_like(zero_ref)

        def per_expert(e, _):
            n = cnt_ref[e]
            base = pst_ref[e]
            end = (n + MOE_SUB - 1) // MOE_SUB * MOE_SUB

            def start(r, _):
                _row_copy(zero_ref.at[pl.ds(0, 1)], xrows_ref.at[pl.ds(base + r, 1)], zsem).start()
                return 0

            def wait(r, _):
                _row_copy(zero_ref.at[pl.ds(0, 1)], xrows_ref.at[pl.ds(base + r, 1)], zsem).wait()
                return 0

            lax.fori_loop(n, end, start, 0)
            lax.fori_loop(n, end, wait, 0)
            return 0

        lax.fori_loop(0, N_EXPERTS, per_expert, 0)

    slot = i % 2
    stage_ref[slot] = h_ref[...]
    for t in range(tm):
        for k in range(TOP_K):
            d = dest_ref[0, 0, t * TOP_K + k]
            _row_copy(stage_ref.at[slot, pl.ds(t, 1)], xrows_ref.at[pl.ds(d, 1)], sems.at[slot]).start()

    def drain(s):
        for t in range(tm):
            for k in range(TOP_K):
                _row_copy(stage_ref.at[s, pl.ds(t, 1)], xrows_ref.at[pl.ds(0, 1)], sems.at[s]).wait()

    @pl.when(i > 0)
    def _():
        drain(1 - slot)

    @pl.when(i == pl.num_programs(0) - 1)
    def _():
        drain(slot)


def _moe_scatter(h2d, dest, counts, pstart, n_rows, x_rows=None):
    m, d = h2d.shape
    tm = min(256, m)
    nt = m // tm
    dest3 = dest.reshape(nt, 1, tm * TOP_K)
    first = x_rows is None
    in_specs = [pl.BlockSpec((1, 1, tm * TOP_K), lambda i, c, p: (i, 0, 0), memory_space=pltpu.SMEM),
                pl.BlockSpec((tm, d), lambda i, c, p: (i, 0))]
    args = [counts, pstart, dest3, h2d]
    aliases = {}
    if not first:
        in_specs.append(pl.BlockSpec(memory_space=pl.ANY))
        args.append(x_rows)
        aliases = {len(args) - 1: 0}
    grid_spec = pltpu.PrefetchScalarGridSpec(
        num_scalar_prefetch=2,
        grid=(nt,),
        in_specs=in_specs,
        out_specs=pl.BlockSpec(memory_space=pl.ANY),
        scratch_shapes=[pltpu.VMEM((8, d), F32), pltpu.VMEM((2, tm, d), F32), pltpu.SemaphoreType.DMA((2,)),
                        pltpu.SemaphoreType.DMA(())],
    )
    return pl.pallas_call(
        functools.partial(_scatter_kernel, first=first),
        grid_spec=grid_spec,
        out_shape=jax.ShapeDtypeStruct((n_rows, d), F32),
        input_output_aliases=aliases,
        compiler_params=_params(("arbitrary",)),
    )(*args)


def _moe_mlp_kernel(ie_ref, ib_ref, iv_ref, x_ref, w1g_ref, w1l_ref, b1g_ref, b1l_ref, w2_ref, b2_ref, o_ref,
                    xb_ref, wg_ref, wl_ref, w2b_ref):
    i = pl.program_id(0)
    j = pl.program_id(1)
    nvalid = iv_ref[i]
    nsub = (nvalid + MOE_SUB - 1) // MOE_SUB
    d = x_ref.shape[1]

    def rows(s):
        return pl.ds(pl.multiple_of(s * MOE_SUB, MOE_SUB), MOE_SUB)

    @pl.when(nvalid > 0)
    def _():
        @pl.when(j == 0)
        def _():
            def init(s, _):
                xb_ref[rows(s), :] = x_ref[rows(s), :].astype(BF16)
                o_ref[rows(s), :] = jnp.broadcast_to(b2_ref[0], (MOE_SUB, d))
                return 0

            lax.fori_loop(0, nsub, init, 0)

        def hidden(s):
            xb = xb_ref[rows(s), :]
            return (jnp.dot(xb, wg_ref[...], preferred_element_type=F32),
                    jnp.dot(xb, wl_ref[...], preferred_element_type=F32))

        def finish(s, hid):
            glu = jnp.minimum(hid[0] + b1g_ref[0], SWIGLU_LIMIT)
            lin = jnp.clip(hid[1] + b1l_ref[0], -SWIGLU_LIMIT, SWIGLU_LIMIT)
            act = glu * _sigmoid(SWIGLU_ALPHA * glu) * (lin + 1.0)
            o_ref[rows(s), :] += jnp.dot(act.astype(BF16), w2b_ref[...], preferred_element_type=F32)

        def step(s, hid):
            nxt = hidden(s + 1)
            finish(s, hid)
            return nxt

        xb0 = xb_ref[rows(0), :]
        wg_ref[...] = w1g_ref[0].astype(BF16)
        glu0 = jnp.dot(xb0, wg_ref[...], preferred_element_type=F32)
        wl_ref[...] = w1l_ref[0].astype(BF16)
        lin0 = jnp.dot(xb0, wl_ref[...], preferred_element_type=F32)
        w2b_ref[...] = w2_ref[0].astype(BF16)
        hid = (glu0, lin0)
        hid = lax.fori_loop(0, nsub - 1, step, hid)
        finish(nsub - 1, hid)


def _moe_mlp(x_rows, item_e, item_blk, item_valid, w1, b1, w2, b2):
    n_rows, d = x_rows.shape
    n_items = item_e.shape[0]
    d_ff = w2.shape[1]
    tf = MOE_F_TILE
    nf = d_ff // tf
    tmr = MOE_ROW_TILE

    def jj(i, j, iv):
        return jnp.where(iv[i] > 0, j, nf - 1)

    grid_spec = pltpu.PrefetchScalarGridSpec(
        num_scalar_prefetch=3,
        grid=(n_items, nf),
        in_specs=[pl.BlockSpec((tmr, d), lambda i, j, ie, ib, iv: (ib[i], 0)),
                  pl.BlockSpec((1, d, tf), lambda i, j, ie, ib, iv: (ie[i], 0, jj(i, j, iv))),
                  pl.BlockSpec((1, d, tf), lambda i, j, ie, ib, iv: (ie[i], 0, nf + jj(i, j, iv))),
                  pl.BlockSpec((1, 1, tf), lambda i, j, ie, ib, iv: (ie[i], 0, jj(i, j, iv))),
                  pl.BlockSpec((1, 1, tf), lambda i, j, ie, ib, iv: (ie[i], 0, nf + jj(i, j, iv))),
                  pl.BlockSpec((1, tf, d), lambda i, j, ie, ib, iv: (ie[i], jj(i, j, iv), 0)),
                  pl.BlockSpec((1, 1, d), lambda i, j, ie, ib, iv: (ie[i], 0, 0))],
        out_specs=pl.BlockSpec((tmr, d), lambda i, j, ie, ib, iv: (ib[i], 0)),
        scratch_shapes=[pltpu.VMEM((tmr, d), BF16), pltpu.VMEM((d, tf), BF16), pltpu.VMEM((d, tf), BF16),
                        pltpu.VMEM((tf, d), BF16)],
    )
    return pl.pallas_call(
        _moe_mlp_kernel,
        grid_spec=grid_spec,
        out_shape=jax.ShapeDtypeStruct((n_rows, d), F32),
        compiler_params=_params(("arbitrary", "arbitrary")),
    )(item_e, item_blk, item_valid, x_rows, w1, w1, b1.reshape(N_EXPERTS, 1, -1), b1.reshape(N_EXPERTS, 1, -1),
      w2, b2.reshape(N_EXPERTS, 1, -1))


def _combine_kernel(dest_ref, nxt_ref, h_ref, route_ref, yrows_ref, g_ref, b_ref, o_ref, buf_ref, sems, *, alpha):
    i = pl.program_id(0)
    nt = pl.num_programs(0)
    tm = h_ref.shape[0]
    slot = i % 2

    def gather(idx_ref, s):
        for t in range(tm):
            for k in range(TOP_K):
                d = idx_ref[0, 0, t * TOP_K + k]
                _row_copy(yrows_ref.at[pl.ds(d, 1)], buf_ref.at[s, k, pl.ds(t, 1)], sems.at[s]).start()

    @pl.when(i == 0)
    def _():
        gather(dest_ref, 0)

    @pl.when(i + 1 < nt)
    def _():
        gather(nxt_ref, 1 - slot)

    for t in range(tm):
        for k in range(TOP_K):
            _row_copy(yrows_ref.at[pl.ds(0, 1)], buf_ref.at[slot, k, pl.ds(t, 1)], sems.at[slot]).wait()
    route = route_ref[...]
    acc = route[:, TOP_K:TOP_K + 1] * buf_ref[slot, 0]
    for k in range(1, TOP_K):
        acc = acc + route[:, TOP_K + k:TOP_K + k + 1] * buf_ref[slot, k]
    o_ref[...] = _layer_norm(alpha * h_ref[...] + acc, g_ref[...], b_ref[...])


def _moe_combine(h2d, route, dest, y_rows, ln_g, ln_b, alpha):
    m, d = h2d.shape
    tm = min(128, m)
    nt = m // tm
    dest3 = dest.reshape(nt, 1, tm * TOP_K)
    kern = functools.partial(_combine_kernel, alpha=alpha)
    idx_block = (1, 1, tm * TOP_K)
    return pl.pallas_call(
        kern,
        grid=(nt,),
        in_specs=[pl.BlockSpec(idx_block, lambda i: (i, 0, 0), memory_space=pltpu.SMEM),
                  pl.BlockSpec(idx_block, lambda i: (jnp.minimum(i + 1, nt - 1), 0, 0), memory_space=pltpu.SMEM),
                  pl.BlockSpec((tm, d), lambda i: (i, 0)),
                  pl.BlockSpec((tm, LANES), lambda i: (i, 0)),
                  pl.BlockSpec(memory_space=pl.ANY),
                  pl.BlockSpec((1, d), lambda i: (0, 0)), pl.BlockSpec((1, d), lambda i: (0, 0))],
        out_specs=pl.BlockSpec((tm, d), lambda i: (i, 0)),
        out_shape=jax.ShapeDtypeStruct((m, d), F32),
        scratch_shapes=[pltpu.VMEM((2, TOP_K, tm, d), F32), pltpu.SemaphoreType.DMA((2,))],
        compiler_params=_params(("arbitrary",)),
    )(dest3, dest3, h2d, route, y_rows, ln_g.reshape(1, d), ln_b.reshape(1, d))


def _moe_ln2(groups, w1, b1, w2, b2, ln_g, ln_b, alpha):
    tmr = MOE_ROW_TILE
    group_counts = [c.reshape(-1).astype(jnp.int32) for _, _, c in groups]
    counts = sum(group_counts)
    tiles = (counts + tmr - 1) // tmr
    tile_end = jnp.cumsum(tiles)
    tile_start = tile_end - tiles
    pstart = (tile_start * tmr).astype(jnp.int32)
    n_tok = sum(h.shape[0] for h, _, _ in groups)
    n_items = -(-(n_tok * TOP_K) // tmr) + N_EXPERTS
    n_rows = n_items * tmr
    it = jnp.arange(n_items, dtype=jnp.int32)
    total = tile_end[-1]
    it_c = jnp.minimum(it, total - 1)
    item_e = jnp.minimum(jnp.searchsorted(tile_end, it_c, side='right'), N_EXPERTS - 1).astype(jnp.int32)
    item_r = it_c - tile_start[item_e]
    item_blk = (tile_start[item_e] + item_r).astype(jnp.int32)
    item_valid = jnp.where(it < total, jnp.clip(counts[item_e] - item_r * tmr, 0, tmr), 0).astype(jnp.int32)

    dests = []
    earlier = jnp.zeros_like(counts)
    x_rows = None
    for (h2d, route, _), gc in zip(groups, group_counts):
        idx = route[:, 0:TOP_K].astype(jnp.int32)
        rank = route[:, 2 * TOP_K:3 * TOP_K].astype(jnp.int32)
        dest = (pstart[idx] + earlier[idx] + rank).reshape(-1)
        dests.append(dest)
        earlier = earlier + gc
        x_rows = _moe_scatter(h2d, dest, counts, pstart, n_rows, x_rows)
    y_rows = _moe_mlp(x_rows, item_e, item_blk, item_valid, w1, b1, w2, b2)
    return [_moe_combine(h2d, route, dest, y_rows, ln_g, ln_b, alpha)
            for (h2d, route, _), dest in zip(groups, dests)]


def kernel(x_prompt, x_sample, cache_k, cache_v, page_table, state_ssm, state_conv, w_in, w_out, lambda_qk,
           attn_subln_w, conv_w, conv_b, dt_bias, a_log, d_skip, ssd_norm_w, ln1_g, ln1_b, w_router, b_router,
           w_mlp1, b_mlp1, w_mlp2, b_mlp2, ln2_g, ln2_b):
    bp, sp, d = x_prompt.shape
    bd, sd, _ = x_sample.shape
    depth = w_in.shape[0]
    assert depth == 1 and sd == 1, "kernel supports the single-layer, single-token-decode configuration"
    alpha = (2.0 * depth) ** 0.25
    slopes = 2.0 ** (-8.0 * jnp.arange(1, ATT_HEADS + 1, dtype=F32) / ATT_HEADS)
    l = 0
    lam_init = 0.8 - 0.6 * math.exp(-0.3 * l)

    w_in_b = _cast_bf16(w_in[l])
    w_out_b = _cast_bf16(w_out[l])
    moe_w = (w_mlp1[l], b_mlp1[l], w_mlp2[l], b_mlp2[l], ln2_g[l], ln2_b[l])
    ssd_w = (conv_w[l], conv_b[l], dt_bias[l], a_log[l], d_skip[l], ssd_norm_w[l])

    xp = x_prompt.reshape(bp * sp, d)
    q, k, v, z, xbc, dt = _inproj(xp, w_in_b, BF16)
    att = _attn_prompt(q.reshape(bp, sp, -1), k.reshape(bp, sp, -1), v.reshape(bp, sp, -1), lambda_qk[l],
                       attn_subln_w[l], slopes, lam_init)
    ssm_zero = jnp.zeros((bp, SSD_HEADS, SSD_HEADDIM, D_STATE), F32)
    conv_zero = jnp.zeros((bp, CONV_W - 1, CONV_DIM), F32)
    ssd, ssm_p, conv_p = _ssd_prompt(xbc.reshape(bp, sp, -1), dt.reshape(bp, sp, -1), z.reshape(bp, sp, -1),
                                     ssm_zero, conv_zero, *ssd_w)
    h, route, counts = _outproj_router(att.reshape(bp * sp, -1), ssd.reshape(bp * sp, -1), xp, w_out_b,
                                       ln1_g[l], ln1_b[l], w_router[l], b_router[l], alpha)

    xs = x_sample.reshape(bd, d)
    qs, ks, vs, zs, xbcs, dts = _inproj(xs, w_in_b, F32)
    att_s = _attn_decode(qs, ks, vs, cache_k, cache_v, page_table, lambda_qk[l], attn_subln_w[l], slopes,
                         lam_init)
    ssd_s, ssm_s, conv_s = _ssd_sample(xbcs, dts, zs, state_ssm[l], state_conv[l], *ssd_w)
    hs, route_s, counts_s = _outproj_router(att_s, ssd_s, xs, w_out_b, ln1_g[l], ln1_b[l], w_router[l],
                                            b_router[l], alpha)
    y_prompt, y_sample = _moe_ln2([(h, route, counts), (hs, route_s, counts_s)], *moe_w, alpha)
    y_prompt = y_prompt.reshape(bp, sp, d)
    y_sample = y_sample.reshape(bd, sd, d)

    hshape = (ATT_HEADS, 2 * ATT_DH)
    return (y_prompt, y_sample,
            k.reshape(1, bp, sp, *hshape), v.reshape(1, bp, sp, *hshape), ssm_p[None], conv_p[None],
            ks.reshape(1, bd, sd, *hshape), vs.reshape(1, bd, sd, *hshape), ssm_s[None], conv_s[None])
```

```python
import functools
import math

import jax
import jax.numpy as jnp
from jax import lax
from jax.experimental import pallas as pl
from jax.experimental.pallas import tpu as pltpu

F32 = jnp.float32
BF16 = jnp.bfloat16

ATT_DH = 64
ATT_HEADS = 8
ATT_WIDTH = ATT_HEADS * 2 * ATT_DH
ATT_SCALE = ATT_DH ** -0.5
SSD_HEADDIM = 64
SSD_HEADS = 16
SSD_WIDTH = SSD_HEADS * SSD_HEADDIM
SSD_GROUPS = 2
D_STATE = 128
CONV_W = 4
CONV_DIM = SSD_WIDTH + 2 * SSD_GROUPS * D_STATE
SSD_CHUNK = 128
N_EXPERTS = 32
TOP_K = 4
SWIGLU_ALPHA = 1.702
SWIGLU_LIMIT = 7.0
LN_EPS = 1e-5
RMS_EPS = 1e-5
PAGE_SIZE = 128

V7X_VMEM_LIMIT_BYTES = 56 * 1024 * 1024
LANES = 128

MOE_ROW_TILE = 1024
MOE_SUB = 256
MOE_F_TILE = 256


def _params(sem, vmem=V7X_VMEM_LIMIT_BYTES):
    return pltpu.CompilerParams(dimension_semantics=sem, vmem_limit_bytes=vmem)


def _sigmoid(x):
    return 1.0 / (1.0 + jnp.exp(-x))


def _silu(x):
    return x * _sigmoid(x)


def _softplus(x):
    return jnp.maximum(x, 0.0) + jnp.log1p(jnp.exp(-jnp.abs(x)))


def _layer_norm(x, g, b):
    mu = jnp.mean(x, axis=-1, keepdims=True)
    xc = x - mu
    var = jnp.mean(xc * xc, axis=-1, keepdims=True)
    return xc * lax.rsqrt(var + LN_EPS) * g + b


def _split3(x):
    a = x.astype(BF16)
    r = x - a.astype(F32)
    b = r.astype(BF16)
    c = (r - b.astype(F32)).astype(BF16)
    return a, b, c


def _lam(lq):
    s01 = jnp.sum(lq[0:1, :] * lq[1:2, :], axis=-1, keepdims=True)
    s23 = jnp.sum(lq[2:3, :] * lq[3:4, :], axis=-1, keepdims=True)
    return jnp.exp(s01) - jnp.exp(s23)


def _cast_kernel(w_ref, o_ref):
    o_ref[...] = w_ref[...].astype(BF16)


def _cast_bf16(w, row_tile=256):
    r, c = w.shape
    rt = min(row_tile, r)
    return pl.pallas_call(
        _cast_kernel,
        grid=(r // rt,),
        in_specs=[pl.BlockSpec((rt, c), lambda i: (i, 0))],
        out_specs=pl.BlockSpec((rt, c), lambda i: (i, 0)),
        out_shape=jax.ShapeDtypeStruct((r, c), BF16),
        compiler_params=_params(("arbitrary",)),
    )(w)


def _inproj_kernel(x_ref, w_ref, q_ref, k_ref, v_ref, z_ref, xbc_ref, dt_ref):
    xb = x_ref[...].astype(BF16)

    def mm(c0, c1):
        return jnp.dot(xb, w_ref[:, c0:c1], preferred_element_type=F32)

    a = ATT_WIDTH
    q_ref[...] = (mm(0, a) * ATT_SCALE).astype(q_ref.dtype)
    k_ref[...] = mm(a, 2 * a)
    v_ref[...] = mm(2 * a, 3 * a)
    z_ref[...] = mm(3 * a, 3 * a + SSD_WIDTH)
    c0 = 3 * a + SSD_WIDTH
    xbc_ref[...] = mm(c0, c0 + CONV_DIM)
    dt_ref[...] = mm(c0 + CONV_DIM, c0 + CONV_DIM + SSD_HEADS)


def _inproj(x2d, w_in_bf16, q_dtype):
    m, d = x2d.shape
    ncol = w_in_bf16.shape[1]
    tm = min(256, m)
    row = lambda i: (i, 0)
    widths = (ATT_WIDTH, ATT_WIDTH, ATT_WIDTH, SSD_WIDTH, CONV_DIM, SSD_HEADS)
    dtypes = (q_dtype, F32, F32, F32, F32, F32)
    return pl.pallas_call(
        _inproj_kernel,
        grid=(m // tm,),
        in_specs=[pl.BlockSpec((tm, d), row),
                  pl.BlockSpec((d, ncol), lambda i: (0, 0), pipeline_mode=pl.Buffered(1))],
        out_specs=[pl.BlockSpec((tm, w), row) for w in widths],
        out_shape=[jax.ShapeDtypeStruct((m, w), dt) for w, dt in zip(widths, dtypes)],
        compiler_params=_params(("arbitrary",)),
    )(x2d, w_in_bf16)


def _attn_prompt_kernel(slopes_ref, qt_ref, k_ref, v_ref, lq_ref, w_ref, o_ref, kb_ref, vt_ref, *, tq, hps,
                        lam_init):
    hp = pl.program_id(1)
    qi = pl.program_id(2)
    hd = 2 * ATT_DH
    vrows = vt_ref.shape[1]

    @pl.when(qi == 0)
    def _():
        kb_ref[...] = k_ref[0].astype(BF16)
        for hh in range(hps):
            vt_ref[hh, 0:hd, :] = v_ref[0, :, hh * hd:(hh + 1) * hd].T.astype(BF16)
            vt_ref[hh, hd:vrows, :] = jnp.ones((vrows - hd, vt_ref.shape[2]), BF16)

    drow = lax.broadcasted_iota(jnp.int32, (hd, tq), 0)
    kr = lax.broadcasted_iota(jnp.int32, (tq, tq), 0)
    qc = lax.broadcasted_iota(jnp.int32, (tq, tq), 1)
    rel = (qc - kr).astype(F32)
    future = kr > qc
    slopes = [slopes_ref[hp * hps + hh] for hh in range(hps)]
    qts = []
    for hh in range(hps):
        qt = qt_ref[0, hh * hd:(hh + 1) * hd, :]
        zero = jnp.zeros_like(qt)
        qts += [jnp.where(drow < ATT_DH, qt, zero), jnp.where(drow >= ATT_DH, qt, zero)]

    def scores(j):
        kb = kb_ref[pl.ds(pl.multiple_of(j * tq, tq), tq), :]
        return tuple(jnp.dot(kb[:, (c // 2) * hd:(c // 2 + 1) * hd], qts[c], preferred_element_type=F32)
                     for c in range(2 * hps))

    def block(j, raw, stats, diag):
        cols = pl.ds(pl.multiple_of(j * tq, tq), tq)
        dist = rel + jnp.full((1, 1), (qi - j) * tq, jnp.int32).astype(F32)
        out = []
        for c in range(2 * hps):
            hh = c // 2
            m, l, a = stats[c]
            s = raw[c] - dist * slopes[hh]
            if diag:
                s = jnp.where(future, -jnp.inf, s)
            mn = jnp.maximum(m, jnp.max(s, axis=0, keepdims=True))
            p = jnp.exp(s - mn)
            al = jnp.exp(m - mn)
            pv = jnp.dot(vt_ref[hh, :, cols], p.astype(BF16), preferred_element_type=F32)
            l = al * l + pv[hd:hd + 1, :]
            a = al * a + pv[0:hd, :]
            out.append((mn, l, a))
        return tuple(out)

    def step(j, stats):
        return block(j, scores(j), stats, False)

    init1 = (jnp.full((1, tq), -1e30, F32), jnp.zeros((1, tq), F32), jnp.zeros((hd, tq), F32))
    stats = lax.fori_loop(0, qi, step, (init1,) * (2 * hps))
    stats = block(qi, scores(qi), stats, True)
    lam = _lam(lq_ref[...]) + lam_init
    for hh in range(hps):
        (_, l0, a0), (_, l1, a1) = stats[2 * hh], stats[2 * hh + 1]
        o = a0 / l0 - lam * (a1 / l1)
        o = o * lax.rsqrt(jnp.mean(o * o, axis=0, keepdims=True) + RMS_EPS) * w_ref[...] * (1.0 - lam_init)
        o_ref[0, :, hh * hd:(hh + 1) * hd] = o.T.astype(BF16)


def _attn_prompt(q, k, v, lambda_qk, subln_w, slopes, lam_init):
    b, t, _ = k.shape
    tq = min(256, t)
    hd = 2 * ATT_DH
    hps = 8
    ones_rows = 16
    qt = jnp.swapaxes(q, 1, 2)
    kern = functools.partial(_attn_prompt_kernel, tq=tq, hps=hps, lam_init=lam_init)
    return pl.pallas_call(
        kern,
        grid=(b, ATT_HEADS // hps, t // tq),
        in_specs=[pl.BlockSpec(memory_space=pltpu.SMEM),
                  pl.BlockSpec((1, hps * hd, tq), lambda bi, h, qi: (bi, h, qi)),
                  pl.BlockSpec((1, t, hps * hd), lambda bi, h, qi: (bi, 0, h)),
                  pl.BlockSpec((1, t, hps * hd), lambda bi, h, qi: (bi, 0, h)),
                  pl.BlockSpec((4, ATT_DH), lambda bi, h, qi: (0, 0)),
                  pl.BlockSpec((hd, 1), lambda bi, h, qi: (0, 0))],
        out_specs=pl.BlockSpec((1, tq, hps * hd), lambda bi, h, qi: (bi, qi, h)),
        out_shape=jax.ShapeDtypeStruct((b, t, ATT_WIDTH), BF16),
        scratch_shapes=[pltpu.VMEM((t, hps * hd), BF16), pltpu.VMEM((hps, hd + ones_rows, t), BF16)],
        compiler_params=_params(("arbitrary", "arbitrary", "arbitrary")),
    )(slopes, qt, k, v, lambda_qk, subln_w.reshape(hd, 1))


def _attn_decode_kernel(pt_ref, q_ref, kn_ref, vn_ref, slope_ref, lq_ref, w_ref, *rest, pps, past_len, lam_init):
    kp_refs, vp_refs = rest[:pps], rest[pps:2 * pps]
    o_ref, qt_ref, s_ref, a_ref, m_ref, snew_ref, anew_ref, acc_ref = rest[2 * pps:]
    ph = pl.program_id(1)
    p = pl.program_id(2)
    n_steps = pl.num_programs(2)
    nh = ATT_HEADS
    nrow = 2 * nh
    hd = 2 * ATT_DH
    plane = PAGE_SIZE * nh
    n_pages = past_len // PAGE_SIZE
    per_vreg = LANES // nh

    def page_lanes(page):
        return pl.ds(pl.multiple_of(page * plane, plane), plane)

    @pl.when((ph == 0) & (p == 0))
    def _():
        q8 = q_ref[0]
        lane = lax.broadcasted_iota(jnp.int32, (nh, hd), 1)
        qt = jnp.concatenate([jnp.where(lane < ATT_DH, q8, 0.0), jnp.where(lane >= ATT_DH, q8, 0.0)], axis=0)
        qt_ref[...] = qt.astype(BF16)
        kn = jnp.concatenate([kn_ref[0], kn_ref[0]], axis=0)
        s_new = jnp.sum(qt * kn, axis=-1, keepdims=True)
        snew_ref[...] = s_new
        m_ref[...] = s_new

    @pl.when(ph == 0)
    def _():
        lane = lax.broadcasted_iota(jnp.int32, (nrow, plane), 1)
        row = lax.broadcasted_iota(jnp.int32, (nrow, plane), 0)
        own_head = (lane % nh) == (row % nh)
        for i in range(pps):
            page = p * pps + i
            kflat = kp_refs[i][0].reshape(plane, hd).astype(BF16)
            s = lax.dot_general(qt_ref[...], kflat, (((1,), (1,)), ((), ())), preferred_element_type=F32)
            dist = (past_len - page * PAGE_SIZE - lane // nh).astype(F32)
            s = jnp.where(own_head, s - slope_ref[...] * dist, -jnp.inf)
            s_ref[:, page_lanes(page)] = s
            m_ref[...] = jnp.maximum(m_ref[...], jnp.max(s, axis=-1, keepdims=True))

    @pl.when((ph == 1) & (p == 0))
    def _():
        m = m_ref[...]
        e_new = jnp.exp(snew_ref[...] - m)

        def expsum(g, part):
            for u in range(pps):
                e = jnp.exp(s_ref[:, page_lanes(g * pps + u)] - m)
                s_ref[:, page_lanes(g * pps + u)] = e
                for c in range(plane // LANES):
                    part = part + e[:, c * LANES:(c + 1) * LANES]
            return part

        part = lax.fori_loop(0, n_pages // pps, expsum, jnp.zeros((nrow, LANES), F32))
        den = jnp.sum(part, axis=-1, keepdims=True) + e_new
        inv = 1.0 / den
        lam = _lam(lq_ref[...]) + lam_init

        def combine(g, _):
            for u in range(pps):
                pn = s_ref[:, page_lanes(g * pps + u)] * inv
                a_ref[:, page_lanes(g * pps + u)] = pn[0:nh] - lam * pn[nh:nrow]
            return 0

        lax.fori_loop(0, n_pages // pps, combine, 0)
        pn_new = e_new * inv
        anew_ref[...] = pn_new[0:nh] - lam * pn_new[nh:nrow]
        acc_ref[...] = jnp.zeros_like(acc_ref)

    @pl.when(ph == 1)
    def _():
        lane = lax.broadcasted_iota(jnp.int32, (nh, LANES), 1)
        n_acc = acc_ref.shape[0]
        for i in range(pps):
            page = p * pps + i

            accs = [acc_ref[k] for k in range(n_acc)]
            for g in range(plane // LANES):
                av = a_ref[:, pl.ds(pl.multiple_of(page * plane + g * LANES, LANES), LANES)]
                for jj in range(per_vreg):
                    sel = (lane >= jj * nh) & (lane < (jj + 1) * nh)
                    wcol = jnp.sum(jnp.where(sel, av, 0.0), axis=-1, keepdims=True)
                    accs[jj % n_acc] = accs[jj % n_acc] + wcol * vp_refs[i][0, g * per_vreg + jj]
            for k in range(n_acc):
                acc_ref[k] = accs[k]

    @pl.when((ph == 1) & (p == n_steps - 1))
    def _():
        o = anew_ref[...] * vn_ref[0]
        for k in range(acc_ref.shape[0]):
            o = o + acc_ref[k]
        o = o * lax.rsqrt(jnp.mean(o * o, axis=-1, keepdims=True) + RMS_EPS) * w_ref[...] * (1.0 - lam_init)
        o_ref[0] = o.astype(BF16)


def _attn_decode(q, k_new, v_new, cache_k, cache_v, page_table, lambda_qk, subln_w, slopes, lam_init):
    bd = q.shape[0]
    n_pages = page_table.shape[1]
    nh = ATT_HEADS
    nrow = 2 * nh
    hd = 2 * ATT_DH
    pps = max(c for c in (1, 2, 4, 8, 16) if n_pages % c == 0)
    n_steps = n_pages // pps
    kp = cache_k.reshape(-1, PAGE_SIZE, nh, hd)
    vp = cache_v.reshape(-1, PAGE_SIZE, nh, hd)
    slope_rows = jnp.tile(slopes, 2).reshape(nrow, 1)
    past_len = n_pages * PAGE_SIZE
    kern = functools.partial(_attn_decode_kernel, pps=pps, past_len=past_len, lam_init=lam_init)
    head3 = lambda b, ph, p, pt: (b, 0, 0)
    const2 = lambda b, ph, p, pt: (0, 0)

    def kpage(i):
        return lambda b, ph, p, pt: (pt[b * n_pages + jnp.where(ph == 0, p, n_steps - 1) * pps + i], 0, 0, 0)

    def vpage(i):
        return lambda b, ph, p, pt: (pt[b * n_pages + jnp.where(ph == 0, 0, p) * pps + i], 0, 0, 0)

    page_block = (1, PAGE_SIZE, nh, hd)
    n_acc = 4
    grid_spec = pltpu.PrefetchScalarGridSpec(
        num_scalar_prefetch=1,
        grid=(bd, 2, n_steps),
        in_specs=[pl.BlockSpec((1, nh, hd), head3),
                  pl.BlockSpec((1, nh, hd), head3),
                  pl.BlockSpec((1, nh, hd), head3),
                  pl.BlockSpec((nrow, 1), const2),
                  pl.BlockSpec((4, ATT_DH), const2),
                  pl.BlockSpec((1, hd), const2)]
                 + [pl.BlockSpec(page_block, kpage(i)) for i in range(pps)]
                 + [pl.BlockSpec(page_block, vpage(i)) for i in range(pps)],
        out_specs=pl.BlockSpec((1, nh, hd), head3),
        scratch_shapes=[pltpu.VMEM((nrow, hd), BF16), pltpu.VMEM((nrow, past_len * nh), F32),
                        pltpu.VMEM((nh, past_len * nh), F32), pltpu.VMEM((nrow, 1), F32),
                        pltpu.VMEM((nrow, 1), F32), pltpu.VMEM((nh, 1), F32),
                        pltpu.VMEM((n_acc, nh, hd), F32)],
    )
    out = pl.pallas_call(
        kern,
        grid_spec=grid_spec,
        out_shape=jax.ShapeDtypeStruct((bd, nh, hd), BF16),
        compiler_params=_params(("arbitrary", "arbitrary", "arbitrary")),
    )(page_table.reshape(-1), q.reshape(bd, nh, hd), k_new.reshape(bd, nh, hd), v_new.reshape(bd, nh, hd),
      slope_rows, lambda_qk, subln_w.reshape(1, hd), *([kp] * pps), *([vp] * pps))
    return out.reshape(bd, ATT_WIDTH)


def _gated_group_norm(y, z, w):
    yg = y * _silu(z)
    gw = SSD_WIDTH // SSD_GROUPS
    parts = []
    for g in range(SSD_GROUPS):
        v = yg[:, g * gw:(g + 1) * gw]
        parts.append(v * lax.rsqrt(jnp.mean(v * v, axis=-1, keepdims=True) + RMS_EPS) * w[:, g * gw:(g + 1) * gw])
    return parts


def _ssd_prompt_kernel(xbc_ref, dt_ref, dtt_ref, z_ref, cw_ref, cb_ref, dtb_ref, dtbt_ref, al_ref, alt_ref,
                       dsk_ref, nw_ref, h0_ref, c0_ref, y_ref, st_ref, cv_ref, xpad_ref, ysc_ref, xdd_ref):
    c = pl.program_id(1)
    nc = pl.num_programs(1)
    L = SSD_CHUNK
    P = SSD_HEADDIM
    hpg = SSD_HEADS // SSD_GROUPS
    halo = 8

    @pl.when(c == 0)
    def _():
        st_ref[...] = h0_ref[...]
        xpad_ref[0:halo, :] = jnp.zeros((halo, CONV_DIM), F32)
        xpad_ref[halo - (CONV_W - 1):halo, :] = c0_ref[0]

    xc = xbc_ref[0]
    xpad_ref[halo:halo + L, :] = xc
    conv = cb_ref[...]
    for i in range(CONV_W - 1):
        sh = CONV_W - 1 - i
        conv = conv + xpad_ref[halo - sh:halo - sh + L, :] * cw_ref[i:i + 1, :]
    conv = conv + xc * cw_ref[CONV_W - 1:CONV_W, :]
    tail = xc[L - (CONV_W - 1):L, :]
    xpad_ref[halo - (CONV_W - 1):halo, :] = tail

    @pl.when(c == nc - 1)
    def _():
        cv_ref[0] = tail

    act = _silu(conv)
    xs = act[:, :SSD_WIDTH]
    bmat = [act[:, SSD_WIDTH + g * D_STATE:SSD_WIDTH + (g + 1) * D_STATE].astype(BF16) for g in range(SSD_GROUPS)]
    c_off = SSD_WIDTH + SSD_GROUPS * D_STATE
    cmat = [act[:, c_off + g * D_STATE:c_off + (g + 1) * D_STATE].astype(BF16) for g in range(SSD_GROUPS)]

    dtp = _softplus(dt_ref[0] + dtb_ref[...])
    dtpt = _softplus(dtt_ref[0] + dtbt_ref[...])
    da = dtp * (-jnp.exp(al_ref[...]))
    dat = dtpt * (-jnp.exp(alt_ref[...]))
    ri = lax.broadcasted_iota(jnp.int32, (L, L), 0)
    ci = lax.broadcasted_iota(jnp.int32, (L, L), 1)
    causal = ri >= ci
    tri = jnp.where(causal, 1.0, 0.0).astype(BF16)
    trit = jnp.where(ci >= ri, 1.0, 0.0).astype(BF16)
    cs = sum(jnp.dot(tri, part, preferred_element_type=F32) for part in _split3(da))
    cst = sum(jnp.dot(part, trit, preferred_element_type=F32) for part in _split3(dat))

    cb = [lax.dot_general(cmat[g], bmat[g], (((1,), (1,)), ((), ())), preferred_element_type=F32)
          for g in range(SSD_GROUPS)]
    dsk = dsk_ref[...]

    for h in range(SSD_HEADS):
        g = h // hpg
        cs_col = cs[:, h:h + 1]
        diff = cs_col - cst[h:h + 1, :]
        lmat = jnp.exp(jnp.where(causal, diff, -jnp.inf))
        mmat = (cb[g] * lmat).astype(BF16)
        xs_h = xs[:, h * P:(h + 1) * P]
        xd_h = xs_h * dtp[:, h:h + 1]
        y = jnp.dot(mmat, xd_h.astype(BF16), preferred_element_type=F32)
        st = st_ref[0, h]
        yoff = lax.dot_general(cmat[g], st.astype(BF16), (((1,), (1,)), ((), ())), preferred_element_type=F32)
        y = y + jnp.exp(cs_col) * yoff + dsk[:, h:h + 1] * xs_h
        ysc_ref[:, h * P:(h + 1) * P] = y
        cs_last = cs[L - 1:L, h:h + 1]
        xdd_ref[:, h * P:(h + 1) * P] = xd_h * jnp.exp(cs_last - cs_col)

    xddt = xdd_ref[...].T
    for h in range(SSD_HEADS):
        g = h // hpg
        new = jnp.dot(xddt[h * P:(h + 1) * P, :].astype(BF16), bmat[g], preferred_element_type=F32)
        cs_last = cs[L - 1:L, h:h + 1]
        st_ref[0, h] = jnp.exp(cs_last) * st_ref[0, h] + new

    parts = _gated_group_norm(ysc_ref[...], z_ref[0], nw_ref[...])
    gw = SSD_WIDTH // SSD_GROUPS
    for g in range(SSD_GROUPS):
        y_ref[0, :, g * gw:(g + 1) * gw] = parts[g].astype(BF16)


def _ssd_prompt(xbc, dt, z, h0, conv0, conv_w, conv_b, dt_bias, a_log, d_skip, norm_w):
    b, t, _ = xbc.shape
    L = SSD_CHUNK
    nc = t // L
    dtt = jnp.swapaxes(dt, 1, 2)
    seq = lambda bi, ci: (bi, ci, 0)
    const2 = lambda bi, ci: (0, 0)
    full2 = lambda shp: pl.BlockSpec(shp, const2)
    return pl.pallas_call(
        _ssd_prompt_kernel,
        grid=(b, nc),
        in_specs=[pl.BlockSpec((1, L, CONV_DIM), seq),
                  pl.BlockSpec((1, L, SSD_HEADS), seq),
                  pl.BlockSpec((1, SSD_HEADS, L), lambda bi, ci: (bi, 0, ci)),
                  pl.BlockSpec((1, L, SSD_WIDTH), seq),
                  full2((CONV_W, CONV_DIM)), full2((1, CONV_DIM)),
                  full2((1, SSD_HEADS)), full2((SSD_HEADS, 1)),
                  full2((1, SSD_HEADS)), full2((SSD_HEADS, 1)),
                  full2((1, SSD_HEADS)), full2((1, SSD_WIDTH)),
                  pl.BlockSpec((1, SSD_HEADS, SSD_HEADDIM, D_STATE), lambda bi, ci: (bi, 0, 0, 0)),
                  pl.BlockSpec((1, CONV_W - 1, CONV_DIM), lambda bi, ci: (bi, 0, 0))],
        out_specs=[pl.BlockSpec((1, L, SSD_WIDTH), seq),
                   pl.BlockSpec((1, SSD_HEADS, SSD_HEADDIM, D_STATE), lambda bi, ci: (bi, 0, 0, 0)),
                   pl.BlockSpec((1, CONV_W - 1, CONV_DIM), lambda bi, ci: (bi, 0, 0))],
        out_shape=[jax.ShapeDtypeStruct((b, t, SSD_WIDTH), BF16),
                   jax.ShapeDtypeStruct((b, SSD_HEADS, SSD_HEADDIM, D_STATE), F32),
                   jax.ShapeDtypeStruct((b, CONV_W - 1, CONV_DIM), F32)],
        scratch_shapes=[pltpu.VMEM((8 + L, CONV_DIM), F32), pltpu.VMEM((L, SSD_WIDTH), F32),
                        pltpu.VMEM((L, SSD_WIDTH), F32)],
        compiler_params=_params(("arbitrary", "arbitrary")),
    )(xbc, dt, dtt, z, conv_w, conv_b.reshape(1, -1), dt_bias.reshape(1, -1), dt_bias.reshape(-1, 1),
      a_log.reshape(1, -1), a_log.reshape(-1, 1), d_skip.reshape(1, -1), norm_w.reshape(1, -1), h0, conv0)


def _bf16_round(v):
    return v.astype(BF16).astype(F32)


def _ssd_sample_kernel(xbc_ref, ci_ref, dt_ref, z_ref, cw_ref, cb_ref, dtb_ref, al_ref, dsk_ref, nw_ref, h0_ref,
                       xbc8_ref, ci8_ref, dt8_ref, cw8_ref, cb8_ref, dtb8_ref, y_ref, st_ref, cv_ref, ysc_ref):
    P = SSD_HEADDIM
    hpg = SSD_HEADS // SSD_GROUPS
    xrow = xbc_ref[0]
    hist = ci_ref[0]
    conv = cb_ref[...]
    for i in range(CONV_W - 1):
        conv = conv + hist[i:i + 1, :] * cw_ref[i:i + 1, :]
    conv = conv + xrow * cw_ref[CONV_W - 1:CONV_W, :]
    cv_ref[0, 0:CONV_W - 2, :] = hist[1:CONV_W - 1, :]
    cv_ref[0, CONV_W - 2:CONV_W - 1, :] = xrow
    act = _silu(conv)
    xs = act[:, :SSD_WIDTH]
    c_off = SSD_WIDTH + SSD_GROUPS * D_STATE
    dtp = _softplus(dt_ref[0] + dtb_ref[...])
    decay = jnp.exp(dtp * (-jnp.exp(al_ref[...])))
    dsk = dsk_ref[...]

    nx = SSD_WIDTH // LANES
    conv8 = cb8_ref[0:nx, :]
    for i in range(CONV_W - 1):
        conv8 = conv8 + ci8_ref[0, i, 0:nx, :] * cw8_ref[i, 0:nx, :]
    conv8 = conv8 + xbc8_ref[0, 0:nx, :] * cw8_ref[CONV_W - 1, 0:nx, :]
    xd8 = _silu(conv8) * _softplus(dt8_ref[0] + dtb8_ref[...])
    xdt = xd8.T

    brows = [act[:, SSD_WIDTH + g * D_STATE:SSD_WIDTH + (g + 1) * D_STATE] for g in range(SSD_GROUPS)]
    crows = [act[:, c_off + g * D_STATE:c_off + (g + 1) * D_STATE] for g in range(SSD_GROUPS)]
    cbs = [jnp.sum(_bf16_round(brows[g]) * _bf16_round(crows[g]), axis=-1, keepdims=True) for g in range(SSD_GROUPS)]
    c8s = [jnp.broadcast_to(crows[g], (8, D_STATE)).astype(BF16) for g in range(SSD_GROUPS)]
    hpr = LANES // P
    for h in range(SSD_HEADS):
        g = h // hpg
        xs_h = xs[:, h * P:(h + 1) * P]
        xd_h = xs_h * dtp[:, h:h + 1]
        xcol = xdt[(h % hpr) * P:(h % hpr + 1) * P, h // hpr:h // hpr + 1]
        dec = decay[:, h:h + 1]
        h0q = h0_ref[0, h].astype(BF16)
        st_ref[0, h] = _bf16_round(dec) * h0q.astype(F32) + _bf16_round(xcol * brows[g])
        yoff = lax.dot_general(c8s[g], h0q, (((1,), (1,)), ((), ())), preferred_element_type=F32)[0:1, :]
        ysc_ref[:, h * P:(h + 1) * P] = (cbs[g] * xd_h + dec * yoff) + dsk[:, h:h + 1] * xs_h
    parts = _gated_group_norm(ysc_ref[...], z_ref[0], nw_ref[...])
    gw = SSD_WIDTH // SSD_GROUPS
    for g in range(SSD_GROUPS):
        y_ref[0, :, g * gw:(g + 1) * gw] = parts[g].astype(BF16)


def _ssd_sample(xbc, dt, z, h0, conv0, conv_w, conv_b, dt_bias, a_log, d_skip, norm_w):
    bd = xbc.shape[0]
    nr = CONV_DIM // LANES
    nx = SSD_WIDTH // LANES
    rep = SSD_HEADDIM
    row3 = lambda b: (b, 0, 0)
    const2 = lambda b: (0, 0)
    full2 = lambda shp: pl.BlockSpec(shp, const2)
    state = pl.BlockSpec((1, SSD_HEADS, SSD_HEADDIM, D_STATE), lambda b: (b, 0, 0, 0))
    y, st, cv = pl.pallas_call(
        _ssd_sample_kernel,
        grid=(bd,),
        in_specs=[pl.BlockSpec((1, 1, CONV_DIM), row3),
                  pl.BlockSpec((1, CONV_W - 1, CONV_DIM), row3),
                  pl.BlockSpec((1, 1, SSD_HEADS), row3),
                  pl.BlockSpec((1, 1, SSD_WIDTH), row3),
                  full2((CONV_W, CONV_DIM)), full2((1, CONV_DIM)), full2((1, SSD_HEADS)), full2((1, SSD_HEADS)),
                  full2((1, SSD_HEADS)), full2((1, SSD_WIDTH)), state,
                  pl.BlockSpec((1, nr, LANES), row3),
                  pl.BlockSpec((1, CONV_W - 1, nr, LANES), lambda b: (b, 0, 0, 0)),
                  pl.BlockSpec((1, nx, LANES), row3),
                  pl.BlockSpec((CONV_W, nr, LANES), lambda b: (0, 0, 0)),
                  full2((nr, LANES)), full2((nx, LANES))],
        out_specs=[pl.BlockSpec((1, 1, SSD_WIDTH), row3), state,
                   pl.BlockSpec((1, CONV_W - 1, CONV_DIM), row3)],
        out_shape=[jax.ShapeDtypeStruct((bd, 1, SSD_WIDTH), BF16),
                   jax.ShapeDtypeStruct((bd, SSD_HEADS, SSD_HEADDIM, D_STATE), F32),
                   jax.ShapeDtypeStruct((bd, CONV_W - 1, CONV_DIM), F32)],
        scratch_shapes=[pltpu.VMEM((1, SSD_WIDTH), F32)],
        compiler_params=_params(("arbitrary",)),
    )(xbc.reshape(bd, 1, CONV_DIM), conv0, dt.reshape(bd, 1, SSD_HEADS), z.reshape(bd, 1, SSD_WIDTH),
      conv_w, conv_b.reshape(1, -1), dt_bias.reshape(1, -1), a_log.reshape(1, -1), d_skip.reshape(1, -1),
      norm_w.reshape(1, -1), h0,
      xbc.reshape(bd, nr, LANES), conv0.reshape(bd, CONV_W - 1, nr, LANES),
      jnp.repeat(dt, rep, axis=-1).reshape(bd, nx, LANES), conv_w.reshape(CONV_W, nr, LANES),
      conv_b.reshape(nr, LANES), jnp.repeat(dt_bias, rep).reshape(nx, LANES))
    return y.reshape(bd, SSD_WIDTH), st, cv


def _outproj_kernel(att_ref, ssd_ref, x_ref, w_ref, g_ref, b_ref, wr_ref, br_ref, h_ref, route_ref, cnt_ref,
                    carry_ref, *, alpha, sub):
    i = pl.program_id(0)
    tm = x_ref.shape[0]

    @pl.when(i == 0)
    def _():
        carry_ref[...] = jnp.zeros_like(carry_ref)

    lane = lax.broadcasted_iota(jnp.int32, (sub, N_EXPERTS), 1).astype(F32)
    ri = lax.broadcasted_iota(jnp.int32, (sub, sub), 0)
    ci = lax.broadcasted_iota(jnp.int32, (sub, sub), 1)
    before = jnp.where(ci < ri, 1.0, 0.0).astype(BF16)
    olane = lax.broadcasted_iota(jnp.int32, (sub, LANES), 1)
    wrb = wr_ref[...].astype(BF16)
    carry = carry_ref[...]
    for r0 in range(0, tm, sub):
        rows = pl.ds(r0, sub)
        mix = jnp.dot(att_ref[rows, :], w_ref[0:ATT_WIDTH, :], preferred_element_type=F32)
        mix = mix + jnp.dot(ssd_ref[rows, :], w_ref[ATT_WIDTH:ATT_WIDTH + SSD_WIDTH, :], preferred_element_type=F32)
        hval = _layer_norm(alpha * x_ref[rows, :] + mix, g_ref[...], b_ref[...])
        h_ref[rows, :] = hval

        logits = jnp.dot(hval.astype(BF16), wrb, preferred_element_type=F32) + br_ref[...]
        work = logits
        chosen = jnp.zeros((sub, N_EXPERTS), F32)
        vals, idxs = [], []
        for _ in range(TOP_K):
            mk = jnp.max(work, axis=-1, keepdims=True)
            ik = jnp.min(jnp.where(work == mk, lane, float(N_EXPERTS)), axis=-1, keepdims=True)
            sel = lane == ik
            work = jnp.where(sel, -jnp.inf, work)
            chosen = jnp.where(sel, 1.0, chosen)
            vals.append(mk)
            idxs.append(ik)
        es = [jnp.exp(v - vals[0]) for v in vals]
        den = es[0] + es[1] + es[2] + es[3]

        prefix = jnp.dot(before, chosen.astype(BF16), preferred_element_type=F32) + carry
        carry = carry + jnp.sum(chosen, axis=0, keepdims=True)

        route = jnp.zeros((sub, LANES), F32)
        for k in range(TOP_K):
            rank_k = jnp.sum(jnp.where(lane == idxs[k], prefix, 0.0), axis=-1, keepdims=True)
            route = jnp.where(olane == k, idxs[k], route)
            route = jnp.where(olane == TOP_K + k, es[k] / den, route)
            route = jnp.where(olane == 2 * TOP_K + k, rank_k, route)
        route_ref[rows, :] = route
    carry_ref[...] = carry
    cnt_ref[...] = carry


def _outproj_router(att, ssd, x2d, w_out_bf16, ln_g, ln_b, w_router, b_router, alpha):
    m, d = x2d.shape
    sub = min(256, m)
    tm = min(2 * sub, m)
    row = lambda i: (i, 0)
    const = lambda i: (0, 0)
    kern = functools.partial(_outproj_kernel, alpha=alpha, sub=sub)
    return pl.pallas_call(
        kern,
        grid=(m // tm,),
        in_specs=[pl.BlockSpec((tm, ATT_WIDTH), row), pl.BlockSpec((tm, SSD_WIDTH), row), pl.BlockSpec((tm, d), row),
                  pl.BlockSpec((ATT_WIDTH + SSD_WIDTH, d), const, pipeline_mode=pl.Buffered(1)),
                  pl.BlockSpec((1, d), const), pl.BlockSpec((1, d), const),
                  pl.BlockSpec((d, N_EXPERTS), const), pl.BlockSpec((1, N_EXPERTS), const)],
        out_specs=[pl.BlockSpec((tm, d), row), pl.BlockSpec((tm, LANES), row), pl.BlockSpec((1, N_EXPERTS), const)],
        out_shape=[jax.ShapeDtypeStruct((m, d), F32), jax.ShapeDtypeStruct((m, LANES), F32),
                   jax.ShapeDtypeStruct((1, N_EXPERTS), F32)],
        scratch_shapes=[pltpu.VMEM((1, N_EXPERTS), F32)],
        compiler_params=_params(("arbitrary",)),
    )(att, ssd, x2d, w_out_bf16, ln_g.reshape(1, d), ln_b.reshape(1, d), w_router, b_router.reshape(1, -1))


def _row_copy(src, dst, sem):
    return pltpu.make_async_copy(src, dst, sem)


def _scatter_kernel(cnt_ref, pst_ref, dest_ref, h_ref, *rest, first):
    xrows_ref, zero_ref, stage_ref, sems, zsem = rest[-5:]
    i = pl.program_id(0)
    tm = h_ref.shape[0]

    @pl.when((i == 0) & first)
    def _():
        zero_ref[...] = jnp.zeros_like(zero_ref)

        def per_expert(e, _):
            n = cnt_ref[e]
            base = pst_ref[e]
            end = (n + MOE_SUB - 1) // MOE_SUB * MOE_SUB

            def start(r, _):
                _row_copy(zero_ref.at[pl.ds(0, 1)], xrows_ref.at[pl.ds(base + r, 1)], zsem).start()
                return 0

            def wait(r, _):
                _row_copy(zero_ref.at[pl.ds(0, 1)], xrows_ref.at[pl.ds(base + r, 1)], zsem).wait()
                return 0

            lax.fori_loop(n, end, start, 0)
            lax.fori_loop(n, end, wait, 0)
            return 0

        lax.fori_loop(0, N_EXPERTS, per_expert, 0)

    slot = i % 2
    stage_ref[slot] = h_ref[...]
    for t in range(tm):
        for k in range(TOP_K):
            d = dest_ref[0, 0, t * TOP_K + k]
            _row_copy(stage_ref.at[slot, pl.ds(t, 1)], xrows_ref.at[pl.ds(d, 1)],
                      sems.at[slot]).start(priority=(t * TOP_K + k) % 2)

    def drain(s):
        for t in range(tm):
            for k in range(TOP_K):
                _row_copy(stage_ref.at[s, pl.ds(t, 1)], xrows_ref.at[pl.ds(0, 1)], sems.at[s]).wait()

    @pl.when(i > 0)
    def _():
        drain(1 - slot)

    @pl.when(i == pl.num_programs(0) - 1)
    def _():
        drain(slot)


def _moe_scatter(h2d, dest, counts, pstart, n_rows, x_rows=None):
    m, d = h2d.shape
    tm = min(256, m)
    nt = m // tm
    dest3 = dest.reshape(nt, 1, tm * TOP_K)
    first = x_rows is None
    in_specs = [pl.BlockSpec((1, 1, tm * TOP_K), lambda i, c, p: (i, 0, 0), memory_space=pltpu.SMEM),
                pl.BlockSpec((tm, d), lambda i, c, p: (i, 0))]
    args = [counts, pstart, dest3, h2d]
    aliases = {}
    if not first:
        in_specs.append(pl.BlockSpec(memory_space=pl.ANY))
        args.append(x_rows)
        aliases = {len(args) - 1: 0}
    grid_spec = pltpu.PrefetchScalarGridSpec(
        num_scalar_prefetch=2,
        grid=(nt,),
        in_specs=in_specs,
        out_specs=pl.BlockSpec(memory_space=pl.ANY),
        scratch_shapes=[pltpu.VMEM((8, d), F32), pltpu.VMEM((2, tm, d), F32), pltpu.SemaphoreType.DMA((2,)),
                        pltpu.SemaphoreType.DMA(())],
    )
    return pl.pallas_call(
        functools.partial(_scatter_kernel, first=first),
        grid_spec=grid_spec,
        out_shape=jax.ShapeDtypeStruct((n_rows, d), F32),
        input_output_aliases=aliases,
        compiler_params=_params(("arbitrary",)),
    )(*args)


def _moe_mlp_kernel(ie_ref, ib_ref, iv_ref, x_ref, w1g_ref, w1l_ref, b1g_ref, b1l_ref, w2_ref, b2_ref, o_ref,
                    xb_ref, wg_ref, wl_ref, w2b_ref):
    i = pl.program_id(0)
    j = pl.program_id(1)
    nvalid = iv_ref[i]
    nsub = (nvalid + MOE_SUB - 1) // MOE_SUB
    d = x_ref.shape[1]

    def rows(s):
        return pl.ds(pl.multiple_of(s * MOE_SUB, MOE_SUB), MOE_SUB)

    @pl.when(nvalid > 0)
    def _():
        @pl.when(j == 0)
        def _():
            def init(s, _):
                xb_ref[rows(s), :] = x_ref[rows(s), :].astype(BF16)
                o_ref[rows(s), :] = jnp.broadcast_to(b2_ref[0], (MOE_SUB, d))
                return 0

            lax.fori_loop(0, nsub, init, 0)

        def hidden(s):
            xb = xb_ref[rows(s), :]
            return (jnp.dot(xb, wg_ref[...], preferred_element_type=F32),
                    jnp.dot(xb, wl_ref[...], preferred_element_type=F32))

        def finish(s, hid):
            glu = jnp.minimum(hid[0] + b1g_ref[0], SWIGLU_LIMIT)
            lin = jnp.clip(hid[1] + b1l_ref[0], -SWIGLU_LIMIT, SWIGLU_LIMIT)
            act = glu * _sigmoid(SWIGLU_ALPHA * glu) * (lin + 1.0)
            o_ref[rows(s), :] += jnp.dot(act.astype(BF16), w2b_ref[...], preferred_element_type=F32)

        def step(s, hid):
            nxt = hidden(s + 1)
            finish(s, hid)
            return nxt

        xb0 = xb_ref[rows(0), :]
        wg_ref[...] = w1g_ref[0].astype(BF16)
        glu0 = jnp.dot(xb0, wg_ref[...], preferred_element_type=F32)
        wl_ref[...] = w1l_ref[0].astype(BF16)
        lin0 = jnp.dot(xb0, wl_ref[...], preferred_element_type=F32)
        w2b_ref[...] = w2_ref[0].astype(BF16)
        hid = (glu0, lin0)
        hid = lax.fori_loop(0, nsub - 1, step, hid)
        finish(nsub - 1, hid)


def _moe_mlp(x_rows, item_e, item_blk, item_valid, w1, b1, w2, b2):
    n_rows, d = x_rows.shape
    n_items = item_e.shape[0]
    d_ff = w2.shape[1]
    tf = MOE_F_TILE
    nf = d_ff // tf
    tmr = MOE_ROW_TILE

    def jj(i, j, iv):
        return jnp.where(iv[i] > 0, j, nf - 1)

    grid_spec = pltpu.PrefetchScalarGridSpec(
        num_scalar_prefetch=3,
        grid=(n_items, nf),
        in_specs=[pl.BlockSpec((tmr, d), lambda i, j, ie, ib, iv: (ib[i], 0)),
                  pl.BlockSpec((1, d, tf), lambda i, j, ie, ib, iv: (ie[i], 0, jj(i, j, iv))),
                  pl.BlockSpec((1, d, tf), lambda i, j, ie, ib, iv: (ie[i], 0, nf + jj(i, j, iv))),
                  pl.BlockSpec((1, 1, tf), lambda i, j, ie, ib, iv: (ie[i], 0, jj(i, j, iv))),
                  pl.BlockSpec((1, 1, tf), lambda i, j, ie, ib, iv: (ie[i], 0, nf + jj(i, j, iv))),
                  pl.BlockSpec((1, tf, d), lambda i, j, ie, ib, iv: (ie[i], jj(i, j, iv), 0)),
                  pl.BlockSpec((1, 1, d), lambda i, j, ie, ib, iv: (ie[i], 0, 0))],
        out_specs=pl.BlockSpec((tmr, d), lambda i, j, ie, ib, iv: (ib[i], 0)),
        scratch_shapes=[pltpu.VMEM((tmr, d), BF16), pltpu.VMEM((d, tf), BF16), pltpu.VMEM((d, tf), BF16),
                        pltpu.VMEM((tf, d), BF16)],
    )
    return pl.pallas_call(
        _moe_mlp_kernel,
        grid_spec=grid_spec,
        out_shape=jax.ShapeDtypeStruct((n_rows, d), F32),
        compiler_params=_params(("arbitrary", "arbitrary")),
    )(item_e, item_blk, item_valid, x_rows, w1, w1, b1.reshape(N_EXPERTS, 1, -1), b1.reshape(N_EXPERTS, 1, -1),
      w2, b2.reshape(N_EXPERTS, 1, -1))


def _combine_kernel(dest_ref, nxt_ref, h_ref, route_ref, yrows_ref, g_ref, b_ref, o_ref, buf_ref, sems, *, alpha):
    i = pl.program_id(0)
    nt = pl.num_programs(0)
    tm = h_ref.shape[0]
    slot = i % 2

    def gather(idx_ref, s):
        for t in range(tm):
            for k in range(TOP_K):
                d = idx_ref[0, 0, t * TOP_K + k]
                _row_copy(yrows_ref.at[pl.ds(d, 1)], buf_ref.at[s, k, pl.ds(t, 1)],
                          sems.at[s]).start(priority=(t * TOP_K + k) % 2)

    @pl.when(i == 0)
    def _():
        gather(dest_ref, 0)

    @pl.when(i + 1 < nt)
    def _():
        gather(nxt_ref, 1 - slot)

    for t in range(tm):
        for k in range(TOP_K):
            _row_copy(yrows_ref.at[pl.ds(0, 1)], buf_ref.at[slot, k, pl.ds(t, 1)], sems.at[slot]).wait()
    route = route_ref[...]
    acc = route[:, TOP_K:TOP_K + 1] * buf_ref[slot, 0]
    for k in range(1, TOP_K):
        acc = acc + route[:, TOP_K + k:TOP_K + k + 1] * buf_ref[slot, k]
    o_ref[...] = _layer_norm(alpha * h_ref[...] + acc, g_ref[...], b_ref[...])


def _moe_combine(h2d, route, dest, y_rows, ln_g, ln_b, alpha):
    m, d = h2d.shape
    tm = min(128, m)
    nt = m // tm
    dest3 = dest.reshape(nt, 1, tm * TOP_K)
    kern = functools.partial(_combine_kernel, alpha=alpha)
    idx_block = (1, 1, tm * TOP_K)
    return pl.pallas_call(
        kern,
        grid=(nt,),
        in_specs=[pl.BlockSpec(idx_block, lambda i: (i, 0, 0), memory_space=pltpu.SMEM),
                  pl.BlockSpec(idx_block, lambda i: (jnp.minimum(i + 1, nt - 1), 0, 0), memory_space=pltpu.SMEM),
                  pl.BlockSpec((tm, d), lambda i: (i, 0)),
                  pl.BlockSpec((tm, LANES), lambda i: (i, 0)),
                  pl.BlockSpec(memory_space=pl.ANY),
                  pl.BlockSpec((1, d), lambda i: (0, 0)), pl.BlockSpec((1, d), lambda i: (0, 0))],
        out_specs=pl.BlockSpec((tm, d), lambda i: (i, 0)),
        out_shape=jax.ShapeDtypeStruct((m, d), F32),
        scratch_shapes=[pltpu.VMEM((2, TOP_K, tm, d), F32), pltpu.SemaphoreType.DMA((2,))],
        compiler_params=_params(("arbitrary",)),
    )(dest3, dest3, h2d, route, y_rows, ln_g.reshape(1, d), ln_b.reshape(1, d))


def _moe_ln2(groups, w1, b1, w2, b2, ln_g, ln_b, alpha):
    tmr = MOE_ROW_TILE
    group_counts = [c.reshape(-1).astype(jnp.int32) for _, _, c in groups]
    counts = sum(group_counts)
    tiles = (counts + tmr - 1) // tmr
    tile_end = jnp.cumsum(tiles)
    tile_start = tile_end - tiles
    pstart = (tile_start * tmr).astype(jnp.int32)
    n_tok = sum(h.shape[0] for h, _, _ in groups)
    n_items = -(-(n_tok * TOP_K) // tmr) + N_EXPERTS
    n_rows = n_items * tmr
    it = jnp.arange(n_items, dtype=jnp.int32)
    total = tile_end[-1]
    it_c = jnp.minimum(it, total - 1)
    item_e = jnp.minimum(jnp.searchsorted(tile_end, it_c, side='right'), N_EXPERTS - 1).astype(jnp.int32)
    item_r = it_c - tile_start[item_e]
    item_blk = (tile_start[item_e] + item_r).astype(jnp.int32)
    item_valid = jnp.where(it < total, jnp.clip(counts[item_e] - item_r * tmr, 0, tmr), 0).astype(jnp.int32)

    dests = []
    earlier = jnp.zeros_like(counts)
    x_rows = None
    for (h2d, route, _), gc in zip(groups, group_counts):
        idx = route[:, 0:TOP_K].astype(jnp.int32)
        rank = route[:, 2 * TOP_K:3 * TOP_K].astype(jnp.int32)
        dest = (pstart[idx] + earlier[idx] + rank).reshape(-1)
        dests.append(dest)
        earlier = earlier + gc
        x_rows = _moe_scatter(h2d, dest, counts, pstart, n_rows, x_rows)
    y_rows = _moe_mlp(x_rows, item_e, item_blk, item_valid, w1, b1, w2, b2)
    return [_moe_combine(h2d, route, dest, y_rows, ln_g, ln_b, alpha)
            for (h2d, route, _), dest in zip(groups, dests)]


def kernel(x_prompt, x_sample, cache_k, cache_v, page_table, state_ssm, state_conv, w_in, w_out, lambda_qk,
           attn_subln_w, conv_w, conv_b, dt_bias, a_log, d_skip, ssd_norm_w, ln1_g, ln1_b, w_router, b_router,
           w_mlp1, b_mlp1, w_mlp2, b_mlp2, ln2_g, ln2_b):
    bp, sp, d = x_prompt.shape
    bd, sd, _ = x_sample.shape
    depth = w_in.shape[0]
    assert depth == 1 and sd == 1, "kernel supports the single-layer, single-token-decode configuration"
    alpha = (2.0 * depth) ** 0.25
    slopes = 2.0 ** (-8.0 * jnp.arange(1, ATT_HEADS + 1, dtype=F32) / ATT_HEADS)
    l = 0
    lam_init = 0.8 - 0.6 * math.exp(-0.3 * l)

    w_in_b = _cast_bf16(w_in[l])
    w_out_b = _cast_bf16(w_out[l])
    moe_w = (w_mlp1[l], b_mlp1[l], w_mlp2[l], b_mlp2[l], ln2_g[l], ln2_b[l])
    ssd_w = (conv_w[l], conv_b[l], dt_bias[l], a_log[l], d_skip[l], ssd_norm_w[l])

    xp = x_prompt.reshape(bp * sp, d)
    q, k, v, z, xbc, dt = _inproj(xp, w_in_b, BF16)
    att = _attn_prompt(q.reshape(bp, sp, -1), k.reshape(bp, sp, -1), v.reshape(bp, sp, -1), lambda_qk[l],
                       attn_subln_w[l], slopes, lam_init)
    ssm_zero = jnp.zeros((bp, SSD_HEADS, SSD_HEADDIM, D_STATE), F32)
    conv_zero = jnp.zeros((bp, CONV_W - 1, CONV_DIM), F32)
    ssd, ssm_p, conv_p = _ssd_prompt(xbc.reshape(bp, sp, -1), dt.reshape(bp, sp, -1), z.reshape(bp, sp, -1),
                                     ssm_zero, conv_zero, *ssd_w)
    h, route, counts = _outproj_router(att.reshape(bp * sp, -1), ssd.reshape(bp * sp, -1), xp, w_out_b,
                                       ln1_g[l], ln1_b[l], w_router[l], b_router[l], alpha)

    xs = x_sample.reshape(bd, d)
    qs, ks, vs, zs, xbcs, dts = _inproj(xs, w_in_b, F32)
    att_s = _attn_decode(qs, ks, vs, cache_k, cache_v, page_table, lambda_qk[l], attn_subln_w[l], slopes,
                         lam_init)
    ssd_s, ssm_s, conv_s = _ssd_sample(xbcs, dts, zs, state_ssm[l], state_conv[l], *ssd_w)
    hs, route_s, counts_s = _outproj_router(att_s, ssd_s, xs, w_out_b, ln1_g[l], ln1_b[l], w_router[l],
                                            b_router[l], alpha)
    y_prompt, y_sample = _moe_ln2([(h, route, counts), (hs, route_s, counts_s)], *moe_w, alpha)
    y_prompt = y_prompt.reshape(bp, sp, d)
    y_sample = y_sample.reshape(bd, sd, d)

    hshape = (ATT_HEADS, 2 * ATT_DH)
    return (y_prompt, y_sample,
            k.reshape(1, bp, sp, *hshape), v.reshape(1, bp, sp, *hshape), ssm_p[None], conv_p[None],
            ks.reshape(1, bd, sd, *hshape), vs.reshape(1, bd, sd, *hshape), ssm_s[None], conv_s[None])
```
